```python
import jax, jax.numpy as jnp
from jax import lax
import numpy as np

D_MODEL = 4096
BATCH = 1
SEQ = 8192
DEPTH = 1
DEC_BATCH = 128
DEC_SEQ = 4
PAST_LEN = 2048
PAGE_SIZE = 128

R_HEADS = 8
R_DK = 256
R_DV = 256
R_CHUNK = 128
ROPE_BASE = 10000.0
A_HEADS = 16
A_KV = 4
A_DH = 128
A_HPG = A_HEADS // A_KV
L_CMP = 32
CMP_HID = 256
L_SEL = 64
N_SEL = 16
WINDOW = 512
Q_BLOCK = 128
FORCE_SCORE = 1e4
NEG = -1e30
N_MEM = 256
M_HEADS = 4
M_DH = 128
M_W = M_HEADS * M_DH
P_HEADS = 8
P_NKEYS = 128
P_EXPERTS = P_NKEYS * P_NKEYS
P_DQ = 256
P_TOPK = 16
P_BLOCK = 128
EPS = 1e-6

R_W = R_HEADS * R_DV
A_W = A_HEADS * A_DH
MIX_W = R_W + A_W
IN_SIZES = (R_HEADS * R_DK, R_HEADS * R_DK, R_W, R_W, A_W) + (A_KV * A_DH,) * 6 + (A_HEADS * 3,)

kernel_name = 'hymba_retention_nsa_peer_step'


def rmsnorm(x, g):
    xf = x.astype(jnp.float32)
    y = xf * lax.rsqrt(jnp.mean(xf * xf, axis=-1, keepdims=True) + EPS)
    return (y * g.astype(jnp.float32)).astype(x.dtype)


def rotary(x, pos):
    half = x.shape[-1] // 2
    inv = ROPE_BASE ** (-jnp.arange(half, dtype=jnp.float32) / half)
    ang = pos.astype(jnp.float32)[:, None] * inv[None, :]
    cos, sin = jnp.cos(ang)[:, None, :], jnp.sin(ang)[:, None, :]
    xf = x.astype(jnp.float32)
    x1, x2 = xf[..., :half], xf[..., half:]
    return jnp.concatenate([x1 * cos - x2 * sin, x1 * sin + x2 * cos], axis=-1).astype(x.dtype)


def masked_softmax(s, mask):
    p = jax.nn.softmax(jnp.where(mask, s, NEG), axis=-1)
    return jnp.where(mask, p, 0.0)


def split_mix(h, w_in):
    B, T, _ = h.shape
    cuts = np.cumsum(IN_SIZES)[:-1].tolist()
    rq, rk, rv, rg, aq, akc, avc, aks, avs, akw, avw, ag = jnp.split(h @ w_in, cuts, axis=-1)
    r = lambda a, n, d: a.reshape(B, T, n, d)
    return (r(rq, R_HEADS, R_DK), r(rk, R_HEADS, R_DK), r(rv, R_HEADS, R_DV), rg,
            r(aq, A_HEADS, A_DH), r(akc, A_KV, A_DH), r(avc, A_KV, A_DH), r(aks, A_KV, A_DH),
            r(avs, A_KV, A_DH), r(akw, A_KV, A_DH), r(avw, A_KV, A_DH),
            jax.nn.sigmoid(ag.astype(jnp.float32)).reshape(B, T, A_HEADS, 3))


def retention_chunk(S, q, k, v):
    q, k, v = q.astype(jnp.float32), k.astype(jnp.float32), v.astype(jnp.float32)
    C = q.shape[1]
    lg = jnp.log1p(-jnp.exp2(-5.0 - jnp.arange(R_HEADS, dtype=jnp.float32)))
    idx = jnp.arange(C, dtype=jnp.float32)
    rel = idx[:, None] - idx[None, :]
    dmask = jnp.where(rel[None] >= 0, jnp.exp(jnp.maximum(rel, 0.0)[None] * lg[:, None, None]), 0.0)
    inner = jnp.einsum('bihd,bjhd->bhij', q, k) * dmask[None]
    o = jnp.einsum('bhij,bjhe->bihe', inner, v)
    o = o + jnp.einsum('bihd,bhde->bihe', q, S) * jnp.exp((idx + 1.0)[:, None] * lg[None, :])[None, :, :, None]
    kdec = k * jnp.exp((C - 1.0 - idx)[:, None] * lg[None, :])[None, :, :, None]
    S_new = S * jnp.exp(C * lg)[None, :, None, None] + jnp.einsum('bjhd,bjhe->bhde', kdec, v)
    return S_new, o


def retention_prompt(q, k, v):
    B, T, H, _ = q.shape
    nch = T // R_CHUNK
    chunks = lambda a: jnp.moveaxis(a.reshape(B, nch, R_CHUNK, H, a.shape[-1]), 1, 0)
    S0 = jnp.zeros((B, H, R_DK, R_DV), jnp.float32)
    S, o = lax.scan(lambda s, xs: retention_chunk(s, *xs), S0, (chunks(q), chunks(k), chunks(v)))
    return jnp.moveaxis(o, 0, 1).reshape(B, T, H, R_DV), S


def compress(rows, pe, w1, b1, w2, b2):
    T = rows.shape[-3]
    nc = T // L_CMP
    lead = rows.shape[:-3]
    blk = rows[..., :nc * L_CMP, :, :].reshape(lead + (nc, L_CMP, A_KV, A_DH)) + pe[:, None, :]
    blk = jnp.swapaxes(blk, -3, -2).reshape(lead + (nc, A_KV, L_CMP * A_DH))
    return jax.nn.gelu(blk @ w1 + b1) @ w2 + b2


def nsa_core(q, q_pos, kc, vc, ks, vs, kw, vw, kw_pos, gates):
    nq = q.shape[0]
    qg = q.astype(jnp.float32).reshape(nq, A_KV, A_HPG, A_DH) * A_DH ** -0.5
    nc = kc.shape[0]
    c_end = (jnp.arange(nc) + 1) * L_CMP - 1
    c_mask = (c_end[None, :] <= q_pos[:, None])[:, None, None, :]
    p_c = masked_softmax(jnp.einsum('qghd,cgd->qghc', qg, kc.astype(jnp.float32)), c_mask)
    o_c = jnp.einsum('qghc,cgd->qghd', p_c, vc.astype(jnp.float32))
    nsb = ks.shape[1] // L_SEL
    ratio = L_SEL // L_CMP
    imp = jnp.pad(p_c, ((0, 0), (0, 0), (0, 0), (0, nsb * ratio - nc)))
    imp = imp.reshape(nq, A_KV, A_HPG, nsb, ratio).sum(axis=(2, 4))
    blk = jnp.arange(nsb)
    cur = (q_pos // L_SEL)[:, None, None]
    forced = (blk == 0) | (blk == cur) | (blk == cur - 1)
    imp = jnp.where(blk <= cur, jnp.where(forced, FORCE_SCORE, imp), NEG)
    _, sel = lax.top_k(imp, min(N_SEL, nsb))
    n_pick = sel.shape[-1]
    tok = (sel[..., None] * L_SEL + jnp.arange(L_SEL)).reshape(nq, A_KV, n_pick * L_SEL)
    gidx = jnp.arange(A_KV)[None, :, None]
    k_sel = ks[gidx, tok].astype(jnp.float32)
    v_sel = vs[gidx, tok].astype(jnp.float32)
    s_mask = (tok <= q_pos[:, None, None])[:, :, None, :]
    p_s = masked_softmax(jnp.einsum('qghd,qgmd->qghm', qg, k_sel), s_mask)
    o_s = jnp.einsum('qghm,qgmd->qghd', p_s, v_sel)
    rel = q_pos[:, None] - kw_pos[None, :]
    w_mask = ((rel >= 0) & (rel < WINDOW) & (kw_pos[None, :] >= 0))[:, None, None, :]
    p_w = masked_softmax(jnp.einsum('qghd,kgd->qghk', qg, kw.astype(jnp.float32)), w_mask)
    o_w = jnp.einsum('qghk,kgd->qghd', p_w, vw.astype(jnp.float32))
    g = gates.astype(jnp.float32).reshape(nq, A_KV, A_HPG, 3, 1)
    o = o_c * g[..., 0, :] + o_s * g[..., 1, :] + o_w * g[..., 2, :]
    return o.reshape(nq, A_HEADS, A_DH).astype(q.dtype)


def nsa_prompt(q, kc, vc, ks, vs, kw, vw, gates):
    B, T = q.shape[:2]
    nsb = -(-T // L_SEL)
    pad_sel = ((0, 0), (0, nsb * L_SEL - T), (0, 0), (0, 0))
    ks_g = jnp.swapaxes(jnp.pad(ks, pad_sel), 1, 2)
    vs_g = jnp.swapaxes(jnp.pad(vs, pad_sel), 1, 2)
    pad_win = ((0, 0), (WINDOW, 0), (0, 0), (0, 0))
    kw_p, vw_p = jnp.pad(kw, pad_win), jnp.pad(vw, pad_win)
    n_qb = T // Q_BLOCK

    def per_seq(args):
        q_b, kc_b, vc_b, ks_b, vs_b, kw_b, vw_b, g_b = args

        def per_block(i):
            start = i * Q_BLOCK
            sl = lambda a, n: lax.dynamic_slice_in_dim(a, start, n, 0)
            q_pos = start + jnp.arange(Q_BLOCK)
            kw_pos = start - WINDOW + jnp.arange(Q_BLOCK + WINDOW)
            return nsa_core(sl(q_b, Q_BLOCK), q_pos, kc_b, vc_b, ks_b, vs_b,
                            sl(kw_b, Q_BLOCK + WINDOW), sl(vw_b, Q_BLOCK + WINDOW), kw_pos, sl(g_b, Q_BLOCK))

        return lax.map(per_block, jnp.arange(n_qb)).reshape(T, A_HEADS, A_DH)

    return lax.map(per_seq, (q, kc, vc, ks_g, vs_g, kw_p, vw_p, gates))


def nsa_sample(q, kcn, vcn, ksn, vsn, kwn, vwn, gates, pool_kc, pool_vc, pool_ks, pool_vs,
               buf_kw, buf_vw, page_table, cmp_k, cmp_v, g_kc):
    S = q.shape[1]
    past = page_table.shape[1] * PAGE_SIZE
    T = past + S
    nsb = -(-T // L_SEL)
    wb = buf_kw.shape[1]
    q_pos = past + jnp.arange(S)
    kw_pos = jnp.concatenate([past - wb + jnp.arange(wb), q_pos])

    def rows(pool, pt, new):
        old = pool[pt].reshape((past,) + pool.shape[2:])
        return jnp.concatenate([old, new.astype(pool.dtype)], axis=0)

    def per_seq(args):
        q_b, kc_b, vc_b, ks_b, vs_b, kw_b, vw_b, g_b, kwb, vwb, pt = args
        kc = rmsnorm(compress(rows(pool_kc, pt, kc_b), *cmp_k), g_kc)
        vc = compress(rows(pool_vc, pt, vc_b), *cmp_v)
        pad = ((0, nsb * L_SEL - T), (0, 0), (0, 0))
        ks = jnp.swapaxes(jnp.pad(rows(pool_ks, pt, ks_b), pad), 0, 1)
        vs = jnp.swapaxes(jnp.pad(rows(pool_vs, pt, vs_b), pad), 0, 1)
        kw = jnp.concatenate([kwb, kw_b.astype(kwb.dtype)], axis=0)
        vw = jnp.concatenate([vwb, vw_b.astype(vwb.dtype)], axis=0)
        return nsa_core(q_b, q_pos, kc, vc, ks, vs, kw, vw, kw_pos, g_b)

    return lax.map(per_seq, (q, kcn, vcn, ksn, vsn, kwn, vwn, gates, buf_kw, buf_vw, page_table))


def merge_groups(ret_o, rg, nsa_o, r_gn, w_out):
    B, T = rg.shape[:2]
    r = rmsnorm(ret_o, r_gn).reshape(B, T, R_W) * jax.nn.silu(rg.astype(jnp.float32))
    mix = jnp.concatenate([r.astype(rg.dtype), nsa_o.reshape(B, T, A_W).astype(rg.dtype)], axis=-1)
    return mix @ w_out


def memory_kv(mem, g_memtok, w_mk, w_mv, m_kn):
    B, M, _ = mem.shape
    m = rmsnorm(mem, g_memtok)
    k = rmsnorm((m @ w_mk).reshape(B, M, M_HEADS, M_DH), m_kn)
    v = (m @ w_mv).reshape(B, M, M_HEADS, M_DH)
    return k, v


def memory_attend(h, mk, mv, w_mq, m_qn, w_mo):
    B, T, _ = h.shape
    q = rmsnorm((h @ w_mq).reshape(B, T, M_HEADS, M_DH), m_qn)
    s = jnp.einsum('bthd,bmhd->bhtm', q.astype(jnp.float32), mk.astype(jnp.float32)) * M_DH ** -0.5
    p = jax.nn.softmax(s, axis=-1)
    o = jnp.einsum('bhtm,bmhd->bthd', p, mv.astype(jnp.float32)).astype(h.dtype)
    return o.reshape(B, T, M_W) @ w_mo


def peer(h, w_pq, sub_k1, sub_k2, exp_u, exp_v):
    lead = h.shape[:-1]
    xt = h.reshape(-1, D_MODEL)
    n = xt.shape[0]
    nb = -(-n // P_BLOCK)
    xb = jnp.pad(xt, ((0, nb * P_BLOCK - n), (0, 0))).reshape(nb, P_BLOCK, D_MODEL)
    k1, k2 = sub_k1.astype(jnp.float32), sub_k2.astype(jnp.float32)

    def block(x):
        q = (x @ w_pq).astype(jnp.float32).reshape(P_BLOCK, P_HEADS, 2, P_DQ // 2)
        s1 = jnp.einsum('phd,hkd->phk', q[:, :, 0], k1)
        s2 = jnp.einsum('phd,hkd->phk', q[:, :, 1], k2)
        v1, i1 = lax.top_k(s1, P_TOPK)
        v2, i2 = lax.top_k(s2, P_TOPK)
        cand = (v1[..., :, None] + v2[..., None, :]).reshape(P_BLOCK, P_HEADS, P_TOPK * P_TOPK)
        cidx = (i1[..., :, None] * P_NKEYS + i2[..., None, :]).reshape(P_BLOCK, P_HEADS, P_TOPK * P_TOPK)
        best, slot = lax.top_k(cand, P_TOPK)
        eidx = jnp.take_along_axis(cidx, slot, axis=-1)
        gate = jax.nn.softmax(best, axis=-1)
        act = jax.nn.gelu(jnp.einsum('phkd,pd->phk', exp_u[eidx], x).astype(jnp.float32))
        return jnp.einsum('phk,phkd->pd', (gate * act).astype(x.dtype), exp_v[eidx])

    y = lax.map(block, xb).reshape(nb * P_BLOCK, D_MODEL)[:n]
    return y.reshape(lead + (D_MODEL,))


def setup_inputs(seed: int = 0) -> dict:
    key = jax.random.key(seed)
    keys = iter(jax.random.split(key, 64))

    def nrm(shape, scale):
        return jax.random.normal(next(keys), shape, jnp.float32) * scale

    def gain(shape):
        return 1.0 + nrm(shape, 0.05)

    n_pages = PAST_LEN // PAGE_SIZE
    n_used = DEC_BATCH * n_pages
    n_pool = n_used + max(1, n_used // 4)
    perm = jax.random.permutation(next(keys), n_pool).astype(jnp.int32)
    page_table = perm[:n_used].reshape(DEC_BATCH, n_pages)
    wb = min(WINDOW, PAST_LEN)
    pool = (n_pool, PAGE_SIZE, A_KV, A_DH)
    in_total = sum(IN_SIZES)
    cmp_in = L_CMP * A_DH
    return {
        'x_prompt': nrm((BATCH, SEQ, D_MODEL), 1.0),
        'x_sample': nrm((DEC_BATCH, DEC_SEQ, D_MODEL), 1.0),
        'mem_prompt': nrm((BATCH, N_MEM, D_MODEL), 1.0),
        'cache_k_cmp': nrm(pool, 1.0),
        'cache_v_cmp': nrm(pool, 1.0),
        'cache_k_sel': nrm(pool, 1.0),
        'cache_v_sel': nrm(pool, 1.0),
        'state_k_win': nrm((DEC_BATCH, wb, A_KV, A_DH), 1.0),
        'state_v_win': nrm((DEC_BATCH, wb, A_KV, A_DH), 1.0),
        'state_ret': nrm((DEC_BATCH, R_HEADS, R_DK, R_DV), 0.3),
        'cache_mem_k': nrm((DEC_BATCH, N_MEM, M_HEADS, M_DH), 1.0),
        'cache_mem_v': nrm((DEC_BATCH, N_MEM, M_HEADS, M_DH), 1.0),
        'page_table': page_table,
        'g_mix': gain((D_MODEL,)),
        'w_in': nrm((D_MODEL, in_total), D_MODEL ** -0.5),
        'r_gn': gain((R_HEADS, R_DV)),
        'a_qn': gain((A_DH,)),
        'a_kcn': gain((A_DH,)),
        'a_ksn': gain((A_DH,)),
        'a_kwn': gain((A_DH,)),
        'cmp_pe_k': nrm((L_CMP, A_DH), 0.1),
        'cmp_w1_k': nrm((cmp_in, CMP_HID), cmp_in ** -0.5),
        'cmp_b1_k': nrm((CMP_HID,), 0.01),
        'cmp_w2_k': nrm((CMP_HID, A_DH), CMP_HID ** -0.5),
        'cmp_b2_k': nrm((A_DH,), 0.01),
        'cmp_pe_v': nrm((L_CMP, A_DH), 0.1),
        'cmp_w1_v': nrm((cmp_in, CMP_HID), cmp_in ** -0.5),
        'cmp_b1_v': nrm((CMP_HID,), 0.01),
        'cmp_w2_v': nrm((CMP_HID, A_DH), CMP_HID ** -0.5),
        'cmp_b2_v': nrm((A_DH,), 0.01),
        'w_out': nrm((MIX_W, D_MODEL), MIX_W ** -0.5),
        'g_xattn': gain((D_MODEL,)),
        'g_memtok': gain((D_MODEL,)),
        'w_mq': nrm((D_MODEL, M_W), D_MODEL ** -0.5),
        'w_mk': nrm((D_MODEL, M_W), D_MODEL ** -0.5),
        'w_mv': nrm((D_MODEL, M_W), D_MODEL ** -0.5),
        'm_qn': gain((M_DH,)),
        'm_kn': gain((M_DH,)),
        'w_mo': nrm((M_W, D_MODEL), M_W ** -0.5),
        'g_ffn': gain((D_MODEL,)),
        'w_pq': nrm((D_MODEL, P_HEADS * P_DQ), D_MODEL ** -0.5),
        'sub_k1': nrm((P_HEADS, P_NKEYS, P_DQ // 2), (P_DQ // 2) ** -0.5),
        'sub_k2': nrm((P_HEADS, P_NKEYS, P_DQ // 2), (P_DQ // 2) ** -0.5),
        'exp_u': nrm((P_EXPERTS, D_MODEL), D_MODEL ** -0.5),
        'exp_v': nrm((P_EXPERTS, D_MODEL), 0.5),
    }


def reference(x_prompt, x_sample, mem_prompt, cache_k_cmp, cache_v_cmp, cache_k_sel, cache_v_sel,
              state_k_win, state_v_win, state_ret, cache_mem_k, cache_mem_v, page_table,
              g_mix, w_in, r_gn, a_qn, a_kcn, a_ksn, a_kwn,
              cmp_pe_k, cmp_w1_k, cmp_b1_k, cmp_w2_k, cmp_b2_k,
              cmp_pe_v, cmp_w1_v, cmp_b1_v, cmp_w2_v, cmp_b2_v,
              w_out, g_xattn, g_memtok, w_mq, w_mk, w_mv, m_qn, m_kn, w_mo,
              g_ffn, w_pq, sub_k1, sub_k2, exp_u, exp_v):
    cmp_k = (cmp_pe_k, cmp_w1_k, cmp_b1_k, cmp_w2_k, cmp_b2_k)
    cmp_v = (cmp_pe_v, cmp_w1_v, cmp_b1_v, cmp_w2_v, cmp_b2_v)
    xp, xs = x_prompt, x_sample
    for _ in range(DEPTH):
        T = xp.shape[1]
        rq, rk, rv, rg, aq, akc, avc, aks, avs, akw, avw, ag = split_mix(rmsnorm(xp, g_mix), w_in)
        pos = jnp.arange(T)
        ret_o, ret_p = retention_prompt(rotary(rq, pos) * R_DK ** -0.5, rotary(rk, pos), rv)
        aq, aks, akw = rmsnorm(aq, a_qn), rmsnorm(aks, a_ksn), rmsnorm(akw, a_kwn)
        kc = rmsnorm(compress(akc, *cmp_k), a_kcn)
        vc = compress(avc, *cmp_v)
        nsa_o = nsa_prompt(aq, kc, vc, aks, avs, akw, avw, ag)
        hp = xp + merge_groups(ret_o, rg, nsa_o, r_gn, w_out)
        mem_k_p, mem_v_p = memory_kv(mem_prompt, g_memtok, w_mk, w_mv, m_kn)
        hp = hp + memory_attend(rmsnorm(hp, g_xattn), mem_k_p, mem_v_p, w_mq, m_qn, w_mo)
        xp = hp + peer(rmsnorm(hp, g_ffn), w_pq, sub_k1, sub_k2, exp_u, exp_v)
        wl = min(WINDOW, T)
        k_cmp_p, v_cmp_p, k_sel_p, v_sel_p = akc, avc, aks, avs
        k_win_p, v_win_p = akw[:, T - wl:], avw[:, T - wl:]
        ret_p = ret_p.astype(xp.dtype)

        S = xs.shape[1]
        past = page_table.shape[1] * PAGE_SIZE
        rq, rk, rv, rg, aq, akc, avc, aks, avs, akw, avw, ag = split_mix(rmsnorm(xs, g_mix), w_in)
        pos = past + jnp.arange(S)
        ret_s, ret_o = retention_chunk(state_ret.astype(jnp.float32), rotary(rq, pos) * R_DK ** -0.5,
                                       rotary(rk, pos), rv)
        aq, aks, akw = rmsnorm(aq, a_qn), rmsnorm(aks, a_ksn), rmsnorm(akw, a_kwn)
        nsa_o = nsa_sample(aq, akc, avc, aks, avs, akw, avw, ag, cache_k_cmp, cache_v_cmp, cache_k_sel,
                           cache_v_sel, state_k_win, state_v_win, page_table, cmp_k, cmp_v, a_kcn)
        hs = xs + merge_groups(ret_o, rg, nsa_o, r_gn, w_out)
        hs = hs + memory_attend(rmsnorm(hs, g_xattn), cache_mem_k, cache_mem_v, w_mq, m_qn, w_mo)
        xs = hs + peer(rmsnorm(hs, g_ffn), w_pq, sub_k1, sub_k2, exp_u, exp_v)
        wb = state_k_win.shape[1]
        k_cmp_s, v_cmp_s, k_sel_s, v_sel_s = akc, avc, aks, avs
        k_win_s = jnp.concatenate([state_k_win, akw.astype(state_k_win.dtype)], axis=1)[:, -wb:]
        v_win_s = jnp.concatenate([state_v_win, avw.astype(state_v_win.dtype)], axis=1)[:, -wb:]
        ret_s = ret_s.astype(state_ret.dtype)
    return (xp, xs, k_cmp_p, v_cmp_p, k_sel_p, v_sel_p, k_win_p, v_win_p, ret_p, mem_k_p, mem_v_p,
            k_cmp_s, v_cmp_s, k_sel_s, v_sel_s, k_win_s, v_win_s, ret_s)
```

```python
import functools

import jax
import jax.numpy as jnp
import numpy as np
from jax import lax
from jax.experimental import pallas as pl
from jax.experimental.pallas import tpu as pltpu

D_MODEL = 4096
PAGE_SIZE = 128
R_HEADS = 8
R_DK = 256
R_DV = 256
R_CHUNK = 128
ROPE_BASE = 10000.0
A_HEADS = 16
A_KV = 4
A_DH = 128
A_HPG = A_HEADS // A_KV
L_CMP = 32
CMP_HID = 256
L_SEL = 64
N_SEL = 16
WINDOW = 512
Q_BLOCK = 128
FORCE_SCORE = 1e4
NEG = -1e30
M_HEADS = 4
M_DH = 128
M_W = M_HEADS * M_DH
P_HEADS = 8
P_NKEYS = 128
P_DQ = 256
P_TOPK = 16
P_BLOCK = 128
EPS = 1e-6
R_W = R_HEADS * R_DV
A_W = A_HEADS * A_DH
IN_SIZES = (R_HEADS * R_DK, R_HEADS * R_DK, R_W, R_W, A_W) + (A_KV * A_DH,) * 6 + (A_HEADS * 3,)

VMEM_LIMIT_BYTES = 56 * 1024 * 1024


def _mm_kernel(a_ref, b_ref, o_ref, acc_ref):
    k = pl.program_id(2)

    @pl.when(k == 0)
    def _():
        acc_ref[...] = jnp.zeros_like(acc_ref)

    acc_ref[...] += jnp.dot(a_ref[...].astype(jnp.bfloat16), b_ref[...].astype(jnp.bfloat16),
                            preferred_element_type=jnp.float32)

    @pl.when(k == pl.num_programs(2) - 1)
    def _():
        o_ref[...] = acc_ref[...]


def mm(a, b, tm=512, tn=512, tk=1024):
    m, kdim = a.shape
    _, n = b.shape
    tm = min(tm, m)
    tn = min(tn, n)
    tk = min(tk, kdim)
    assert m % tm == 0 and kdim % tk == 0
    grid = (m // tm, pl.cdiv(n, tn), kdim // tk)
    return pl.pallas_call(
        _mm_kernel,
        grid=grid,
        in_specs=[pl.BlockSpec((tm, tk), lambda i, j, k: (i, k)),
                  pl.BlockSpec((tk, tn), lambda i, j, k: (k, j))],
        out_specs=pl.BlockSpec((tm, tn), lambda i, j, k: (i, j)),
        out_shape=jax.ShapeDtypeStruct((m, n), jnp.float32),
        scratch_shapes=[pltpu.VMEM((tm, tn), jnp.float32)],
        compiler_params=pltpu.CompilerParams(
            dimension_semantics=("parallel", "parallel", "arbitrary"),
            vmem_limit_bytes=VMEM_LIMIT_BYTES),
        name="mm",
    )(a, b)


def mmnd(a, b):
    lead = a.shape[:-1]
    out = mm(a.reshape(-1, a.shape[-1]), b)
    return out.reshape(lead + (b.shape[-1],))


def rmsnorm(x, g):
    xf = x.astype(jnp.float32)
    y = xf * lax.rsqrt(jnp.mean(xf * xf, axis=-1, keepdims=True) + EPS)
    return (y * g.astype(jnp.float32)).astype(x.dtype)


def rotary(x, pos):
    half = x.shape[-1] // 2
    inv = ROPE_BASE ** (-jnp.arange(half, dtype=jnp.float32) / half)
    ang = pos.astype(jnp.float32)[:, None] * inv[None, :]
    cos, sin = jnp.cos(ang)[:, None, :], jnp.sin(ang)[:, None, :]
    xf = x.astype(jnp.float32)
    x1, x2 = xf[..., :half], xf[..., half:]
    return jnp.concatenate([x1 * cos - x2 * sin, x1 * sin + x2 * cos], axis=-1).astype(x.dtype)


def masked_softmax(s, mask):
    p = jax.nn.softmax(jnp.where(mask, s, NEG), axis=-1)
    return jnp.where(mask, p, 0.0)


def split_mix(h, w_in):
    B, T, _ = h.shape
    cuts = np.cumsum(IN_SIZES)[:-1].tolist()
    rq, rk, rv, rg, aq, akc, avc, aks, avs, akw, avw, ag = jnp.split(mmnd(h, w_in), cuts, axis=-1)
    r = lambda a, n, d: a.reshape(B, T, n, d)
    return (r(rq, R_HEADS, R_DK), r(rk, R_HEADS, R_DK), r(rv, R_HEADS, R_DV), rg,
            r(aq, A_HEADS, A_DH), r(akc, A_KV, A_DH), r(avc, A_KV, A_DH), r(aks, A_KV, A_DH),
            r(avs, A_KV, A_DH), r(akw, A_KV, A_DH), r(avw, A_KV, A_DH),
            jax.nn.sigmoid(ag.astype(jnp.float32)).reshape(B, T, A_HEADS, 3))


def retention_chunk(S, q, k, v):
    q, k, v = q.astype(jnp.float32), k.astype(jnp.float32), v.astype(jnp.float32)
    C = q.shape[1]
    lg = jnp.log1p(-jnp.exp2(-5.0 - jnp.arange(R_HEADS, dtype=jnp.float32)))
    idx = jnp.arange(C, dtype=jnp.float32)
    rel = idx[:, None] - idx[None, :]
    dmask = jnp.where(rel[None] >= 0, jnp.exp(jnp.maximum(rel, 0.0)[None] * lg[:, None, None]), 0.0)
    inner = jnp.einsum('bihd,bjhd->bhij', q, k) * dmask[None]
    o = jnp.einsum('bhij,bjhe->bihe', inner, v)
    o = o + jnp.einsum('bihd,bhde->bihe', q, S) * jnp.exp((idx + 1.0)[:, None] * lg[None, :])[None, :, :, None]
    kdec = k * jnp.exp((C - 1.0 - idx)[:, None] * lg[None, :])[None, :, :, None]
    S_new = S * jnp.exp(C * lg)[None, :, None, None] + jnp.einsum('bjhd,bjhe->bhde', kdec, v)
    return S_new, o


def retention_prompt(q, k, v):
    B, T, H, _ = q.shape
    nch = T // R_CHUNK
    chunks = lambda a: jnp.moveaxis(a.reshape(B, nch, R_CHUNK, H, a.shape[-1]), 1, 0)
    S0 = jnp.zeros((B, H, R_DK, R_DV), jnp.float32)
    S, o = lax.scan(lambda s, xs: retention_chunk(s, *xs), S0, (chunks(q), chunks(k), chunks(v)))
    return jnp.moveaxis(o, 0, 1).reshape(B, T, H, R_DV), S


def compress(rows, pe, w1, b1, w2, b2):
    T = rows.shape[-3]
    nc = T // L_CMP
    lead = rows.shape[:-3]
    blk = rows[..., :nc * L_CMP, :, :].reshape(lead + (nc, L_CMP, A_KV, A_DH)) + pe[:, None, :]
    blk = jnp.swapaxes(blk, -3, -2).reshape(lead + (nc, A_KV, L_CMP * A_DH))
    return mmnd(jax.nn.gelu(mmnd(blk, w1) + b1), w2) + b2


def nsa_core(q, q_pos, kc, vc, ks, vs, kw, vw, kw_pos, gates):
    nq = q.shape[0]
    qg = q.astype(jnp.float32).reshape(nq, A_KV, A_HPG, A_DH) * A_DH ** -0.5
    nc = kc.shape[0]
    c_end = (jnp.arange(nc) + 1) * L_CMP - 1
    c_mask = (c_end[None, :] <= q_pos[:, None])[:, None, None, :]
    p_c = masked_softmax(jnp.einsum('qghd,cgd->qghc', qg, kc.astype(jnp.float32)), c_mask)
    o_c = jnp.einsum('qghc,cgd->qghd', p_c, vc.astype(jnp.float32))
    nsb = ks.shape[1] // L_SEL
    ratio = L_SEL // L_CMP
    imp = jnp.pad(p_c, ((0, 0), (0, 0), (0, 0), (0, nsb * ratio - nc)))
    imp = imp.reshape(nq, A_KV, A_HPG, nsb, ratio).sum(axis=(2, 4))
    blk = jnp.arange(nsb)
    cur = (q_pos // L_SEL)[:, None, None]
    forced = (blk == 0) | (blk == cur) | (blk == cur - 1)
    imp = jnp.where(blk <= cur, jnp.where(forced, FORCE_SCORE, imp), NEG)
    _, sel = lax.top_k(imp, min(N_SEL, nsb))
    n_pick = sel.shape[-1]
    tok = (sel[..., None] * L_SEL + jnp.arange(L_SEL)).reshape(nq, A_KV, n_pick * L_SEL)
    gidx = jnp.arange(A_KV)[None, :, None]
    k_sel = ks[gidx, tok].astype(jnp.float32)
    v_sel = vs[gidx, tok].astype(jnp.float32)
    s_mask = (tok <= q_pos[:, None, None])[:, :, None, :]
    p_s = masked_softmax(jnp.einsum('qghd,qgmd->qghm', qg, k_sel), s_mask)
    o_s = jnp.einsum('qghm,qgmd->qghd', p_s, v_sel)
    rel = q_pos[:, None] - kw_pos[None, :]
    w_mask = ((rel >= 0) & (rel < WINDOW) & (kw_pos[None, :] >= 0))[:, None, None, :]
    p_w = masked_softmax(jnp.einsum('qghd,kgd->qghk', qg, kw.astype(jnp.float32)), w_mask)
    o_w = jnp.einsum('qghk,kgd->qghd', p_w, vw.astype(jnp.float32))
    g = gates.astype(jnp.float32).reshape(nq, A_KV, A_HPG, 3, 1)
    o = o_c * g[..., 0, :] + o_s * g[..., 1, :] + o_w * g[..., 2, :]
    return o.reshape(nq, A_HEADS, A_DH).astype(q.dtype)


def nsa_prompt(q, kc, vc, ks, vs, kw, vw, gates):
    B, T = q.shape[:2]
    nsb = -(-T // L_SEL)
    pad_sel = ((0, 0), (0, nsb * L_SEL - T), (0, 0), (0, 0))
    ks_g = jnp.swapaxes(jnp.pad(ks, pad_sel), 1, 2)
    vs_g = jnp.swapaxes(jnp.pad(vs, pad_sel), 1, 2)
    pad_win = ((0, 0), (WINDOW, 0), (0, 0), (0, 0))
    kw_p, vw_p = jnp.pad(kw, pad_win), jnp.pad(vw, pad_win)
    n_qb = T // Q_BLOCK

    def per_seq(args):
        q_b, kc_b, vc_b, ks_b, vs_b, kw_b, vw_b, g_b = args

        def per_block(i):
            start = i * Q_BLOCK
            sl = lambda a, n: lax.dynamic_slice_in_dim(a, start, n, 0)
            q_pos = start + jnp.arange(Q_BLOCK)
            kw_pos = start - WINDOW + jnp.arange(Q_BLOCK + WINDOW)
            return nsa_core(sl(q_b, Q_BLOCK), q_pos, kc_b, vc_b, ks_b, vs_b,
                            sl(kw_b, Q_BLOCK + WINDOW), sl(vw_b, Q_BLOCK + WINDOW), kw_pos, sl(g_b, Q_BLOCK))

        return lax.map(per_block, jnp.arange(n_qb)).reshape(T, A_HEADS, A_DH)

    return lax.map(per_seq, (q, kc, vc, ks_g, vs_g, kw_p, vw_p, gates))


def nsa_sample(q, kcn, vcn, ksn, vsn, kwn, vwn, gates, pool_kc, pool_vc, pool_ks, pool_vs,
               buf_kw, buf_vw, page_table, cmp_k, cmp_v, g_kc):
    S = q.shape[1]
    past = page_table.shape[1] * PAGE_SIZE
    T = past + S
    nsb = -(-T // L_SEL)
    wb = buf_kw.shape[1]
    q_pos = past + jnp.arange(S)
    kw_pos = jnp.concatenate([past - wb + jnp.arange(wb), q_pos])

    def rows(pool, pt, new):
        old = pool[pt].reshape((past,) + pool.shape[2:])
        return jnp.concatenate([old, new.astype(pool.dtype)], axis=0)

    def per_seq(args):
        q_b, kc_b, vc_b, ks_b, vs_b, kw_b, vw_b, g_b, kwb, vwb, pt = args
        kc = rmsnorm(compress(rows(pool_kc, pt, kc_b), *cmp_k), g_kc)
        vc = compress(rows(pool_vc, pt, vc_b), *cmp_v)
        pad = ((0, nsb * L_SEL - T), (0, 0), (0, 0))
        ks = jnp.swapaxes(jnp.pad(rows(pool_ks, pt, ks_b), pad), 0, 1)
        vs = jnp.swapaxes(jnp.pad(rows(pool_vs, pt, vs_b), pad), 0, 1)
        kw = jnp.concatenate([kwb, kw_b.astype(kwb.dtype)], axis=0)
        vw = jnp.concatenate([vwb, vw_b.astype(vwb.dtype)], axis=0)
        return nsa_core(q_b, q_pos, kc, vc, ks, vs, kw, vw, kw_pos, g_b)

    return lax.map(per_seq, (q, kcn, vcn, ksn, vsn, kwn, vwn, gates, buf_kw, buf_vw, page_table))


def merge_groups(ret_o, rg, nsa_o, r_gn, w_out):
    B, T = rg.shape[:2]
    r = rmsnorm(ret_o, r_gn).reshape(B, T, R_W) * jax.nn.silu(rg.astype(jnp.float32))
    mix = jnp.concatenate([r.astype(rg.dtype), nsa_o.reshape(B, T, A_W).astype(rg.dtype)], axis=-1)
    return mmnd(mix, w_out)


def memory_kv(mem, g_memtok, w_mk, w_mv, m_kn):
    B, M, _ = mem.shape
    m = rmsnorm(mem, g_memtok)
    k = rmsnorm(mmnd(m, w_mk).reshape(B, M, M_HEADS, M_DH), m_kn)
    v = mmnd(m, w_mv).reshape(B, M, M_HEADS, M_DH)
    return k, v


def memory_attend(h, mk, mv, w_mq, m_qn, w_mo):
    B, T, _ = h.shape
    q = rmsnorm(mmnd(h, w_mq).reshape(B, T, M_HEADS, M_DH), m_qn)
    s = jnp.einsum('bthd,bmhd->bhtm', q.astype(jnp.float32), mk.astype(jnp.float32)) * M_DH ** -0.5
    p = jax.nn.softmax(s, axis=-1)
    o = jnp.einsum('bhtm,bmhd->bthd', p, mv.astype(jnp.float32)).astype(h.dtype)
    return mmnd(o.reshape(B, T, M_W), w_mo)


def peer(h, w_pq, sub_k1, sub_k2, exp_u, exp_v):
    lead = h.shape[:-1]
    xt = h.reshape(-1, D_MODEL)
    n = xt.shape[0]
    nb = -(-n // P_BLOCK)
    q_all = mm(xt, w_pq)
    xb = jnp.pad(xt, ((0, nb * P_BLOCK - n), (0, 0))).reshape(nb, P_BLOCK, D_MODEL)
    qb = jnp.pad(q_all, ((0, nb * P_BLOCK - n), (0, 0))).reshape(nb, P_BLOCK, P_HEADS * P_DQ)
    k1, k2 = sub_k1.astype(jnp.float32), sub_k2.astype(jnp.float32)

    def block(args):
        x, qf = args
        q = qf.astype(jnp.float32).reshape(P_BLOCK, P_HEADS, 2, P_DQ // 2)
        s1 = jnp.einsum('phd,hkd->phk', q[:, :, 0], k1)
        s2 = jnp.einsum('phd,hkd->phk', q[:, :, 1], k2)
        v1, i1 = lax.top_k(s1, P_TOPK)
        v2, i2 = lax.top_k(s2, P_TOPK)
        cand = (v1[..., :, None] + v2[..., None, :]).reshape(P_BLOCK, P_HEADS, P_TOPK * P_TOPK)
        cidx = (i1[..., :, None] * P_NKEYS + i2[..., None, :]).reshape(P_BLOCK, P_HEADS, P_TOPK * P_TOPK)
        best, slot = lax.top_k(cand, P_TOPK)
        eidx = jnp.take_along_axis(cidx, slot, axis=-1)
        gate = jax.nn.softmax(best, axis=-1)
        act = jax.nn.gelu(jnp.einsum('phkd,pd->phk', exp_u[eidx], x).astype(jnp.float32))
        return jnp.einsum('phk,phkd->pd', (gate * act).astype(x.dtype), exp_v[eidx])

    y = lax.map(block, (xb, qb)).reshape(nb * P_BLOCK, D_MODEL)[:n]
    return y.reshape(lead + (D_MODEL,))


def kernel(x_prompt, x_sample, mem_prompt, cache_k_cmp, cache_v_cmp, cache_k_sel, cache_v_sel, state_k_win, state_v_win, state_ret, cache_mem_k, cache_mem_v, page_table, g_mix, w_in, r_gn, a_qn, a_kcn, a_ksn, a_kwn, cmp_pe_k, cmp_w1_k, cmp_b1_k, cmp_w2_k, cmp_b2_k, cmp_pe_v, cmp_w1_v, cmp_b1_v, cmp_w2_v, cmp_b2_v, w_out, g_xattn, g_memtok, w_mq, w_mk, w_mv, m_qn, m_kn, w_mo, g_ffn, w_pq, sub_k1, sub_k2, exp_u, exp_v):
    cmp_k = (cmp_pe_k, cmp_w1_k, cmp_b1_k, cmp_w2_k, cmp_b2_k)
    cmp_v = (cmp_pe_v, cmp_w1_v, cmp_b1_v, cmp_w2_v, cmp_b2_v)
    xp, xs = x_prompt, x_sample

    T = xp.shape[1]
    rq, rk, rv, rg, aq, akc, avc, aks, avs, akw, avw, ag = split_mix(rmsnorm(xp, g_mix), w_in)
    pos = jnp.arange(T)
    ret_o, ret_p = retention_prompt(rotary(rq, pos) * R_DK ** -0.5, rotary(rk, pos), rv)
    aq, aks, akw = rmsnorm(aq, a_qn), rmsnorm(aks, a_ksn), rmsnorm(akw, a_kwn)
    kc = rmsnorm(compress(akc, *cmp_k), a_kcn)
    vc = compress(avc, *cmp_v)
    nsa_o = nsa_prompt(aq, kc, vc, aks, avs, akw, avw, ag)
    hp = xp + merge_groups(ret_o, rg, nsa_o, r_gn, w_out)
    mem_k_p, mem_v_p = memory_kv(mem_prompt, g_memtok, w_mk, w_mv, m_kn)
    hp = hp + memory_attend(rmsnorm(hp, g_xattn), mem_k_p, mem_v_p, w_mq, m_qn, w_mo)
    xp = hp + peer(rmsnorm(hp, g_ffn), w_pq, sub_k1, sub_k2, exp_u, exp_v)
    wl = min(WINDOW, T)
    k_cmp_p, v_cmp_p, k_sel_p, v_sel_p = akc, avc, aks, avs
    k_win_p, v_win_p = akw[:, T - wl:], avw[:, T - wl:]
    ret_p = ret_p.astype(xp.dtype)

    S = xs.shape[1]
    past = page_table.shape[1] * PAGE_SIZE
    rq, rk, rv, rg, aq, akc, avc, aks, avs, akw, avw, ag = split_mix(rmsnorm(xs, g_mix), w_in)
    pos = past + jnp.arange(S)
    ret_s, ret_o = retention_chunk(state_ret.astype(jnp.float32), rotary(rq, pos) * R_DK ** -0.5,
                                   rotary(rk, pos), rv)
    aq, aks, akw = rmsnorm(aq, a_qn), rmsnorm(aks, a_ksn), rmsnorm(akw, a_kwn)
    nsa_o = nsa_sample(aq, akc, avc, aks, avs, akw, avw, ag, cache_k_cmp, cache_v_cmp, cache_k_sel,
                       cache_v_sel, state_k_win, state_v_win, page_table, cmp_k, cmp_v, a_kcn)
    hs = xs + merge_groups(ret_o, rg, nsa_o, r_gn, w_out)
    hs = hs + memory_attend(rmsnorm(hs, g_xattn), cache_mem_k, cache_mem_v, w_mq, m_qn, w_mo)
    xs = hs + peer(rmsnorm(hs, g_ffn), w_pq, sub_k1, sub_k2, exp_u, exp_v)
    wb = state_k_win.shape[1]
    k_cmp_s, v_cmp_s, k_sel_s, v_sel_s = akc, avc, aks, avs
    k_win_s = jnp.concatenate([state_k_win, akw.astype(state_k_win.dtype)], axis=1)[:, -wb:]
    v_win_s = jnp.concatenate([state_v_win, avw.astype(state_v_win.dtype)], axis=1)[:, -wb:]
    ret_s = ret_s.astype(state_ret.dtype)
    return (xp, xs, k_cmp_p, v_cmp_p, k_sel_p, v_sel_p, k_win_p, v_win_p, ret_p, mem_k_p, mem_v_p,
            k_cmp_s, v_cmp_s, k_sel_s, v_sel_s, k_win_s, v_win_s, ret_s)
```

```python
import functools

import jax
import jax.numpy as jnp
import numpy as np
from jax import lax
from jax.experimental import pallas as pl
from jax.experimental.pallas import tpu as pltpu

D_MODEL = 4096
PAGE_SIZE = 128
R_HEADS = 8
R_DK = 256
R_DV = 256
R_CHUNK = 128
ROPE_BASE = 10000.0
A_HEADS = 16
A_KV = 4
A_DH = 128
A_HPG = A_HEADS // A_KV
L_CMP = 32
CMP_HID = 256
L_SEL = 64
N_SEL = 16
WINDOW = 512
Q_BLOCK = 128
FORCE_SCORE = 1e4
NEG = -1e30
M_HEADS = 4
M_DH = 128
M_W = M_HEADS * M_DH
P_HEADS = 8
P_NKEYS = 128
P_DQ = 256
P_TOPK = 16
P_BLOCK = 128
EPS = 1e-6
R_W = R_HEADS * R_DV
A_W = A_HEADS * A_DH
IN_SIZES = (R_HEADS * R_DK, R_HEADS * R_DK, R_W, R_W, A_W) + (A_KV * A_DH,) * 6 + (A_HEADS * 3,)

VMEM_LIMIT_BYTES = 56 * 1024 * 1024


def _mm_kernel(a_ref, b_ref, o_ref, acc_ref):
    k = pl.program_id(2)

    @pl.when(k == 0)
    def _():
        acc_ref[...] = jnp.zeros_like(acc_ref)

    acc_ref[...] += jnp.dot(a_ref[...].astype(jnp.bfloat16), b_ref[...].astype(jnp.bfloat16),
                            preferred_element_type=jnp.float32)

    @pl.when(k == pl.num_programs(2) - 1)
    def _():
        o_ref[...] = acc_ref[...]


def mm(a, b, tm=512, tn=512, tk=1024):
    m, kdim = a.shape
    _, n = b.shape
    tm = min(tm, m)
    tn = min(tn, n)
    tk = min(tk, kdim)
    assert m % tm == 0 and kdim % tk == 0
    grid = (m // tm, pl.cdiv(n, tn), kdim // tk)
    return pl.pallas_call(
        _mm_kernel,
        grid=grid,
        in_specs=[pl.BlockSpec((tm, tk), lambda i, j, k: (i, k)),
                  pl.BlockSpec((tk, tn), lambda i, j, k: (k, j))],
        out_specs=pl.BlockSpec((tm, tn), lambda i, j, k: (i, j)),
        out_shape=jax.ShapeDtypeStruct((m, n), jnp.float32),
        scratch_shapes=[pltpu.VMEM((tm, tn), jnp.float32)],
        compiler_params=pltpu.CompilerParams(
            dimension_semantics=("parallel", "parallel", "arbitrary"),
            vmem_limit_bytes=VMEM_LIMIT_BYTES),
        name="mm",
    )(a, b)


def mmnd(a, b):
    lead = a.shape[:-1]
    out = mm(a.reshape(-1, a.shape[-1]), b)
    return out.reshape(lead + (b.shape[-1],))


def _mm_nt_kernel(a_ref, b_ref, o_ref):
    o_ref[...] = lax.dot_general(a_ref[...].astype(jnp.bfloat16), b_ref[...].astype(jnp.bfloat16),
                                 (((1,), (1,)), ((), ())), preferred_element_type=jnp.float32)


def mm_nt(a, b, tm=512, tn=512):
    m, kdim = a.shape
    n, _ = b.shape
    tm, tn = min(tm, m), min(tn, n)
    assert m % tm == 0 and n % tn == 0
    return pl.pallas_call(
        _mm_nt_kernel,
        grid=(m // tm, n // tn),
        in_specs=[pl.BlockSpec((tm, kdim), lambda i, j: (i, 0)),
                  pl.BlockSpec((tn, kdim), lambda i, j: (j, 0))],
        out_specs=pl.BlockSpec((tm, tn), lambda i, j: (i, j)),
        out_shape=jax.ShapeDtypeStruct((m, n), jnp.float32),
        compiler_params=pltpu.CompilerParams(
            dimension_semantics=("parallel", "parallel"),
            vmem_limit_bytes=VMEM_LIMIT_BYTES),
        name="mm_nt",
    )(a, b)


PEER_TR = 256
PEER_TM = 512
PEER_TE = 512
PEER_SLABS = PEER_TE // P_NKEYS
TAKEN = -3.0e38


def _extract_top(s, n):
    rows = s.shape[0]
    row_id = lax.broadcasted_iota(jnp.int32, s.shape, 0).astype(jnp.float32)
    rank = jnp.full(s.shape, float(n), jnp.float32)
    vals = []
    for k in range(n):
        m = jnp.max(s, axis=0, keepdims=True)
        first = jnp.min(jnp.where(s == m, row_id, float(rows)), axis=0, keepdims=True)
        hit = row_id == first
        rank = jnp.where(hit, float(k), rank)
        s = jnp.where(hit, TAKEN, s)
        vals.append(m)
    return rank, jnp.concatenate(vals, axis=0)


def _peer_route_kernel(qt_ref, k1_ref, k2_ref, r2_ref, e2_ref, n_ref, e1n_ref):
    f32, bf16 = jnp.float32, jnp.bfloat16
    half = P_DQ // 2
    qt = qt_ref[...].astype(bf16)
    s1 = jnp.dot(k1_ref[0].astype(bf16), qt[:half], preferred_element_type=f32)
    s2 = jnp.dot(k2_ref[0].astype(bf16), qt[half:], preferred_element_type=f32)
    rank1, v1 = _extract_top(s1, P_TOPK)
    rank2, v2 = _extract_top(s2, P_TOPK)
    r1_id = lax.broadcasted_iota(jnp.int32, v1.shape, 0).astype(f32)
    cnt = jnp.zeros_like(v1)
    for _ in range(P_TOPK):
        nxt = jnp.full_like(v1, TAKEN)
        for j in range(P_TOPK):
            nxt = jnp.where(cnt == float(j), v2[j:j + 1], nxt)
        front = jnp.where(cnt < float(P_TOPK), v1 + nxt, TAKEN)
        m = jnp.max(front, axis=0, keepdims=True)
        first = jnp.min(jnp.where(front == m, r1_id, float(P_TOPK)), axis=0, keepdims=True)
        cnt = cnt + jnp.where(r1_id == first, 1.0, 0.0)
    ev1 = jnp.exp(v1 - v1[0:1])
    ev2 = jnp.exp(v2 - v2[0:1])
    inner = jnp.zeros_like(v1)
    for j in range(P_TOPK):
        inner = inner + jnp.where(cnt > float(j), ev2[j:j + 1], 0.0)
    z = jnp.sum(ev1 * inner, axis=0, keepdims=True)
    n_dense = jnp.zeros_like(s1)
    for r in range(P_TOPK):
        n_dense = n_dense + jnp.where(rank1 == float(r), cnt[r:r + 1], 0.0)
    r2_ref[0] = rank2
    e2_ref[0] = jnp.exp(s2 - v2[0:1])
    n_ref[0] = n_dense
    e1n_ref[0] = jnp.exp(s1 - v1[0:1]) * (1.0 / z)


def _peer_mix_kernel(x_ref, u_ref, vt_ref, r2_ref, e2_ref, n_ref, e1n_ref, yt_ref):
    f32, bf16 = jnp.float32, jnp.bfloat16
    j = pl.program_id(1)

    @pl.when(j == 0)
    def _():
        yt_ref[...] = jnp.zeros_like(yt_ref)

    a_t = lax.dot_general(u_ref[...], x_ref[...], (((1,), (1,)), ((), ())),
                          preferred_element_type=f32)
    act = jax.nn.gelu(a_t)
    parts = []
    for s in range(PEER_SLABS):
        first_key = j * PEER_SLABS + s
        g = jnp.zeros((P_NKEYS, a_t.shape[1]), f32)
        for h in range(P_HEADS):
            n_row = n_ref[h, pl.ds(first_key, 1), :]
            w_row = e1n_ref[h, pl.ds(first_key, 1), :]
            g = g + jnp.where(r2_ref[h] < n_row, e2_ref[h] * w_row, 0.0)
        parts.append((g * act[s * P_NKEYS:(s + 1) * P_NKEYS]).astype(bf16))
    w_t = jnp.concatenate(parts, axis=0)
    yt_ref[...] += jnp.dot(vt_ref[...], w_t, preferred_element_type=f32)


def peer_pallas(x, w_pq, sub_k1, sub_k2, exp_u_b, exp_vt_b):
    bf16 = jnp.bfloat16
    n, d = x.shape
    n_exp = exp_u_b.shape[0]
    assert n % PEER_TM == 0 and n % PEER_TR == 0 and n_exp % PEER_TE == 0
    xb = x.astype(bf16)
    qt = mm_nt(jnp.swapaxes(w_pq, 0, 1), xb)
    route_shape = jax.ShapeDtypeStruct((P_HEADS, P_NKEYS, n), jnp.float32)
    key_spec = pl.BlockSpec((1, P_NKEYS, P_DQ // 2), lambda i, h: (h, 0, 0))
    route_out = pl.BlockSpec((1, P_NKEYS, PEER_TR), lambda i, h: (h, 0, i))
    r2, e2, nd, e1n = pl.pallas_call(
        _peer_route_kernel,
        grid=(n // PEER_TR, P_HEADS),
        in_specs=[pl.BlockSpec((P_DQ, PEER_TR), lambda i, h: (h, i)), key_spec, key_spec],
        out_specs=[route_out] * 4,
        out_shape=[route_shape] * 4,
        compiler_params=pltpu.CompilerParams(
            dimension_semantics=("parallel", "parallel"), vmem_limit_bytes=VMEM_LIMIT_BYTES),
        name="peer_route",
    )(qt, sub_k1, sub_k2)
    route_in = pl.BlockSpec((P_HEADS, P_NKEYS, PEER_TM), lambda i, j: (0, 0, i),
                            pipeline_mode=pl.Buffered(1))
    yt = pl.pallas_call(
        _peer_mix_kernel,
        grid=(n // PEER_TM, n_exp // PEER_TE),
        in_specs=[pl.BlockSpec((PEER_TM, d), lambda i, j: (i, 0), pipeline_mode=pl.Buffered(1)),
                  pl.BlockSpec((PEER_TE, d), lambda i, j: (j, 0)),
                  pl.BlockSpec((d, PEER_TE), lambda i, j: (0, j)),
                  route_in, route_in, route_in, route_in],
        out_specs=pl.BlockSpec((d, PEER_TM), lambda i, j: (0, i)),
        out_shape=jax.ShapeDtypeStruct((d, n), jnp.float32),
        compiler_params=pltpu.CompilerParams(
            dimension_semantics=("parallel", "arbitrary"), vmem_limit_bytes=VMEM_LIMIT_BYTES),
        name="peer_mix",
    )(xb, exp_u_b, exp_vt_b, r2, e2, nd, e1n)
    return jnp.swapaxes(yt, 0, 1)


def rmsnorm(x, g):
    xf = x.astype(jnp.float32)
    y = xf * lax.rsqrt(jnp.mean(xf * xf, axis=-1, keepdims=True) + EPS)
    return (y * g.astype(jnp.float32)).astype(x.dtype)


def rotary(x, pos):
    half = x.shape[-1] // 2
    inv = ROPE_BASE ** (-jnp.arange(half, dtype=jnp.float32) / half)
    ang = pos.astype(jnp.float32)[:, None] * inv[None, :]
    cos, sin = jnp.cos(ang)[:, None, :], jnp.sin(ang)[:, None, :]
    xf = x.astype(jnp.float32)
    x1, x2 = xf[..., :half], xf[..., half:]
    return jnp.concatenate([x1 * cos - x2 * sin, x1 * sin + x2 * cos], axis=-1).astype(x.dtype)


def masked_softmax(s, mask):
    p = jax.nn.softmax(jnp.where(mask, s, NEG), axis=-1)
    return jnp.where(mask, p, 0.0)


def split_mix(h, w_in):
    B, T, _ = h.shape
    cuts = np.cumsum(IN_SIZES)[:-1].tolist()
    rq, rk, rv, rg, aq, akc, avc, aks, avs, akw, avw, ag = jnp.split(mmnd(h, w_in), cuts, axis=-1)
    r = lambda a, n, d: a.reshape(B, T, n, d)
    return (r(rq, R_HEADS, R_DK), r(rk, R_HEADS, R_DK), r(rv, R_HEADS, R_DV), rg,
            r(aq, A_HEADS, A_DH), r(akc, A_KV, A_DH), r(avc, A_KV, A_DH), r(aks, A_KV, A_DH),
            r(avs, A_KV, A_DH), r(akw, A_KV, A_DH), r(avw, A_KV, A_DH),
            jax.nn.sigmoid(ag.astype(jnp.float32)).reshape(B, T, A_HEADS, 3), ag)


def retention_chunk(S, q, k, v):
    q, k, v = q.astype(jnp.float32), k.astype(jnp.float32), v.astype(jnp.float32)
    C = q.shape[1]
    lg = jnp.log1p(-jnp.exp2(-5.0 - jnp.arange(R_HEADS, dtype=jnp.float32)))
    idx = jnp.arange(C, dtype=jnp.float32)
    rel = idx[:, None] - idx[None, :]
    dmask = jnp.where(rel[None] >= 0, jnp.exp(jnp.maximum(rel, 0.0)[None] * lg[:, None, None]), 0.0)
    inner = jnp.einsum('bihd,bjhd->bhij', q, k) * dmask[None]
    o = jnp.einsum('bhij,bjhe->bihe', inner, v)
    o = o + jnp.einsum('bihd,bhde->bihe', q, S) * jnp.exp((idx + 1.0)[:, None] * lg[None, :])[None, :, :, None]
    kdec = k * jnp.exp((C - 1.0 - idx)[:, None] * lg[None, :])[None, :, :, None]
    S_new = S * jnp.exp(C * lg)[None, :, None, None] + jnp.einsum('bjhd,bjhe->bhde', kdec, v)
    return S_new, o


def retention_prompt(q, k, v):
    B, T, H, _ = q.shape
    nch = T // R_CHUNK
    chunks = lambda a: jnp.moveaxis(a.reshape(B, nch, R_CHUNK, H, a.shape[-1]), 1, 0)
    S0 = jnp.zeros((B, H, R_DK, R_DV), jnp.float32)
    S, o = lax.scan(lambda s, xs: retention_chunk(s, *xs), S0, (chunks(q), chunks(k), chunks(v)))
    return jnp.moveaxis(o, 0, 1).reshape(B, T, H, R_DV), S


def compress(rows, pe, w1, b1, w2, b2):
    T = rows.shape[-3]
    nc = T // L_CMP
    lead = rows.shape[:-3]
    blk = rows[..., :nc * L_CMP, :, :].reshape(lead + (nc, L_CMP, A_KV, A_DH)) + pe[:, None, :]
    blk = jnp.swapaxes(blk, -3, -2).reshape(lead + (nc, A_KV, L_CMP * A_DH))
    return mmnd(jax.nn.gelu(mmnd(blk, w1) + b1), w2) + b2


def nsa_core(q, q_pos, kc, vc, ks, vs, kw, vw, kw_pos, gates):
    nq = q.shape[0]
    qg = q.astype(jnp.float32).reshape(nq, A_KV, A_HPG, A_DH) * A_DH ** -0.5
    nc = kc.shape[0]
    c_end = (jnp.arange(nc) + 1) * L_CMP - 1
    c_mask = (c_end[None, :] <= q_pos[:, None])[:, None, None, :]
    p_c = masked_softmax(jnp.einsum('qghd,cgd->qghc', qg, kc.astype(jnp.float32)), c_mask)
    o_c = jnp.einsum('qghc,cgd->qghd', p_c, vc.astype(jnp.float32))
    nsb = ks.shape[1] // L_SEL
    ratio = L_SEL // L_CMP
    imp = jnp.pad(p_c, ((0, 0), (0, 0), (0, 0), (0, nsb * ratio - nc)))
    imp = imp.reshape(nq, A_KV, A_HPG, nsb, ratio).sum(axis=(2, 4))
    blk = jnp.arange(nsb)
    cur = (q_pos // L_SEL)[:, None, None]
    forced = (blk == 0) | (blk == cur) | (blk == cur - 1)
    imp = jnp.where(blk <= cur, jnp.where(forced, FORCE_SCORE, imp), NEG)
    _, sel = lax.top_k(imp, min(N_SEL, nsb))
    n_pick = sel.shape[-1]
    tok = (sel[..., None] * L_SEL + jnp.arange(L_SEL)).reshape(nq, A_KV, n_pick * L_SEL)
    gidx = jnp.arange(A_KV)[None, :, None]
    k_sel = ks[gidx, tok].astype(jnp.float32)
    v_sel = vs[gidx, tok].astype(jnp.float32)
    s_mask = (tok <= q_pos[:, None, None])[:, :, None, :]
    p_s = masked_softmax(jnp.einsum('qghd,qgmd->qghm', qg, k_sel), s_mask)
    o_s = jnp.einsum('qghm,qgmd->qghd', p_s, v_sel)
    rel = q_pos[:, None] - kw_pos[None, :]
    w_mask = ((rel >= 0) & (rel < WINDOW) & (kw_pos[None, :] >= 0))[:, None, None, :]
    p_w = masked_softmax(jnp.einsum('qghd,kgd->qghk', qg, kw.astype(jnp.float32)), w_mask)
    o_w = jnp.einsum('qghk,kgd->qghd', p_w, vw.astype(jnp.float32))
    g = gates.astype(jnp.float32).reshape(nq, A_KV, A_HPG, 3, 1)
    o = o_c * g[..., 0, :] + o_s * g[..., 1, :] + o_w * g[..., 2, :]
    return o.reshape(nq, A_HEADS, A_DH).astype(q.dtype)


NSA_TK = 512
N_TOPK_NEG = -3.0e38


def _softmax_rows(s, mask):
    s = jnp.where(mask, s, NEG)
    m = jnp.max(s, axis=-1, keepdims=True)
    e = jnp.where(mask, jnp.exp(s - m), 0.0)
    l = jnp.sum(e, axis=-1, keepdims=True)
    return e * (1.0 / jnp.maximum(l, 1e-30))


def _top_n_mask_cols(score_t, n):
    rows = score_t.shape[0]
    row_id = lax.broadcasted_iota(jnp.int32, score_t.shape, 0).astype(jnp.float32)

    def one(_, c):
        sc, sel = c
        m = jnp.max(sc, axis=0, keepdims=True)
        first = jnp.min(jnp.where(sc == m, row_id, float(rows)), axis=0, keepdims=True)
        hit = row_id == first
        return jnp.where(hit, N_TOPK_NEG, sc), jnp.where(hit, 1.0, sel)

    _, sel = lax.fori_loop(0, n, one, (score_t, jnp.zeros_like(score_t)), unroll=True)
    return sel


def _nsa_prompt_kernel(q_ref, gate_ref, kc_ref, vc_ref, ks_ref, vs_ref, kw_ref, vw_ref, e_ref, o_ref,
                       *, n_win_keys):
    f32, bf16 = jnp.float32, jnp.bfloat16
    qb = pl.program_id(1)
    start = qb * Q_BLOCK
    nt_dims = (((1,), (1,)), ((), ()))
    q = q_ref[...]
    qg = jnp.concatenate([q[:, h * A_DH:(h + 1) * A_DH] for h in range(A_HPG)], axis=0)
    q_pos = start + lax.broadcasted_iota(jnp.int32, (Q_BLOCK, 1), 0)
    rep = lambda a: jnp.concatenate([a] * A_HPG, axis=0)

    kc, vc = kc_ref[0], vc_ref[0]
    nc = kc.shape[0]
    nsb = nc // 2
    s_c = lax.dot_general(qg, kc, nt_dims, preferred_element_type=f32)
    lane_c = lax.broadcasted_iota(jnp.int32, (1, nc), 1)
    c_orig = jnp.where(lane_c < nsb, 2 * lane_c, 2 * (lane_c - nsb) + 1)
    cmask = ((c_orig + 1) * L_CMP - 1) <= q_pos
    p_c = _softmax_rows(s_c, rep(cmask))
    o_c = jnp.dot(p_c.astype(bf16), vc, preferred_element_type=f32)

    ph = p_c[0:Q_BLOCK]
    for h in range(1, A_HPG):
        ph = ph + p_c[h * Q_BLOCK:(h + 1) * Q_BLOCK]
    imp = ph[:, :nsb] + ph[:, nsb:]
    blk = lax.broadcasted_iota(jnp.int32, (1, nsb), 1)
    cur = q_pos // L_SEL
    forced = (blk == 0) | (blk == cur) | (blk == cur - 1)
    imp = jnp.where(blk <= cur, jnp.where(forced, FORCE_SCORE, imp), NEG)
    sel = _top_n_mask_cols(imp.T, min(N_SEL, nsb)).T
    sel_b = sel.astype(bf16)

    n_tiles = (start + Q_BLOCK + NSA_TK - 1) // NSA_TK

    def sel_step(j, carry):
        m, l, acc = carry
        off = pl.multiple_of(j * NSA_TK, NSA_TK)
        k = ks_ref[0, pl.ds(off, NSA_TK), :]
        v = vs_ref[0, pl.ds(off, NSA_TK), :]
        s = lax.dot_general(qg, k, nt_dims, preferred_element_type=f32)
        picked = jnp.dot(sel_b, e_ref[j], preferred_element_type=f32)
        k_pos = off + lax.broadcasted_iota(jnp.int32, (1, NSA_TK), 1)
        bias = jnp.where((picked > 0.5) & (k_pos <= q_pos), 0.0, NEG)
        s = s + rep(bias)
        m_new = jnp.maximum(m, jnp.max(s, axis=-1, keepdims=True))
        alpha = jnp.exp(m - m_new)
        p = jnp.exp(s - m_new)
        l = alpha * l + jnp.sum(p, axis=-1, keepdims=True)
        acc = alpha * acc + jnp.dot(p.astype(bf16), v, preferred_element_type=f32)
        return m_new, l, acc

    rows = A_HPG * Q_BLOCK
    m_s, l_s, acc_s = lax.fori_loop(
        0, n_tiles, sel_step,
        (jnp.full((rows, 1), NEG, f32), jnp.zeros((rows, 1), f32), jnp.zeros((rows, A_DH), f32)))
    o_s = acc_s * (1.0 / l_s)

    base = pl.multiple_of(jnp.maximum(start + Q_BLOCK - n_win_keys, 0), Q_BLOCK)
    kw = kw_ref[0, pl.ds(base, n_win_keys), :]
    vw = vw_ref[0, pl.ds(base, n_win_keys), :]
    s_w = lax.dot_general(qg, kw, nt_dims, preferred_element_type=f32)
    rel = q_pos - (base + lax.broadcasted_iota(jnp.int32, (1, n_win_keys), 1))
    p_w = _softmax_rows(s_w, rep((rel >= 0) & (rel < WINDOW)))
    o_w = jnp.dot(p_w.astype(bf16), vw, preferred_element_type=f32)

    gl = gate_ref[0]
    g = 1.0 / (1.0 + jnp.exp(-gl))
    outs = []
    for h in range(A_HPG):
        r = slice(h * Q_BLOCK, (h + 1) * Q_BLOCK)
        outs.append(o_c[r] * g[:, 3 * h:3 * h + 1] + o_s[r] * g[:, 3 * h + 1:3 * h + 2]
                    + o_w[r] * g[:, 3 * h + 2:3 * h + 3])
    o_ref[...] = jnp.concatenate(outs, axis=-1)


def nsa_prompt_pallas(q, kc, vc, ks, vs, kw, vw, gate_logits):
    bf16 = jnp.bfloat16
    T = q.shape[0]
    nc = kc.shape[0]
    nsb = T // L_SEL
    assert T % NSA_TK == 0 and nc == 2 * nsb
    n_win_keys = min(WINDOW + Q_BLOCK, T)
    qs = (q.astype(jnp.float32) * A_DH ** -0.5).astype(bf16).reshape(T, A_W)
    grp = lambda a: jnp.swapaxes(a, 0, 1).astype(bf16)
    perm = lambda a: jnp.concatenate([a[0::2], a[1::2]], axis=0)
    gl = jnp.swapaxes(gate_logits.reshape(T, A_KV, A_HPG * 3), 0, 1)
    n_t = T // NSA_TK
    key_blk = (jnp.arange(T, dtype=jnp.int32) // L_SEL).reshape(n_t, 1, NSA_TK)
    expand = (jnp.arange(nsb, dtype=jnp.int32)[None, :, None] == key_blk).astype(bf16)
    kv_spec = lambda rows: pl.BlockSpec((1, rows, A_DH), lambda g, i: (g, 0, 0))
    return pl.pallas_call(
        functools.partial(_nsa_prompt_kernel, n_win_keys=n_win_keys),
        grid=(A_KV, T // Q_BLOCK),
        in_specs=[pl.BlockSpec((Q_BLOCK, A_HPG * A_DH), lambda g, i: (i, g)),
                  pl.BlockSpec((1, Q_BLOCK, A_HPG * 3), lambda g, i: (g, i, 0)),
                  kv_spec(nc), kv_spec(nc), kv_spec(T), kv_spec(T), kv_spec(T), kv_spec(T),
                  pl.BlockSpec((n_t, nsb, NSA_TK), lambda g, i: (0, 0, 0))],
        out_specs=pl.BlockSpec((Q_BLOCK, A_HPG * A_DH), lambda g, i: (i, g)),
        out_shape=jax.ShapeDtypeStruct((T, A_W), jnp.float32),
        compiler_params=pltpu.CompilerParams(
            dimension_semantics=("parallel", "arbitrary"),
            vmem_limit_bytes=VMEM_LIMIT_BYTES),
        name="nsa_prompt",
    )(qs, gl, grp(perm(kc)), grp(perm(vc)), grp(ks), grp(vs), grp(kw), grp(vw), expand)


def nsa_sample(q, kcn, vcn, ksn, vsn, kwn, vwn, gates, pool_kc, pool_vc, pool_ks, pool_vs,
               buf_kw, buf_vw, page_table, cmp_k, cmp_v, g_kc):
    S = q.shape[1]
    past = page_table.shape[1] * PAGE_SIZE
    T = past + S
    nsb = -(-T // L_SEL)
    wb = buf_kw.shape[1]
    q_pos = past + jnp.arange(S)
    kw_pos = jnp.concatenate([past - wb + jnp.arange(wb), q_pos])

    def rows(pool, pt, new):
        old = pool[pt].reshape((past,) + pool.shape[2:])
        return jnp.concatenate([old, new.astype(pool.dtype)], axis=0)

    def per_seq(args):
        q_b, kc_b, vc_b, ks_b, vs_b, kw_b, vw_b, g_b, kwb, vwb, pt = args
        kc = rmsnorm(compress(rows(pool_kc, pt, kc_b), *cmp_k), g_kc)
        vc = compress(rows(pool_vc, pt, vc_b), *cmp_v)
        pad = ((0, nsb * L_SEL - T), (0, 0), (0, 0))
        ks = jnp.swapaxes(jnp.pad(rows(pool_ks, pt, ks_b), pad), 0, 1)
        vs = jnp.swapaxes(jnp.pad(rows(pool_vs, pt, vs_b), pad), 0, 1)
        kw = jnp.concatenate([kwb, kw_b.astype(kwb.dtype)], axis=0)
        vw = jnp.concatenate([vwb, vw_b.astype(vwb.dtype)], axis=0)
        return nsa_core(q_b, q_pos, kc, vc, ks, vs, kw, vw, kw_pos, g_b)

    return lax.map(per_seq, (q, kcn, vcn, ksn, vsn, kwn, vwn, gates, buf_kw, buf_vw, page_table))


def merge_groups(ret_o, rg, nsa_o, r_gn, w_out):
    B, T = rg.shape[:2]
    r = rmsnorm(ret_o, r_gn).reshape(B, T, R_W) * jax.nn.silu(rg.astype(jnp.float32))
    mix = jnp.concatenate([r.astype(rg.dtype), nsa_o.reshape(B, T, A_W).astype(rg.dtype)], axis=-1)
    return mmnd(mix, w_out)


def memory_kv(mem, g_memtok, w_mk, w_mv, m_kn):
    B, M, _ = mem.shape
    m = rmsnorm(mem, g_memtok)
    k = rmsnorm(mmnd(m, w_mk).reshape(B, M, M_HEADS, M_DH), m_kn)
    v = mmnd(m, w_mv).reshape(B, M, M_HEADS, M_DH)
    return k, v


def memory_attend(h, mk, mv, w_mq, m_qn, w_mo):
    B, T, _ = h.shape
    q = rmsnorm(mmnd(h, w_mq).reshape(B, T, M_HEADS, M_DH), m_qn)
    s = jnp.einsum('bthd,bmhd->bhtm', q.astype(jnp.float32), mk.astype(jnp.float32)) * M_DH ** -0.5
    p = jax.nn.softmax(s, axis=-1)
    o = jnp.einsum('bhtm,bmhd->bthd', p, mv.astype(jnp.float32)).astype(h.dtype)
    return mmnd(o.reshape(B, T, M_W), w_mo)


def kernel(x_prompt, x_sample, mem_prompt, cache_k_cmp, cache_v_cmp, cache_k_sel, cache_v_sel, state_k_win, state_v_win, state_ret, cache_mem_k, cache_mem_v, page_table, g_mix, w_in, r_gn, a_qn, a_kcn, a_ksn, a_kwn, cmp_pe_k, cmp_w1_k, cmp_b1_k, cmp_w2_k, cmp_b2_k, cmp_pe_v, cmp_w1_v, cmp_b1_v, cmp_w2_v, cmp_b2_v, w_out, g_xattn, g_memtok, w_mq, w_mk, w_mv, m_qn, m_kn, w_mo, g_ffn, w_pq, sub_k1, sub_k2, exp_u, exp_v):
    cmp_k = (cmp_pe_k, cmp_w1_k, cmp_b1_k, cmp_w2_k, cmp_b2_k)
    cmp_v = (cmp_pe_v, cmp_w1_v, cmp_b1_v, cmp_w2_v, cmp_b2_v)
    xp, xs = x_prompt, x_sample

    T = xp.shape[1]
    rq, rk, rv, rg, aq, akc, avc, aks, avs, akw, avw, ag, ag_p = split_mix(rmsnorm(xp, g_mix), w_in)
    pos = jnp.arange(T)
    ret_o, ret_p = retention_prompt(rotary(rq, pos) * R_DK ** -0.5, rotary(rk, pos), rv)
    aq, aks, akw = rmsnorm(aq, a_qn), rmsnorm(aks, a_ksn), rmsnorm(akw, a_kwn)
    kc = rmsnorm(compress(akc, *cmp_k), a_kcn)
    vc = compress(avc, *cmp_v)
    nsa_o = nsa_prompt_pallas(aq[0], kc[0], vc[0], aks[0], avs[0], akw[0], avw[0], ag_p[0])[None]
    hp = xp + merge_groups(ret_o, rg, nsa_o, r_gn, w_out)
    mem_k_p, mem_v_p = memory_kv(mem_prompt, g_memtok, w_mk, w_mv, m_kn)
    hp = hp + memory_attend(rmsnorm(hp, g_xattn), mem_k_p, mem_v_p, w_mq, m_qn, w_mo)
    wl = min(WINDOW, T)
    k_cmp_p, v_cmp_p, k_sel_p, v_sel_p = akc, avc, aks, avs
    k_win_p, v_win_p = akw[:, T - wl:], avw[:, T - wl:]
    ret_p = ret_p.astype(xp.dtype)

    S = xs.shape[1]
    past = page_table.shape[1] * PAGE_SIZE
    rq, rk, rv, rg, aq, akc, avc, aks, avs, akw, avw, ag, ag_s = split_mix(rmsnorm(xs, g_mix), w_in)
    pos = past + jnp.arange(S)
    ret_s, ret_o = retention_chunk(state_ret.astype(jnp.float32), rotary(rq, pos) * R_DK ** -0.5,
                                   rotary(rk, pos), rv)
    aq, aks, akw = rmsnorm(aq, a_qn), rmsnorm(aks, a_ksn), rmsnorm(akw, a_kwn)
    nsa_o = nsa_sample(aq, akc, avc, aks, avs, akw, avw, ag, cache_k_cmp, cache_v_cmp, cache_k_sel,
                       cache_v_sel, state_k_win, state_v_win, page_table, cmp_k, cmp_v, a_kcn)
    hs = xs + merge_groups(ret_o, rg, nsa_o, r_gn, w_out)
    hs = hs + memory_attend(rmsnorm(hs, g_xattn), cache_mem_k, cache_mem_v, w_mq, m_qn, w_mo)
    n_p = hp.shape[0] * hp.shape[1]
    tokens = jnp.concatenate([rmsnorm(hp, g_ffn).reshape(-1, D_MODEL), rmsnorm(hs, g_ffn).reshape(-1, D_MODEL)], axis=0)
    y = peer_pallas(tokens, w_pq, sub_k1, sub_k2, exp_u.astype(jnp.bfloat16),
                    jnp.swapaxes(exp_v, 0, 1).astype(jnp.bfloat16))
    xp = hp + y[:n_p].reshape(hp.shape)
    xs = hs + y[n_p:].reshape(hs.shape)
    wb = state_k_win.shape[1]
    k_cmp_s, v_cmp_s, k_sel_s, v_sel_s = akc, avc, aks, avs
    k_win_s = jnp.concatenate([state_k_win, akw.astype(state_k_win.dtype)], axis=1)[:, -wb:]
    v_win_s = jnp.concatenate([state_v_win, avw.astype(state_v_win.dtype)], axis=1)[:, -wb:]
    ret_s = ret_s.astype(state_ret.dtype)
    return (xp, xs, k_cmp_p, v_cmp_p, k_sel_p, v_sel_p, k_win_p, v_win_p, ret_p, mem_k_p, mem_v_p,
            k_cmp_s, v_cmp_s, k_sel_s, v_sel_s, k_win_s, v_win_s, ret_s)
```

```python
import functools

import jax
import jax.numpy as jnp
import numpy as np
from jax import lax
from jax.experimental import pallas as pl
from jax.experimental.pallas import tpu as pltpu

D_MODEL = 4096
PAGE_SIZE = 128
R_HEADS = 8
R_DK = 256
R_DV = 256
R_CHUNK = 128
ROPE_BASE = 10000.0
A_HEADS = 16
A_KV = 4
A_DH = 128
A_HPG = A_HEADS // A_KV
L_CMP = 32
CMP_HID = 256
L_SEL = 64
N_SEL = 16
WINDOW = 512
Q_BLOCK = 128
FORCE_SCORE = 1e4
NEG = -1e30
M_HEADS = 4
M_DH = 128
M_W = M_HEADS * M_DH
P_HEADS = 8
P_NKEYS = 128
P_DQ = 256
P_TOPK = 16
P_BLOCK = 128
EPS = 1e-6
R_W = R_HEADS * R_DV
A_W = A_HEADS * A_DH
IN_SIZES = (R_HEADS * R_DK, R_HEADS * R_DK, R_W, R_W, A_W) + (A_KV * A_DH,) * 6 + (A_HEADS * 3,)

VMEM_LIMIT_BYTES = 56 * 1024 * 1024


def _mm_kernel(a_ref, b_ref, o_ref, acc_ref):
    k = pl.program_id(2)

    @pl.when(k == 0)
    def _():
        acc_ref[...] = jnp.zeros_like(acc_ref)

    acc_ref[...] += jnp.dot(a_ref[...].astype(jnp.bfloat16), b_ref[...].astype(jnp.bfloat16),
                            preferred_element_type=jnp.float32)

    @pl.when(k == pl.num_programs(2) - 1)
    def _():
        o_ref[...] = acc_ref[...]


def mm(a, b, tm=1024, tn=1024, tk=1024):
    m, kdim = a.shape
    _, n = b.shape
    tm = min(tm, m)
    tn = min(tn, n)
    tk = min(tk, kdim)
    assert m % tm == 0 and kdim % tk == 0
    grid = (m // tm, pl.cdiv(n, tn), kdim // tk)
    return pl.pallas_call(
        _mm_kernel,
        grid=grid,
        in_specs=[pl.BlockSpec((tm, tk), lambda i, j, k: (i, k)),
                  pl.BlockSpec((tk, tn), lambda i, j, k: (k, j))],
        out_specs=pl.BlockSpec((tm, tn), lambda i, j, k: (i, j)),
        out_shape=jax.ShapeDtypeStruct((m, n), jnp.float32),
        scratch_shapes=[pltpu.VMEM((tm, tn), jnp.float32)],
        compiler_params=pltpu.CompilerParams(
            dimension_semantics=("parallel", "parallel", "arbitrary"),
            vmem_limit_bytes=VMEM_LIMIT_BYTES),
        name="mm",
    )(a, b)


def mmnd(a, b):
    lead = a.shape[:-1]
    out = mm(a.reshape(-1, a.shape[-1]), b)
    return out.reshape(lead + (b.shape[-1],))


def _mm_nt_kernel(a_ref, b_ref, o_ref):
    o_ref[...] = lax.dot_general(a_ref[...].astype(jnp.bfloat16), b_ref[...].astype(jnp.bfloat16),
                                 (((1,), (1,)), ((), ())), preferred_element_type=jnp.float32)


def mm_nt(a, b, tm=512, tn=512):
    m, kdim = a.shape
    n, _ = b.shape
    tm, tn = min(tm, m), min(tn, n)
    assert m % tm == 0 and n % tn == 0
    return pl.pallas_call(
        _mm_nt_kernel,
        grid=(m // tm, n // tn),
        in_specs=[pl.BlockSpec((tm, kdim), lambda i, j: (i, 0)),
                  pl.BlockSpec((tn, kdim), lambda i, j: (j, 0))],
        out_specs=pl.BlockSpec((tm, tn), lambda i, j: (i, j)),
        out_shape=jax.ShapeDtypeStruct((m, n), jnp.float32),
        compiler_params=pltpu.CompilerParams(
            dimension_semantics=("parallel", "parallel"),
            vmem_limit_bytes=VMEM_LIMIT_BYTES),
        name="mm_nt",
    )(a, b)


PEER_TR = 256
PEER_TM = 512
PEER_TE = 512
PEER_SLABS = PEER_TE // P_NKEYS
TAKEN = -3.0e38


def _extract_top(s, n):
    rows = s.shape[0]
    row_id = lax.broadcasted_iota(jnp.int32, s.shape, 0).astype(jnp.float32)
    rank = jnp.full(s.shape, float(n), jnp.float32)
    vals = []
    for k in range(n):
        m = jnp.max(s, axis=0, keepdims=True)
        first = jnp.min(jnp.where(s == m, row_id, float(rows)), axis=0, keepdims=True)
        hit = row_id == first
        rank = jnp.where(hit, float(k), rank)
        s = jnp.where(hit, TAKEN, s)
        vals.append(m)
    return rank, jnp.concatenate(vals, axis=0)


def _peer_route_kernel(qt_ref, k1_ref, k2_ref, r2_ref, e2_ref, n_ref, e1n_ref):
    f32, bf16 = jnp.float32, jnp.bfloat16
    half = P_DQ // 2
    qt = qt_ref[...].astype(bf16)
    s1 = jnp.dot(k1_ref[0].astype(bf16), qt[:half], preferred_element_type=f32)
    s2 = jnp.dot(k2_ref[0].astype(bf16), qt[half:], preferred_element_type=f32)
    rank1, v1 = _extract_top(s1, P_TOPK)
    rank2, v2 = _extract_top(s2, P_TOPK)
    r1_id = lax.broadcasted_iota(jnp.int32, v1.shape, 0).astype(f32)
    cnt = jnp.zeros_like(v1)
    for _ in range(P_TOPK):
        nxt = jnp.full_like(v1, TAKEN)
        for j in range(P_TOPK):
            nxt = jnp.where(cnt == float(j), v2[j:j + 1], nxt)
        front = jnp.where(cnt < float(P_TOPK), v1 + nxt, TAKEN)
        m = jnp.max(front, axis=0, keepdims=True)
        first = jnp.min(jnp.where(front == m, r1_id, float(P_TOPK)), axis=0, keepdims=True)
        cnt = cnt + jnp.where(r1_id == first, 1.0, 0.0)
    ev1 = jnp.exp(v1 - v1[0:1])
    ev2 = jnp.exp(v2 - v2[0:1])
    inner = jnp.zeros_like(v1)
    for j in range(P_TOPK):
        inner = inner + jnp.where(cnt > float(j), ev2[j:j + 1], 0.0)
    z = jnp.sum(ev1 * inner, axis=0, keepdims=True)
    n_dense = jnp.zeros_like(s1)
    for r in range(P_TOPK):
        n_dense = n_dense + jnp.where(rank1 == float(r), cnt[r:r + 1], 0.0)
    r2_ref[0] = rank2
    e2_ref[0] = jnp.exp(s2 - v2[0:1])
    n_ref[0] = n_dense
    e1n_ref[0] = jnp.exp(s1 - v1[0:1]) * (1.0 / z)


def _peer_mix_kernel(x_ref, u_ref, vt_ref, r2_ref, e2_ref, n_ref, e1n_ref, yt_ref):
    f32, bf16 = jnp.float32, jnp.bfloat16
    j = pl.program_id(1)

    @pl.when(j == 0)
    def _():
        yt_ref[...] = jnp.zeros_like(yt_ref)

    a_t = lax.dot_general(u_ref[...], x_ref[...], (((1,), (1,)), ((), ())),
                          preferred_element_type=f32)
    act = jax.nn.gelu(a_t)
    parts = []
    for s in range(PEER_SLABS):
        first_key = j * PEER_SLABS + s
        g = jnp.zeros((P_NKEYS, a_t.shape[1]), f32)
        for h in range(P_HEADS):
            n_row = n_ref[h, pl.ds(first_key, 1), :]
            w_row = e1n_ref[h, pl.ds(first_key, 1), :]
            g = g + jnp.where(r2_ref[h] < n_row, e2_ref[h] * w_row, 0.0)
        parts.append((g * act[s * P_NKEYS:(s + 1) * P_NKEYS]).astype(bf16))
    w_t = jnp.concatenate(parts, axis=0)
    yt_ref[...] += jnp.dot(vt_ref[...], w_t, preferred_element_type=f32)


def peer_pallas(x, w_pq, sub_k1, sub_k2, exp_u_b, exp_vt_b):
    bf16 = jnp.bfloat16
    n, d = x.shape
    n_exp = exp_u_b.shape[0]
    assert n % PEER_TM == 0 and n % PEER_TR == 0 and n_exp % PEER_TE == 0
    xb = x.astype(bf16)
    qt = mm_nt(jnp.swapaxes(w_pq, 0, 1), xb)
    route_shape = jax.ShapeDtypeStruct((P_HEADS, P_NKEYS, n), jnp.float32)
    key_spec = pl.BlockSpec((1, P_NKEYS, P_DQ // 2), lambda i, h: (h, 0, 0))
    route_out = pl.BlockSpec((1, P_NKEYS, PEER_TR), lambda i, h: (h, 0, i))
    r2, e2, nd, e1n = pl.pallas_call(
        _peer_route_kernel,
        grid=(n // PEER_TR, P_HEADS),
        in_specs=[pl.BlockSpec((P_DQ, PEER_TR), lambda i, h: (h, i)), key_spec, key_spec],
        out_specs=[route_out] * 4,
        out_shape=[route_shape] * 4,
        compiler_params=pltpu.CompilerParams(
            dimension_semantics=("parallel", "parallel"), vmem_limit_bytes=VMEM_LIMIT_BYTES),
        name="peer_route",
    )(qt, sub_k1, sub_k2)
    route_in = pl.BlockSpec((P_HEADS, P_NKEYS, PEER_TM), lambda i, j: (0, 0, i),
                            pipeline_mode=pl.Buffered(1))
    yt = pl.pallas_call(
        _peer_mix_kernel,
        grid=(n // PEER_TM, n_exp // PEER_TE),
        in_specs=[pl.BlockSpec((PEER_TM, d), lambda i, j: (i, 0), pipeline_mode=pl.Buffered(1)),
                  pl.BlockSpec((PEER_TE, d), lambda i, j: (j, 0)),
                  pl.BlockSpec((d, PEER_TE), lambda i, j: (0, j)),
                  route_in, route_in, route_in, route_in],
        out_specs=pl.BlockSpec((d, PEER_TM), lambda i, j: (0, i)),
        out_shape=jax.ShapeDtypeStruct((d, n), jnp.float32),
        compiler_params=pltpu.CompilerParams(
            dimension_semantics=("parallel", "arbitrary"), vmem_limit_bytes=VMEM_LIMIT_BYTES),
        name="peer_mix",
    )(xb, exp_u_b, exp_vt_b, r2, e2, nd, e1n)
    return jnp.swapaxes(yt, 0, 1)


def rmsnorm(x, g):
    xf = x.astype(jnp.float32)
    y = xf * lax.rsqrt(jnp.mean(xf * xf, axis=-1, keepdims=True) + EPS)
    return (y * g.astype(jnp.float32)).astype(x.dtype)


def rotary(x, pos):
    half = x.shape[-1] // 2
    inv = ROPE_BASE ** (-jnp.arange(half, dtype=jnp.float32) / half)
    ang = pos.astype(jnp.float32)[:, None] * inv[None, :]
    cos, sin = jnp.cos(ang)[:, None, :], jnp.sin(ang)[:, None, :]
    xf = x.astype(jnp.float32)
    x1, x2 = xf[..., :half], xf[..., half:]
    return jnp.concatenate([x1 * cos - x2 * sin, x1 * sin + x2 * cos], axis=-1).astype(x.dtype)


def masked_softmax(s, mask):
    p = jax.nn.softmax(jnp.where(mask, s, NEG), axis=-1)
    return jnp.where(mask, p, 0.0)


def split_mix(h, w_in):
    B, T, _ = h.shape
    cuts = np.cumsum(IN_SIZES)[:-1].tolist()
    rq, rk, rv, rg, aq, akc, avc, aks, avs, akw, avw, ag = jnp.split(mmnd(h, w_in), cuts, axis=-1)
    r = lambda a, n, d: a.reshape(B, T, n, d)
    return (r(rq, R_HEADS, R_DK), r(rk, R_HEADS, R_DK), r(rv, R_HEADS, R_DV), rg,
            r(aq, A_HEADS, A_DH), r(akc, A_KV, A_DH), r(avc, A_KV, A_DH), r(aks, A_KV, A_DH),
            r(avs, A_KV, A_DH), r(akw, A_KV, A_DH), r(avw, A_KV, A_DH),
            jax.nn.sigmoid(ag.astype(jnp.float32)).reshape(B, T, A_HEADS, 3), ag)


def retention_chunk(S, q, k, v):
    q, k, v = q.astype(jnp.float32), k.astype(jnp.float32), v.astype(jnp.float32)
    C = q.shape[1]
    lg = jnp.log1p(-jnp.exp2(-5.0 - jnp.arange(R_HEADS, dtype=jnp.float32)))
    idx = jnp.arange(C, dtype=jnp.float32)
    rel = idx[:, None] - idx[None, :]
    dmask = jnp.where(rel[None] >= 0, jnp.exp(jnp.maximum(rel, 0.0)[None] * lg[:, None, None]), 0.0)
    inner = jnp.einsum('bihd,bjhd->bhij', q, k) * dmask[None]
    o = jnp.einsum('bhij,bjhe->bihe', inner, v)
    o = o + jnp.einsum('bihd,bhde->bihe', q, S) * jnp.exp((idx + 1.0)[:, None] * lg[None, :])[None, :, :, None]
    kdec = k * jnp.exp((C - 1.0 - idx)[:, None] * lg[None, :])[None, :, :, None]
    S_new = S * jnp.exp(C * lg)[None, :, None, None] + jnp.einsum('bjhd,bjhe->bhde', kdec, v)
    return S_new, o


def retention_prompt(q, k, v):
    B, T, H, _ = q.shape
    nch = T // R_CHUNK
    chunks = lambda a: jnp.moveaxis(a.reshape(B, nch, R_CHUNK, H, a.shape[-1]), 1, 0)
    S0 = jnp.zeros((B, H, R_DK, R_DV), jnp.float32)
    S, o = lax.scan(lambda s, xs: retention_chunk(s, *xs), S0, (chunks(q), chunks(k), chunks(v)))
    return jnp.moveaxis(o, 0, 1).reshape(B, T, H, R_DV), S


def compress(rows, pe, w1, b1, w2, b2):
    T = rows.shape[-3]
    nc = T // L_CMP
    lead = rows.shape[:-3]
    blk = rows[..., :nc * L_CMP, :, :].reshape(lead + (nc, L_CMP, A_KV, A_DH)) + pe[:, None, :]
    blk = jnp.swapaxes(blk, -3, -2).reshape(lead + (nc, A_KV, L_CMP * A_DH))
    return mmnd(jax.nn.gelu(mmnd(blk, w1) + b1), w2) + b2


NSA_TK = 512
N_TOPK_NEG = -3.0e38


def _softmax_rows(s, mask):
    s = jnp.where(mask, s, NEG)
    m = jnp.max(s, axis=-1, keepdims=True)
    e = jnp.where(mask, jnp.exp(s - m), 0.0)
    l = jnp.sum(e, axis=-1, keepdims=True)
    return e * (1.0 / jnp.maximum(l, 1e-30))


def _top_n_mask_cols(score_t, n):
    rows = score_t.shape[0]
    row_id = lax.broadcasted_iota(jnp.int32, score_t.shape, 0).astype(jnp.float32)

    def one(_, c):
        sc, sel = c
        m = jnp.max(sc, axis=0, keepdims=True)
        first = jnp.min(jnp.where(sc == m, row_id, float(rows)), axis=0, keepdims=True)
        hit = row_id == first
        return jnp.where(hit, N_TOPK_NEG, sc), jnp.where(hit, 1.0, sel)

    _, sel = lax.fori_loop(0, n, one, (score_t, jnp.zeros_like(score_t)), unroll=True)
    return sel


def _nsa_prompt_kernel(q_ref, gate_ref, kc_ref, vc_ref, ks_ref, vs_ref, kw_ref, vw_ref, e_ref, o_ref,
                       *, n_win_keys):
    f32, bf16 = jnp.float32, jnp.bfloat16
    qb = pl.program_id(1)
    start = qb * Q_BLOCK
    nt_dims = (((1,), (1,)), ((), ()))
    q = q_ref[...]
    qg = jnp.concatenate([q[:, h * A_DH:(h + 1) * A_DH] for h in range(A_HPG)], axis=0)
    q_pos = start + lax.broadcasted_iota(jnp.int32, (Q_BLOCK, 1), 0)
    rep = lambda a: jnp.concatenate([a] * A_HPG, axis=0)

    kc, vc = kc_ref[0], vc_ref[0]
    nc = kc.shape[0]
    nsb = nc // 2
    s_c = lax.dot_general(qg, kc, nt_dims, preferred_element_type=f32)
    lane_c = lax.broadcasted_iota(jnp.int32, (1, nc), 1)
    c_orig = jnp.where(lane_c < nsb, 2 * lane_c, 2 * (lane_c - nsb) + 1)
    cmask = ((c_orig + 1) * L_CMP - 1) <= q_pos
    p_c = _softmax_rows(s_c, rep(cmask))
    o_c = jnp.dot(p_c.astype(bf16), vc, preferred_element_type=f32)

    ph = p_c[0:Q_BLOCK]
    for h in range(1, A_HPG):
        ph = ph + p_c[h * Q_BLOCK:(h + 1) * Q_BLOCK]
    imp = ph[:, :nsb] + ph[:, nsb:]
    blk = lax.broadcasted_iota(jnp.int32, (1, nsb), 1)
    cur = q_pos // L_SEL
    forced = (blk == 0) | (blk == cur) | (blk == cur - 1)
    imp = jnp.where(blk <= cur, jnp.where(forced, FORCE_SCORE, imp), NEG)
    sel = _top_n_mask_cols(imp.T, min(N_SEL, nsb)).T
    sel_b = sel.astype(bf16)

    n_tiles = (start + Q_BLOCK + NSA_TK - 1) // NSA_TK

    def sel_step(j, carry):
        m, l, acc = carry
        off = pl.multiple_of(j * NSA_TK, NSA_TK)
        k = ks_ref[0, pl.ds(off, NSA_TK), :]
        v = vs_ref[0, pl.ds(off, NSA_TK), :]
        s = lax.dot_general(qg, k, nt_dims, preferred_element_type=f32)
        picked = jnp.dot(sel_b, e_ref[j], preferred_element_type=f32)
        k_pos = off + lax.broadcasted_iota(jnp.int32, (1, NSA_TK), 1)
        bias = jnp.where((picked > 0.5) & (k_pos <= q_pos), 0.0, NEG)
        s = s + rep(bias)
        m_new = jnp.maximum(m, jnp.max(s, axis=-1, keepdims=True))
        alpha = jnp.exp(m - m_new)
        p = jnp.exp(s - m_new)
        l = alpha * l + jnp.sum(p, axis=-1, keepdims=True)
        acc = alpha * acc + jnp.dot(p.astype(bf16), v, preferred_element_type=f32)
        return m_new, l, acc

    rows = A_HPG * Q_BLOCK
    m_s, l_s, acc_s = lax.fori_loop(
        0, n_tiles, sel_step,
        (jnp.full((rows, 1), NEG, f32), jnp.zeros((rows, 1), f32), jnp.zeros((rows, A_DH), f32)))
    o_s = acc_s * (1.0 / l_s)

    base = pl.multiple_of(jnp.maximum(start + Q_BLOCK - n_win_keys, 0), Q_BLOCK)
    kw = kw_ref[0, pl.ds(base, n_win_keys), :]
    vw = vw_ref[0, pl.ds(base, n_win_keys), :]
    s_w = lax.dot_general(qg, kw, nt_dims, preferred_element_type=f32)
    rel = q_pos - (base + lax.broadcasted_iota(jnp.int32, (1, n_win_keys), 1))
    p_w = _softmax_rows(s_w, rep((rel >= 0) & (rel < WINDOW)))
    o_w = jnp.dot(p_w.astype(bf16), vw, preferred_element_type=f32)

    gl = gate_ref[0]
    g = 1.0 / (1.0 + jnp.exp(-gl))
    outs = []
    for h in range(A_HPG):
        r = slice(h * Q_BLOCK, (h + 1) * Q_BLOCK)
        outs.append(o_c[r] * g[:, 3 * h:3 * h + 1] + o_s[r] * g[:, 3 * h + 1:3 * h + 2]
                    + o_w[r] * g[:, 3 * h + 2:3 * h + 3])
    o_ref[...] = jnp.concatenate(outs, axis=-1)


def nsa_prompt_pallas(q, kc, vc, ks, vs, kw, vw, gate_logits):
    bf16 = jnp.bfloat16
    T = q.shape[0]
    nc = kc.shape[0]
    nsb = T // L_SEL
    assert T % NSA_TK == 0 and nc == 2 * nsb
    n_win_keys = min(WINDOW + Q_BLOCK, T)
    qs = (q.astype(jnp.float32) * A_DH ** -0.5).astype(bf16).reshape(T, A_W)
    grp = lambda a: jnp.swapaxes(a, 0, 1).astype(bf16)
    perm = lambda a: jnp.concatenate([a[0::2], a[1::2]], axis=0)
    gl = jnp.swapaxes(gate_logits.reshape(T, A_KV, A_HPG * 3), 0, 1)
    n_t = T // NSA_TK
    key_blk = (jnp.arange(T, dtype=jnp.int32) // L_SEL).reshape(n_t, 1, NSA_TK)
    expand = (jnp.arange(nsb, dtype=jnp.int32)[None, :, None] == key_blk).astype(bf16)
    kv_spec = lambda rows: pl.BlockSpec((1, rows, A_DH), lambda g, i: (g, 0, 0))
    return pl.pallas_call(
        functools.partial(_nsa_prompt_kernel, n_win_keys=n_win_keys),
        grid=(A_KV, T // Q_BLOCK),
        in_specs=[pl.BlockSpec((Q_BLOCK, A_HPG * A_DH), lambda g, i: (i, g)),
                  pl.BlockSpec((1, Q_BLOCK, A_HPG * 3), lambda g, i: (g, i, 0)),
                  kv_spec(nc), kv_spec(nc), kv_spec(T), kv_spec(T), kv_spec(T), kv_spec(T),
                  pl.BlockSpec((n_t, nsb, NSA_TK), lambda g, i: (0, 0, 0))],
        out_specs=pl.BlockSpec((Q_BLOCK, A_HPG * A_DH), lambda g, i: (i, g)),
        out_shape=jax.ShapeDtypeStruct((T, A_W), jnp.float32),
        compiler_params=pltpu.CompilerParams(
            dimension_semantics=("parallel", "arbitrary"),
            vmem_limit_bytes=VMEM_LIMIT_BYTES),
        name="nsa_prompt",
    )(qs, gl, grp(perm(kc)), grp(perm(vc)), grp(ks), grp(vs), grp(kw), grp(vw), expand)


SQ_PAD = 8
SEL_TAIL = 128


def _top_n_mask_rows(score, n):
    cols = score.shape[1]
    col_id = lax.broadcasted_iota(jnp.int32, score.shape, 1).astype(jnp.float32)

    def one(_, c):
        sc, sel = c
        m = jnp.max(sc, axis=1, keepdims=True)
        first = jnp.min(jnp.where(sc == m, col_id, float(cols)), axis=1, keepdims=True)
        hit = col_id == first
        return jnp.where(hit, N_TOPK_NEG, sc), jnp.where(hit, 1.0, sel)

    _, sel = lax.fori_loop(0, n, one, (score, jnp.zeros_like(score)), unroll=True)
    return sel


def _compress_seq(src, w1_ref, b1_ref, w2_ref, b2_ref, n_blk):
    bf16 = jnp.bfloat16
    half = n_blk // 2
    acc = jnp.zeros((A_KV * n_blk, CMP_HID), jnp.float32)
    for l in range(L_CMP):
        parts = []
        for g in range(A_KV):
            parts.append(src[g, pl.ds(l, half, stride=2 * L_CMP), :])
            parts.append(src[g, pl.ds(L_CMP + l, half, stride=2 * L_CMP), :])
        xg = jnp.concatenate(parts, axis=0).astype(bf16)
        acc = acc + jnp.dot(xg, w1_ref[l * A_DH:(l + 1) * A_DH, :], preferred_element_type=jnp.float32)
    hid = jax.nn.gelu(acc + b1_ref[...])
    return jnp.dot(hid.astype(bf16), w2_ref[...], preferred_element_type=jnp.float32) + b2_ref[...]


def _nsa_sample_kernel(pt_ref, kcp_ref, vcp_ref, ksp_ref, vsp_ref, q_ref, gate_ref,
                       ksn_ref, vsn_ref, kwn_ref, vwn_ref, kwb_ref, vwb_ref, pek_ref, pev_ref,
                       w1k_ref, b1k_ref, w2k_ref, b2k_ref, w1v_ref, b1v_ref, w2v_ref, b2v_ref, gkc_ref,
                       e_ref, o_ref, kc_scr, vc_scr, ks_scr, vs_scr, *, past, n_pages, n_new):
    f32, bf16 = jnp.float32, jnp.bfloat16
    p = pl.program_id(1)
    off = pl.multiple_of(p * PAGE_SIZE, PAGE_SIZE)
    for g in range(A_KV):
        cols = slice(g * A_DH, (g + 1) * A_DH)
        kc_scr[g, pl.ds(off, PAGE_SIZE), :] = kcp_ref[0, :, cols] + pek_ref[...]
        vc_scr[g, pl.ds(off, PAGE_SIZE), :] = vcp_ref[0, :, cols] + pev_ref[...]
    ks_scr[pl.ds(off, PAGE_SIZE), :] = ksp_ref[0].astype(bf16)
    vs_scr[pl.ds(off, PAGE_SIZE), :] = vsp_ref[0].astype(bf16)

    @pl.when(p == n_pages - 1)
    def _():
        nt_dims = (((1,), (1,)), ((), ()))
        n_blk = past // L_CMP
        half = n_blk // 2
        rows = A_HPG * SQ_PAD
        pad_rows = jnp.zeros((SEL_TAIL - SQ_PAD, A_KV * A_DH), f32)
        ks_scr[pl.ds(past, SEL_TAIL), :] = jnp.concatenate([ksn_ref[0], pad_rows], axis=0).astype(bf16)
        vs_scr[pl.ds(past, SEL_TAIL), :] = jnp.concatenate([vsn_ref[0], pad_rows], axis=0).astype(bf16)
        kcmp = _compress_seq(kc_scr, w1k_ref, b1k_ref, w2k_ref, b2k_ref, n_blk)
        kcmp = kcmp * lax.rsqrt(jnp.mean(kcmp * kcmp, axis=-1, keepdims=True) + EPS) * gkc_ref[...]
        vcmp = _compress_seq(vc_scr, w1v_ref, b1v_ref, w2v_ref, b2v_ref, n_blk)
        wb = kwb_ref.shape[1]
        kw_all = jnp.concatenate([kwb_ref[0], kwn_ref[0], pad_rows], axis=0).astype(bf16)
        vw_all = jnp.concatenate([vwb_ref[0], vwn_ref[0], pad_rows], axis=0).astype(bf16)

        q_pos = past + (lax.broadcasted_iota(jnp.int32, (SQ_PAD, 1), 0))
        rep = lambda a: jnp.concatenate([a] * A_HPG, axis=0)
        lanes = P_NKEYS
        zero_blk = jnp.zeros((lanes - n_blk, A_DH), f32)
        lane_c = lax.broadcasted_iota(jnp.int32, (1, lanes), 1)
        cmask = jnp.broadcast_to(lane_c < n_blk, (rows, lanes))

        o_c, imps = [], []
        for g in range(A_KV):
            qg = q_ref[0, g]
            kc_g = jnp.concatenate([kcmp[g * n_blk:(g + 1) * n_blk], zero_blk], axis=0).astype(bf16)
            vc_g = jnp.concatenate([vcmp[g * n_blk:(g + 1) * n_blk], zero_blk], axis=0).astype(bf16)
            s_c = lax.dot_general(qg, kc_g, nt_dims, preferred_element_type=f32)
            p_c = _softmax_rows(s_c, cmask)
            o_c.append(jnp.dot(p_c.astype(bf16), vc_g, preferred_element_type=f32))
            ph = p_c[0:SQ_PAD]
            for h in range(1, A_HPG):
                ph = ph + p_c[h * SQ_PAD:(h + 1) * SQ_PAD]
            imps.append(ph + pltpu.roll(ph, lanes - half, axis=1))
        imp = jnp.concatenate(imps, axis=0)
        q_pos4 = jnp.concatenate([q_pos] * A_KV, axis=0)
        cur = q_pos4 // L_SEL
        forced = (lane_c == 0) | (lane_c == cur) | (lane_c == cur - 1)
        imp = jnp.where(lane_c <= cur, jnp.where(forced, FORCE_SCORE, imp), NEG)
        nsb = -(-(past + n_new) // L_SEL)
        sel = _top_n_mask_rows(imp, min(N_SEL, nsb))
        picked = jnp.dot(sel.astype(bf16), e_ref[...], preferred_element_type=f32)
        n_keys = past + SEL_TAIL
        k_pos = lax.broadcasted_iota(jnp.int32, (1, n_keys), 1)
        w_pos = past - wb + lax.broadcasted_iota(jnp.int32, (1, wb + SEL_TAIL), 1)
        rel = q_pos - w_pos
        w_mask = rep((rel >= 0) & (rel < WINDOW) & (w_pos >= 0))

        gl = gate_ref[0]
        outs = []
        for g in range(A_KV):
            qg = q_ref[0, g]
            cols = slice(g * A_DH, (g + 1) * A_DH)
            s_mask = rep((picked[g * SQ_PAD:(g + 1) * SQ_PAD] > 0.5) & (k_pos <= q_pos))
            s_s = lax.dot_general(qg, ks_scr[:, cols], nt_dims, preferred_element_type=f32)
            p_s = _softmax_rows(s_s, s_mask)
            o_s = jnp.dot(p_s.astype(bf16), vs_scr[:, cols], preferred_element_type=f32)
            s_w = lax.dot_general(qg, kw_all[:, cols], nt_dims, preferred_element_type=f32)
            p_w = _softmax_rows(s_w, w_mask)
            o_w = jnp.dot(p_w.astype(bf16), vw_all[:, cols], preferred_element_type=f32)
            gs = 1.0 / (1.0 + jnp.exp(-gl[g]))
            o = o_c[g] * gs[:, 0:1] + o_s * gs[:, 1:2] + o_w * gs[:, 2:3]
            outs.extend([o[h * SQ_PAD:(h + 1) * SQ_PAD] for h in range(A_HPG)])
        o_ref[0] = jnp.concatenate(outs, axis=1)


def nsa_sample_pallas(q, ksn, vsn, kwn, vwn, gate_logits, pool_kc, pool_vc, pool_ks, pool_vs,
                      buf_kw, buf_vw, page_table, cmp_k, cmp_v, g_kc):
    f32, bf16 = jnp.float32, jnp.bfloat16
    B, S = q.shape[:2]
    assert S <= SQ_PAD and S < L_CMP
    n_pages = page_table.shape[1]
    past = n_pages * PAGE_SIZE
    wb = buf_kw.shape[1]
    kvw = A_KV * A_DH
    pad_q = lambda a: jnp.pad(a, ((0, 0), (0, SQ_PAD - S)) + ((0, 0),) * (a.ndim - 2))
    qs = pad_q((q.astype(f32) * A_DH ** -0.5).astype(bf16)).reshape(B, SQ_PAD, A_KV, A_HPG, A_DH)
    qs = qs.transpose(0, 2, 3, 1, 4).reshape(B, A_KV, A_HPG * SQ_PAD, A_DH)
    gl = pad_q(gate_logits).reshape(B, SQ_PAD, A_KV, A_HPG, 3).transpose(0, 2, 3, 1, 4)
    gl = gl.reshape(B, A_KV, A_HPG * SQ_PAD, 3)
    new = lambda a: pad_q(a.reshape(B, S, kvw))
    pool = lambda a: a.reshape(a.shape[0], PAGE_SIZE, kvw)
    tile_pe = lambda pe: jnp.tile(pe, (PAGE_SIZE // L_CMP, 1))
    pe_k, w1_k, b1_k, w2_k, b2_k = cmp_k
    pe_v, w1_v, b1_v, w2_v, b2_v = cmp_v
    row = lambda a: a.reshape(1, -1).astype(f32)
    n_keys = past + SEL_TAIL
    expand = (jnp.arange(P_NKEYS, dtype=jnp.int32)[:, None]
              == (jnp.arange(n_keys, dtype=jnp.int32) // L_SEL)[None, :]).astype(bf16)

    page_spec = pl.BlockSpec((1, PAGE_SIZE, kvw), lambda b, p, pt: (pt[b, p], 0, 0))
    seq3 = lambda r, c: pl.BlockSpec((1, r, c), lambda b, p, pt: (b, 0, 0))
    seq4 = lambda a, r, c: pl.BlockSpec((1, a, r, c), lambda b, p, pt: (b, 0, 0, 0))
    full = lambda a: pl.BlockSpec(a.shape, lambda b, p, pt: (0,) * a.ndim)
    consts = [tile_pe(pe_k), tile_pe(pe_v), w1_k.astype(bf16), row(b1_k), w2_k.astype(bf16), row(b2_k),
              w1_v.astype(bf16), row(b1_v), w2_v.astype(bf16), row(b2_v), row(g_kc), expand]
    out = pl.pallas_call(
        functools.partial(_nsa_sample_kernel, past=past, n_pages=n_pages, n_new=S),
        grid_spec=pltpu.PrefetchScalarGridSpec(
            num_scalar_prefetch=1,
            grid=(B, n_pages),
            in_specs=[page_spec] * 4
                     + [seq4(A_KV, A_HPG * SQ_PAD, A_DH), seq4(A_KV, A_HPG * SQ_PAD, 3)]
                     + [seq3(SQ_PAD, kvw)] * 4 + [seq3(wb, kvw)] * 2
                     + [full(c) for c in consts],
            out_specs=pl.BlockSpec((1, SQ_PAD, A_W), lambda b, p, pt: (b, 0, 0)),
            scratch_shapes=[pltpu.VMEM((A_KV, past, A_DH), f32), pltpu.VMEM((A_KV, past, A_DH), f32),
                            pltpu.VMEM((n_keys, kvw), bf16), pltpu.VMEM((n_keys, kvw), bf16)]),
        out_shape=jax.ShapeDtypeStruct((B, SQ_PAD, A_W), f32),
        compiler_params=pltpu.CompilerParams(
            dimension_semantics=("arbitrary", "arbitrary"), vmem_limit_bytes=VMEM_LIMIT_BYTES),
        name="nsa_sample",
    )(page_table, pool(pool_kc), pool(pool_vc), pool(pool_ks), pool(pool_vs), qs, gl,
      new(ksn), new(vsn), new(kwn), new(vwn), buf_kw.reshape(B, wb, kvw), buf_vw.reshape(B, wb, kvw),
      *consts)
    return out[:, :S]


def merge_groups(ret_o, rg, nsa_o, r_gn, w_out):
    B, T = rg.shape[:2]
    r = rmsnorm(ret_o, r_gn).reshape(B, T, R_W) * jax.nn.silu(rg.astype(jnp.float32))
    mix = jnp.concatenate([r.astype(rg.dtype), nsa_o.reshape(B, T, A_W).astype(rg.dtype)], axis=-1)
    return mmnd(mix, w_out)


def memory_kv(mem, g_memtok, w_mk, w_mv, m_kn):
    B, M, _ = mem.shape
    m = rmsnorm(mem, g_memtok)
    k = rmsnorm(mmnd(m, w_mk).reshape(B, M, M_HEADS, M_DH), m_kn)
    v = mmnd(m, w_mv).reshape(B, M, M_HEADS, M_DH)
    return k, v


def memory_attend(h, mk, mv, w_mq, m_qn, w_mo):
    B, T, _ = h.shape
    q = rmsnorm(mmnd(h, w_mq).reshape(B, T, M_HEADS, M_DH), m_qn)
    s = jnp.einsum('bthd,bmhd->bhtm', q.astype(jnp.float32), mk.astype(jnp.float32)) * M_DH ** -0.5
    p = jax.nn.softmax(s, axis=-1)
    o = jnp.einsum('bhtm,bmhd->bthd', p, mv.astype(jnp.float32)).astype(h.dtype)
    return mmnd(o.reshape(B, T, M_W), w_mo)


def kernel(x_prompt, x_sample, mem_prompt, cache_k_cmp, cache_v_cmp, cache_k_sel, cache_v_sel, state_k_win, state_v_win, state_ret, cache_mem_k, cache_mem_v, page_table, g_mix, w_in, r_gn, a_qn, a_kcn, a_ksn, a_kwn, cmp_pe_k, cmp_w1_k, cmp_b1_k, cmp_w2_k, cmp_b2_k, cmp_pe_v, cmp_w1_v, cmp_b1_v, cmp_w2_v, cmp_b2_v, w_out, g_xattn, g_memtok, w_mq, w_mk, w_mv, m_qn, m_kn, w_mo, g_ffn, w_pq, sub_k1, sub_k2, exp_u, exp_v):
    cmp_k = (cmp_pe_k, cmp_w1_k, cmp_b1_k, cmp_w2_k, cmp_b2_k)
    cmp_v = (cmp_pe_v, cmp_w1_v, cmp_b1_v, cmp_w2_v, cmp_b2_v)
    xp, xs = x_prompt, x_sample

    T = xp.shape[1]
    w_in = w_in.astype(jnp.bfloat16)
    w_out = w_out.astype(jnp.bfloat16)
    rq, rk, rv, rg, aq, akc, avc, aks, avs, akw, avw, ag, ag_p = split_mix(
        rmsnorm(xp, g_mix).astype(jnp.bfloat16), w_in)
    pos = jnp.arange(T)
    ret_o, ret_p = retention_prompt(rotary(rq, pos) * R_DK ** -0.5, rotary(rk, pos), rv)
    aq, aks, akw = rmsnorm(aq, a_qn), rmsnorm(aks, a_ksn), rmsnorm(akw, a_kwn)
    kc = rmsnorm(compress(akc, *cmp_k), a_kcn)
    vc = compress(avc, *cmp_v)
    nsa_o = nsa_prompt_pallas(aq[0], kc[0], vc[0], aks[0], avs[0], akw[0], avw[0], ag_p[0])[None]
    hp = xp + merge_groups(ret_o, rg, nsa_o, r_gn, w_out)
    mem_k_p, mem_v_p = memory_kv(mem_prompt, g_memtok, w_mk, w_mv, m_kn)
    hp = hp + memory_attend(rmsnorm(hp, g_xattn), mem_k_p, mem_v_p, w_mq, m_qn, w_mo)
    wl = min(WINDOW, T)
    k_cmp_p, v_cmp_p, k_sel_p, v_sel_p = akc, avc, aks, avs
    k_win_p, v_win_p = akw[:, T - wl:], avw[:, T - wl:]
    ret_p = ret_p.astype(xp.dtype)

    S = xs.shape[1]
    past = page_table.shape[1] * PAGE_SIZE
    rq, rk, rv, rg, aq, akc, avc, aks, avs, akw, avw, ag, ag_s = split_mix(
        rmsnorm(xs, g_mix).astype(jnp.bfloat16), w_in)
    pos = past + jnp.arange(S)
    ret_s, ret_o = retention_chunk(state_ret.astype(jnp.float32), rotary(rq, pos) * R_DK ** -0.5,
                                   rotary(rk, pos), rv)
    aq, aks, akw = rmsnorm(aq, a_qn), rmsnorm(aks, a_ksn), rmsnorm(akw, a_kwn)
    nsa_o = nsa_sample_pallas(aq, aks, avs, akw, avw, ag_s, cache_k_cmp, cache_v_cmp, cache_k_sel,
                              cache_v_sel, state_k_win, state_v_win, page_table, cmp_k, cmp_v, a_kcn)
    hs = xs + merge_groups(ret_o, rg, nsa_o, r_gn, w_out)
    hs = hs + memory_attend(rmsnorm(hs, g_xattn), cache_mem_k, cache_mem_v, w_mq, m_qn, w_mo)
    n_p = hp.shape[0] * hp.shape[1]
    tokens = jnp.concatenate([rmsnorm(hp, g_ffn).reshape(-1, D_MODEL), rmsnorm(hs, g_ffn).reshape(-1, D_MODEL)], axis=0)
    y = peer_pallas(tokens, w_pq, sub_k1, sub_k2, exp_u.astype(jnp.bfloat16),
                    jnp.swapaxes(exp_v, 0, 1).astype(jnp.bfloat16))
    xp = hp + y[:n_p].reshape(hp.shape)
    xs = hs + y[n_p:].reshape(hs.shape)
    wb = state_k_win.shape[1]
    k_cmp_s, v_cmp_s, k_sel_s, v_sel_s = akc, avc, aks, avs
    k_win_s = jnp.concatenate([state_k_win, akw.astype(state_k_win.dtype)], axis=1)[:, -wb:]
    v_win_s = jnp.concatenate([state_v_win, avw.astype(state_v_win.dtype)], axis=1)[:, -wb:]
    ret_s = ret_s.astype(state_ret.dtype)
    return (xp, xs, k_cmp_p, v_cmp_p, k_sel_p, v_sel_p, k_win_p, v_win_p, ret_p, mem_k_p, mem_v_p,
            k_cmp_s, v_cmp_s, k_sel_s, v_sel_s, k_win_s, v_win_s, ret_s)
```

```python
import functools

import jax
import jax.numpy as jnp
import numpy as np
from jax import lax
from jax.experimental import pallas as pl
from jax.experimental.pallas import tpu as pltpu

D_MODEL = 4096
PAGE_SIZE = 128
R_HEADS = 8
R_DK = 256
R_DV = 256
R_CHUNK = 128
ROPE_BASE = 10000.0
A_HEADS = 16
A_KV = 4
A_DH = 128
A_HPG = A_HEADS // A_KV
L_CMP = 32
CMP_HID = 256
L_SEL = 64
N_SEL = 16
WINDOW = 512
Q_BLOCK = 128
FORCE_SCORE = 1e4
NEG = -1e30
M_HEADS = 4
M_DH = 128
M_W = M_HEADS * M_DH
P_HEADS = 8
P_NKEYS = 128
P_DQ = 256
P_TOPK = 16
P_BLOCK = 128
EPS = 1e-6
R_W = R_HEADS * R_DV
A_W = A_HEADS * A_DH
IN_SIZES = (R_HEADS * R_DK, R_HEADS * R_DK, R_W, R_W, A_W) + (A_KV * A_DH,) * 6 + (A_HEADS * 3,)

VMEM_LIMIT_BYTES = 56 * 1024 * 1024


def _mm_kernel(a_ref, b_ref, o_ref, acc_ref):
    k = pl.program_id(2)

    @pl.when(k == 0)
    def _():
        acc_ref[...] = jnp.zeros_like(acc_ref)

    acc_ref[...] += jnp.dot(a_ref[...].astype(jnp.bfloat16), b_ref[...].astype(jnp.bfloat16),
                            preferred_element_type=jnp.float32)

    @pl.when(k == pl.num_programs(2) - 1)
    def _():
        o_ref[...] = acc_ref[...]


def mm(a, b, tm=1024, tn=1024, tk=1024):
    m, kdim = a.shape
    _, n = b.shape
    tm = min(tm, m)
    tn = min(tn, n)
    tk = min(tk, kdim)
    assert m % tm == 0 and kdim % tk == 0
    grid = (m // tm, pl.cdiv(n, tn), kdim // tk)
    return pl.pallas_call(
        _mm_kernel,
        grid=grid,
        in_specs=[pl.BlockSpec((tm, tk), lambda i, j, k: (i, k)),
                  pl.BlockSpec((tk, tn), lambda i, j, k: (k, j))],
        out_specs=pl.BlockSpec((tm, tn), lambda i, j, k: (i, j)),
        out_shape=jax.ShapeDtypeStruct((m, n), jnp.float32),
        scratch_shapes=[pltpu.VMEM((tm, tn), jnp.float32)],
        compiler_params=pltpu.CompilerParams(
            dimension_semantics=("parallel", "parallel", "arbitrary"),
            vmem_limit_bytes=VMEM_LIMIT_BYTES),
        name="mm",
    )(a, b)


def mmnd(a, b):
    lead = a.shape[:-1]
    out = mm(a.reshape(-1, a.shape[-1]), b)
    return out.reshape(lead + (b.shape[-1],))


def _mm_nt_kernel(a_ref, b_ref, o_ref):
    o_ref[...] = lax.dot_general(a_ref[...].astype(jnp.bfloat16), b_ref[...].astype(jnp.bfloat16),
                                 (((1,), (1,)), ((), ())), preferred_element_type=jnp.float32)


def mm_nt(a, b, tm=512, tn=512):
    m, kdim = a.shape
    n, _ = b.shape
    tm, tn = min(tm, m), min(tn, n)
    assert m % tm == 0 and n % tn == 0
    return pl.pallas_call(
        _mm_nt_kernel,
        grid=(m // tm, n // tn),
        in_specs=[pl.BlockSpec((tm, kdim), lambda i, j: (i, 0)),
                  pl.BlockSpec((tn, kdim), lambda i, j: (j, 0))],
        out_specs=pl.BlockSpec((tm, tn), lambda i, j: (i, j)),
        out_shape=jax.ShapeDtypeStruct((m, n), jnp.float32),
        compiler_params=pltpu.CompilerParams(
            dimension_semantics=("parallel", "parallel"),
            vmem_limit_bytes=VMEM_LIMIT_BYTES),
        name="mm_nt",
    )(a, b)


PEER_TR = 256
PEER_TM = 512
PEER_TE = 512
PEER_SLABS = PEER_TE // P_NKEYS
TAKEN = -3.0e38


def _extract_top(s, n):
    rows = s.shape[0]
    row_id = lax.broadcasted_iota(jnp.int32, s.shape, 0).astype(jnp.float32)
    rank = jnp.full(s.shape, float(n), jnp.float32)
    vals = []
    for k in range(n):
        m = jnp.max(s, axis=0, keepdims=True)
        first = jnp.min(jnp.where(s == m, row_id, float(rows)), axis=0, keepdims=True)
        hit = row_id == first
        rank = jnp.where(hit, float(k), rank)
        s = jnp.where(hit, TAKEN, s)
        vals.append(m)
    return rank, jnp.concatenate(vals, axis=0)


def _peer_route_kernel(qt_ref, k1_ref, k2_ref, r2_ref, e2_ref, n_ref, e1n_ref):
    f32, bf16 = jnp.float32, jnp.bfloat16
    half = P_DQ // 2
    qt = qt_ref[...].astype(bf16)
    s1 = jnp.dot(k1_ref[0].astype(bf16), qt[:half], preferred_element_type=f32)
    s2 = jnp.dot(k2_ref[0].astype(bf16), qt[half:], preferred_element_type=f32)
    rank1, v1 = _extract_top(s1, P_TOPK)
    rank2, v2 = _extract_top(s2, P_TOPK)
    r1_id = lax.broadcasted_iota(jnp.int32, v1.shape, 0).astype(f32)
    cnt = jnp.zeros_like(v1)
    for _ in range(P_TOPK):
        nxt = jnp.full_like(v1, TAKEN)
        for j in range(P_TOPK):
            nxt = jnp.where(cnt == float(j), v2[j:j + 1], nxt)
        front = jnp.where(cnt < float(P_TOPK), v1 + nxt, TAKEN)
        m = jnp.max(front, axis=0, keepdims=True)
        first = jnp.min(jnp.where(front == m, r1_id, float(P_TOPK)), axis=0, keepdims=True)
        cnt = cnt + jnp.where(r1_id == first, 1.0, 0.0)
    ev1 = jnp.exp(v1 - v1[0:1])
    ev2 = jnp.exp(v2 - v2[0:1])
    inner = jnp.zeros_like(v1)
    for j in range(P_TOPK):
        inner = inner + jnp.where(cnt > float(j), ev2[j:j + 1], 0.0)
    z = jnp.sum(ev1 * inner, axis=0, keepdims=True)
    n_dense = jnp.zeros_like(s1)
    for r in range(P_TOPK):
        n_dense = n_dense + jnp.where(rank1 == float(r), cnt[r:r + 1], 0.0)
    r2_ref[0] = rank2
    e2_ref[0] = jnp.exp(s2 - v2[0:1])
    n_ref[0] = n_dense
    e1n_ref[0] = jnp.exp(s1 - v1[0:1]) * (1.0 / z)


def _peer_mix_kernel(x_ref, u_ref, vt_ref, r2_ref, e2_ref, n_ref, e1n_ref, yt_ref, a_scr):
    f32, bf16 = jnp.float32, jnp.bfloat16
    j = pl.program_id(1)

    @pl.when(j == 0)
    def _():
        yt_ref[...] = jnp.zeros_like(yt_ref)
        a_scr[1] = jnp.zeros(a_scr.shape[1:], f32)

    act = jax.nn.gelu(a_scr[(j + 1) % 2])
    a_scr[j % 2] = lax.dot_general(u_ref[...], x_ref[...], (((1,), (1,)), ((), ())),
                                   preferred_element_type=f32)

    tile = jnp.maximum(j - 1, 0)
    parts = []
    for s in range(PEER_SLABS):
        first_key = tile * PEER_SLABS + s
        g = jnp.zeros((P_NKEYS, act.shape[1]), f32)
        for h in range(P_HEADS):
            n_row = n_ref[h, pl.ds(first_key, 1), :]
            w_row = e1n_ref[h, pl.ds(first_key, 1), :]
            g = g + jnp.where(r2_ref[h] < n_row, e2_ref[h] * w_row, 0.0)
        parts.append((g * act[s * P_NKEYS:(s + 1) * P_NKEYS]).astype(bf16))
    w_t = jnp.concatenate(parts, axis=0)
    yt_ref[...] += jnp.dot(vt_ref[0], w_t, preferred_element_type=f32)


def peer_pallas(x, w_pq, sub_k1, sub_k2, exp_u_b, exp_vt_b):
    bf16 = jnp.bfloat16
    n, d = x.shape
    n_exp = exp_u_b.shape[0]
    assert n % PEER_TM == 0 and n % PEER_TR == 0 and n_exp % PEER_TE == 0
    xb = x.astype(bf16)
    qt = mm_nt(jnp.swapaxes(w_pq, 0, 1), xb)
    route_shape = jax.ShapeDtypeStruct((P_HEADS, P_NKEYS, n), jnp.float32)
    key_spec = pl.BlockSpec((1, P_NKEYS, P_DQ // 2), lambda i, h: (h, 0, 0))
    route_out = pl.BlockSpec((1, P_NKEYS, PEER_TR), lambda i, h: (h, 0, i))
    r2, e2, nd, e1n = pl.pallas_call(
        _peer_route_kernel,
        grid=(n // PEER_TR, P_HEADS),
        in_specs=[pl.BlockSpec((P_DQ, PEER_TR), lambda i, h: (h, i)), key_spec, key_spec],
        out_specs=[route_out] * 4,
        out_shape=[route_shape] * 4,
        compiler_params=pltpu.CompilerParams(
            dimension_semantics=("parallel", "parallel"), vmem_limit_bytes=VMEM_LIMIT_BYTES),
        name="peer_route",
    )(qt, sub_k1, sub_k2)
    route_in = pl.BlockSpec((P_HEADS, P_NKEYS, PEER_TM), lambda i, j: (0, 0, i),
                            pipeline_mode=pl.Buffered(1))
    n_tiles = n_exp // PEER_TE
    yt = pl.pallas_call(
        _peer_mix_kernel,
        grid=(n // PEER_TM, n_tiles + 1),
        in_specs=[pl.BlockSpec((PEER_TM, d), lambda i, j: (i, 0), pipeline_mode=pl.Buffered(1)),
                  pl.BlockSpec((PEER_TE, d), lambda i, j: (jnp.minimum(j, n_tiles - 1), 0)),
                  pl.BlockSpec((1, d, PEER_TE), lambda i, j: (jnp.maximum(j - 1, 0), 0, 0)),
                  route_in, route_in, route_in, route_in],
        out_specs=pl.BlockSpec((d, PEER_TM), lambda i, j: (0, i)),
        out_shape=jax.ShapeDtypeStruct((d, n), jnp.float32),
        scratch_shapes=[pltpu.VMEM((2, PEER_TE, PEER_TM), jnp.float32)],
        compiler_params=pltpu.CompilerParams(
            dimension_semantics=("parallel", "arbitrary"), vmem_limit_bytes=VMEM_LIMIT_BYTES),
        name="peer_mix",
    )(xb, exp_u_b, exp_vt_b, r2, e2, nd, e1n)
    return jnp.swapaxes(yt, 0, 1)


def rmsnorm(x, g):
    xf = x.astype(jnp.float32)
    y = xf * lax.rsqrt(jnp.mean(xf * xf, axis=-1, keepdims=True) + EPS)
    return (y * g.astype(jnp.float32)).astype(x.dtype)


def rotary(x, pos):
    half = x.shape[-1] // 2
    inv = ROPE_BASE ** (-jnp.arange(half, dtype=jnp.float32) / half)
    ang = pos.astype(jnp.float32)[:, None] * inv[None, :]
    cos, sin = jnp.cos(ang)[:, None, :], jnp.sin(ang)[:, None, :]
    xf = x.astype(jnp.float32)
    x1, x2 = xf[..., :half], xf[..., half:]
    return jnp.concatenate([x1 * cos - x2 * sin, x1 * sin + x2 * cos], axis=-1).astype(x.dtype)


def masked_softmax(s, mask):
    p = jax.nn.softmax(jnp.where(mask, s, NEG), axis=-1)
    return jnp.where(mask, p, 0.0)


def split_mix(h, w_in):
    B, T, _ = h.shape
    cuts = np.cumsum(IN_SIZES)[:-1].tolist()
    rq, rk, rv, rg, aq, akc, avc, aks, avs, akw, avw, ag = jnp.split(mmnd(h, w_in), cuts, axis=-1)
    r = lambda a, n, d: a.reshape(B, T, n, d)
    return (r(rq, R_HEADS, R_DK), r(rk, R_HEADS, R_DK), r(rv, R_HEADS, R_DV), rg,
            r(aq, A_HEADS, A_DH), r(akc, A_KV, A_DH), r(avc, A_KV, A_DH), r(aks, A_KV, A_DH),
            r(avs, A_KV, A_DH), r(akw, A_KV, A_DH), r(avw, A_KV, A_DH),
            jax.nn.sigmoid(ag.astype(jnp.float32)).reshape(B, T, A_HEADS, 3), ag)


def retention_chunk(S, q, k, v):
    q, k, v = q.astype(jnp.float32), k.astype(jnp.float32), v.astype(jnp.float32)
    C = q.shape[1]
    lg = jnp.log1p(-jnp.exp2(-5.0 - jnp.arange(R_HEADS, dtype=jnp.float32)))
    idx = jnp.arange(C, dtype=jnp.float32)
    rel = idx[:, None] - idx[None, :]
    dmask = jnp.where(rel[None] >= 0, jnp.exp(jnp.maximum(rel, 0.0)[None] * lg[:, None, None]), 0.0)
    inner = jnp.einsum('bihd,bjhd->bhij', q, k) * dmask[None]
    o = jnp.einsum('bhij,bjhe->bihe', inner, v)
    o = o + jnp.einsum('bihd,bhde->bihe', q, S) * jnp.exp((idx + 1.0)[:, None] * lg[None, :])[None, :, :, None]
    kdec = k * jnp.exp((C - 1.0 - idx)[:, None] * lg[None, :])[None, :, :, None]
    S_new = S * jnp.exp(C * lg)[None, :, None, None] + jnp.einsum('bjhd,bjhe->bhde', kdec, v)
    return S_new, o


def retention_prompt(q, k, v):
    B, T, H, _ = q.shape
    nch = T // R_CHUNK
    chunks = lambda a: jnp.moveaxis(a.reshape(B, nch, R_CHUNK, H, a.shape[-1]), 1, 0)
    S0 = jnp.zeros((B, H, R_DK, R_DV), jnp.float32)
    S, o = lax.scan(lambda s, xs: retention_chunk(s, *xs), S0, (chunks(q), chunks(k), chunks(v)))
    return jnp.moveaxis(o, 0, 1).reshape(B, T, H, R_DV), S


def compress(rows, pe, w1, b1, w2, b2):
    T = rows.shape[-3]
    nc = T // L_CMP
    lead = rows.shape[:-3]
    blk = rows[..., :nc * L_CMP, :, :].reshape(lead + (nc, L_CMP, A_KV, A_DH)) + pe[:, None, :]
    blk = jnp.swapaxes(blk, -3, -2).reshape(lead + (nc, A_KV, L_CMP * A_DH))
    return mmnd(jax.nn.gelu(mmnd(blk, w1) + b1), w2) + b2


NSA_TK = 512
N_TOPK_NEG = -3.0e38


def _softmax_rows(s, mask):
    s = jnp.where(mask, s, NEG)
    m = jnp.max(s, axis=-1, keepdims=True)
    e = jnp.where(mask, jnp.exp(s - m), 0.0)
    l = jnp.sum(e, axis=-1, keepdims=True)
    return e * (1.0 / jnp.maximum(l, 1e-30))


def _top_n_mask_cols(score_t, n):
    rows = score_t.shape[0]
    row_id = lax.broadcasted_iota(jnp.int32, score_t.shape, 0).astype(jnp.float32)

    def one(_, c):
        sc, sel = c
        m = jnp.max(sc, axis=0, keepdims=True)
        first = jnp.min(jnp.where(sc == m, row_id, float(rows)), axis=0, keepdims=True)
        hit = row_id == first
        return jnp.where(hit, N_TOPK_NEG, sc), jnp.where(hit, 1.0, sel)

    _, sel = lax.fori_loop(0, n, one, (score_t, jnp.zeros_like(score_t)), unroll=True)
    return sel


def _nsa_prompt_kernel(q_ref, gate_ref, kc_ref, vc_ref, ks_ref, vs_ref, kw_ref, vw_ref, e_ref, o_ref,
                       *, n_win_keys):
    f32, bf16 = jnp.float32, jnp.bfloat16
    qb = pl.program_id(1)
    start = qb * Q_BLOCK
    nt_dims = (((1,), (1,)), ((), ()))
    q = q_ref[...]
    qg = jnp.concatenate([q[:, h * A_DH:(h + 1) * A_DH] for h in range(A_HPG)], axis=0)
    q_pos = start + lax.broadcasted_iota(jnp.int32, (Q_BLOCK, 1), 0)
    rep = lambda a: jnp.concatenate([a] * A_HPG, axis=0)

    kc, vc = kc_ref[0], vc_ref[0]
    nc = kc.shape[0]
    nsb = nc // 2
    s_c = lax.dot_general(qg, kc, nt_dims, preferred_element_type=f32)
    lane_c = lax.broadcasted_iota(jnp.int32, (1, nc), 1)
    c_orig = jnp.where(lane_c < nsb, 2 * lane_c, 2 * (lane_c - nsb) + 1)
    cmask = ((c_orig + 1) * L_CMP - 1) <= q_pos
    p_c = _softmax_rows(s_c, rep(cmask))
    o_c = jnp.dot(p_c.astype(bf16), vc, preferred_element_type=f32)

    ph = p_c[0:Q_BLOCK]
    for h in range(1, A_HPG):
        ph = ph + p_c[h * Q_BLOCK:(h + 1) * Q_BLOCK]
    imp = ph[:, :nsb] + ph[:, nsb:]
    blk = lax.broadcasted_iota(jnp.int32, (1, nsb), 1)
    cur = q_pos // L_SEL
    forced = (blk == 0) | (blk == cur) | (blk == cur - 1)
    imp = jnp.where(blk <= cur, jnp.where(forced, FORCE_SCORE, imp), NEG)
    sel = _top_n_mask_cols(imp.T, min(N_SEL, nsb)).T
    sel_b = sel.astype(bf16)

    n_tiles = (start + Q_BLOCK + NSA_TK - 1) // NSA_TK

    def sel_step(j, carry):
        m, l, acc = carry
        off = pl.multiple_of(j * NSA_TK, NSA_TK)
        k = ks_ref[0, pl.ds(off, NSA_TK), :]
        v = vs_ref[0, pl.ds(off, NSA_TK), :]
        s = lax.dot_general(qg, k, nt_dims, preferred_element_type=f32)
        picked = jnp.dot(sel_b, e_ref[j], preferred_element_type=f32)
        k_pos = off + lax.broadcasted_iota(jnp.int32, (1, NSA_TK), 1)
        bias = jnp.where((picked > 0.5) & (k_pos <= q_pos), 0.0, NEG)
        s = s + rep(bias)
        m_new = jnp.maximum(m, jnp.max(s, axis=-1, keepdims=True))
        alpha = jnp.exp(m - m_new)
        p = jnp.exp(s - m_new)
        l = alpha * l + jnp.sum(p, axis=-1, keepdims=True)
        acc = alpha * acc + jnp.dot(p.astype(bf16), v, preferred_element_type=f32)
        return m_new, l, acc

    rows = A_HPG * Q_BLOCK
    m_s, l_s, acc_s = lax.fori_loop(
        0, n_tiles, sel_step,
        (jnp.full((rows, 1), NEG, f32), jnp.zeros((rows, 1), f32), jnp.zeros((rows, A_DH), f32)))
    o_s = acc_s * (1.0 / l_s)

    base = pl.multiple_of(jnp.maximum(start + Q_BLOCK - n_win_keys, 0), Q_BLOCK)
    kw = kw_ref[0, pl.ds(base, n_win_keys), :]
    vw = vw_ref[0, pl.ds(base, n_win_keys), :]
    s_w = lax.dot_general(qg, kw, nt_dims, preferred_element_type=f32)
    rel = q_pos - (base + lax.broadcasted_iota(jnp.int32, (1, n_win_keys), 1))
    p_w = _softmax_rows(s_w, rep((rel >= 0) & (rel < WINDOW)))
    o_w = jnp.dot(p_w.astype(bf16), vw, preferred_element_type=f32)

    gl = gate_ref[0]
    g = 1.0 / (1.0 + jnp.exp(-gl))
    outs = []
    for h in range(A_HPG):
        r = slice(h * Q_BLOCK, (h + 1) * Q_BLOCK)
        outs.append(o_c[r] * g[:, 3 * h:3 * h + 1] + o_s[r] * g[:, 3 * h + 1:3 * h + 2]
                    + o_w[r] * g[:, 3 * h + 2:3 * h + 3])
    o_ref[...] = jnp.concatenate(outs, axis=-1)


def nsa_prompt_pallas(q, kc, vc, ks, vs, kw, vw, gate_logits):
    bf16 = jnp.bfloat16
    T = q.shape[0]
    nc = kc.shape[0]
    nsb = T // L_SEL
    assert T % NSA_TK == 0 and nc == 2 * nsb
    n_win_keys = min(WINDOW + Q_BLOCK, T)
    qs = (q.astype(jnp.float32) * A_DH ** -0.5).astype(bf16).reshape(T, A_W)
    grp = lambda a: jnp.swapaxes(a, 0, 1).astype(bf16)
    perm = lambda a: jnp.concatenate([a[0::2], a[1::2]], axis=0)
    gl = jnp.swapaxes(gate_logits.reshape(T, A_KV, A_HPG * 3), 0, 1)
    n_t = T // NSA_TK
    key_blk = (jnp.arange(T, dtype=jnp.int32) // L_SEL).reshape(n_t, 1, NSA_TK)
    expand = (jnp.arange(nsb, dtype=jnp.int32)[None, :, None] == key_blk).astype(bf16)
    kv_spec = lambda rows: pl.BlockSpec((1, rows, A_DH), lambda g, i: (g, 0, 0))
    return pl.pallas_call(
        functools.partial(_nsa_prompt_kernel, n_win_keys=n_win_keys),
        grid=(A_KV, T // Q_BLOCK),
        in_specs=[pl.BlockSpec((Q_BLOCK, A_HPG * A_DH), lambda g, i: (i, g)),
                  pl.BlockSpec((1, Q_BLOCK, A_HPG * 3), lambda g, i: (g, i, 0)),
                  kv_spec(nc), kv_spec(nc), kv_spec(T), kv_spec(T), kv_spec(T), kv_spec(T),
                  pl.BlockSpec((n_t, nsb, NSA_TK), lambda g, i: (0, 0, 0))],
        out_specs=pl.BlockSpec((Q_BLOCK, A_HPG * A_DH), lambda g, i: (i, g)),
        out_shape=jax.ShapeDtypeStruct((T, A_W), jnp.float32),
        compiler_params=pltpu.CompilerParams(
            dimension_semantics=("parallel", "arbitrary"),
            vmem_limit_bytes=VMEM_LIMIT_BYTES),
        name="nsa_prompt",
    )(qs, gl, grp(perm(kc)), grp(perm(vc)), grp(ks), grp(vs), grp(kw), grp(vw), expand)


SQ_PAD = 8
SEL_TAIL = 128


def _top_n_mask_rows(score, n):
    cols = score.shape[1]
    col_id = lax.broadcasted_iota(jnp.int32, score.shape, 1).astype(jnp.float32)

    def one(_, c):
        sc, sel = c
        m = jnp.max(sc, axis=1, keepdims=True)
        first = jnp.min(jnp.where(sc == m, col_id, float(cols)), axis=1, keepdims=True)
        hit = col_id == first
        return jnp.where(hit, N_TOPK_NEG, sc), jnp.where(hit, 1.0, sel)

    _, sel = lax.fori_loop(0, n, one, (score, jnp.zeros_like(score)), unroll=True)
    return sel


def _compress_seq(src, pe_ref, w1_ref, b1_ref, w2_ref, b2_ref, n_blk):
    bf16 = jnp.bfloat16
    half = n_blk // 2
    acc = jnp.zeros((A_KV * n_blk, CMP_HID), jnp.float32)
    for l in range(L_CMP):
        parts = []
        for g in range(A_KV):
            parts.append(src[g, pl.ds(l, half, stride=2 * L_CMP), :])
            parts.append(src[g, pl.ds(L_CMP + l, half, stride=2 * L_CMP), :])
        xg = (jnp.concatenate(parts, axis=0) + pe_ref[l:l + 1, :]).astype(bf16)
        acc = acc + jnp.dot(xg, w1_ref[l * A_DH:(l + 1) * A_DH, :], preferred_element_type=jnp.float32)
    hid = jax.nn.gelu(acc + b1_ref[...])
    return jnp.dot(hid.astype(bf16), w2_ref[...], preferred_element_type=jnp.float32) + b2_ref[...]


def _nsa_sample_kernel(pt_ref, kcp_ref, vcp_ref, ksp_ref, vsp_ref, kwb_ref, vwb_ref, q_ref, gate_ref,
                       ksn_ref, vsn_ref, kwn_ref, vwn_ref, pek_ref, pev_ref,
                       w1k_ref, b1k_ref, w2k_ref, b2k_ref, w1v_ref, b1v_ref, w2v_ref, b2v_ref, gkc_ref,
                       e_ref, o_ref, kc_scr, vc_scr, ks_scr, vs_scr, kw_scr, vw_scr, sem,
                       *, past, n_pages, n_new):
    f32, bf16 = jnp.float32, jnp.bfloat16
    b = pl.program_id(0)
    slot = b % 2

    def seq_copies(seq, dst_slot):
        cps = []
        for pg in range(n_pages):
            page = pt_ref[seq, pg]
            rows = pl.ds(pg * PAGE_SIZE, PAGE_SIZE)
            for g in range(A_KV):
                for pool_ref, scr in ((kcp_ref, kc_scr), (vcp_ref, vc_scr), (ksp_ref, ks_scr), (vsp_ref, vs_scr)):
                    cps.append(pltpu.make_async_copy(pool_ref.at[page, :, g, :], scr.at[dst_slot, g, rows, :],
                                                     sem.at[dst_slot]))
        for g in range(A_KV):
            for buf_ref, scr in ((kwb_ref, kw_scr), (vwb_ref, vw_scr)):
                cps.append(pltpu.make_async_copy(buf_ref.at[seq, :, g, :], scr.at[dst_slot, g, pl.ds(0, wb), :],
                                                 sem.at[dst_slot]))
        return cps

    wb = kwb_ref.shape[1]

    @pl.when(b == 0)
    def _():
        for cp in seq_copies(0, 0):
            cp.start()

    @pl.when(b + 1 < pl.num_programs(0))
    def _():
        for cp in seq_copies(b + 1, 1 - slot):
            cp.start()

    for cp in seq_copies(b, slot):
        cp.wait()

    def compute():
        nt_dims = (((1,), (1,)), ((), ()))
        n_blk = past // L_CMP
        half = n_blk // 2
        rows = A_HPG * SQ_PAD
        pad_rows = jnp.zeros((SEL_TAIL - SQ_PAD, A_DH), f32)
        tail = lambda new_ref, g: jnp.concatenate(
            [new_ref[0, :, g * A_DH:(g + 1) * A_DH], pad_rows], axis=0)
        for g in range(A_KV):
            ks_scr[slot, g, pl.ds(past, SEL_TAIL), :] = tail(ksn_ref, g)
            vs_scr[slot, g, pl.ds(past, SEL_TAIL), :] = tail(vsn_ref, g)
            kw_scr[slot, g, pl.ds(wb, SEL_TAIL), :] = tail(kwn_ref, g)
            vw_scr[slot, g, pl.ds(wb, SEL_TAIL), :] = tail(vwn_ref, g)
        kcmp = _compress_seq(kc_scr.at[slot], pek_ref, w1k_ref, b1k_ref, w2k_ref, b2k_ref, n_blk)
        kcmp = kcmp * lax.rsqrt(jnp.mean(kcmp * kcmp, axis=-1, keepdims=True) + EPS) * gkc_ref[...]
        vcmp = _compress_seq(vc_scr.at[slot], pev_ref, w1v_ref, b1v_ref, w2v_ref, b2v_ref, n_blk)

        q_pos = past + (lax.broadcasted_iota(jnp.int32, (SQ_PAD, 1), 0))
        rep = lambda a: jnp.concatenate([a] * A_HPG, axis=0)
        lanes = P_NKEYS
        zero_blk = jnp.zeros((lanes - n_blk, A_DH), f32)
        lane_c = lax.broadcasted_iota(jnp.int32, (1, lanes), 1)
        cmask = jnp.broadcast_to(lane_c < n_blk, (rows, lanes))

        o_c, imps = [], []
        for g in range(A_KV):
            qg = q_ref[0, g]
            kc_g = jnp.concatenate([kcmp[g * n_blk:(g + 1) * n_blk], zero_blk], axis=0).astype(bf16)
            vc_g = jnp.concatenate([vcmp[g * n_blk:(g + 1) * n_blk], zero_blk], axis=0).astype(bf16)
            s_c = lax.dot_general(qg, kc_g, nt_dims, preferred_element_type=f32)
            p_c = _softmax_rows(s_c, cmask)
            o_c.append(jnp.dot(p_c.astype(bf16), vc_g, preferred_element_type=f32))
            ph = p_c[0:SQ_PAD]
            for h in range(1, A_HPG):
                ph = ph + p_c[h * SQ_PAD:(h + 1) * SQ_PAD]
            imps.append(ph + pltpu.roll(ph, lanes - half, axis=1))
        imp = jnp.concatenate(imps, axis=0)
        q_pos4 = jnp.concatenate([q_pos] * A_KV, axis=0)
        cur = q_pos4 // L_SEL
        forced = (lane_c == 0) | (lane_c == cur) | (lane_c == cur - 1)
        imp = jnp.where(lane_c <= cur, jnp.where(forced, FORCE_SCORE, imp), NEG)
        nsb = -(-(past + n_new) // L_SEL)
        sel = _top_n_mask_rows(imp, min(N_SEL, nsb))
        picked = jnp.dot(sel.astype(bf16), e_ref[...], preferred_element_type=f32)
        n_keys = past + SEL_TAIL
        k_pos = lax.broadcasted_iota(jnp.int32, (1, n_keys), 1)
        w_pos = past - wb + lax.broadcasted_iota(jnp.int32, (1, wb + SEL_TAIL), 1)
        rel = q_pos - w_pos
        w_mask = rep((rel >= 0) & (rel < WINDOW) & (w_pos >= 0))

        gl = gate_ref[0]
        outs = []
        for g in range(A_KV):
            qg = q_ref[0, g]
            s_mask = rep((picked[g * SQ_PAD:(g + 1) * SQ_PAD] > 0.5) & (k_pos <= q_pos))
            s_s = lax.dot_general(qg, ks_scr[slot, g].astype(bf16), nt_dims, preferred_element_type=f32)
            p_s = _softmax_rows(s_s, s_mask)
            o_s = jnp.dot(p_s.astype(bf16), vs_scr[slot, g].astype(bf16), preferred_element_type=f32)
            s_w = lax.dot_general(qg, kw_scr[slot, g].astype(bf16), nt_dims, preferred_element_type=f32)
            p_w = _softmax_rows(s_w, w_mask)
            o_w = jnp.dot(p_w.astype(bf16), vw_scr[slot, g].astype(bf16), preferred_element_type=f32)
            gs = 1.0 / (1.0 + jnp.exp(-gl[g]))
            o = o_c[g] * gs[:, 0:1] + o_s * gs[:, 1:2] + o_w * gs[:, 2:3]
            outs.extend([o[h * SQ_PAD:(h + 1) * SQ_PAD] for h in range(A_HPG)])
        o_ref[0] = jnp.concatenate(outs, axis=1)

    compute()


def nsa_sample_pallas(q, ksn, vsn, kwn, vwn, gate_logits, pool_kc, pool_vc, pool_ks, pool_vs,
                      buf_kw, buf_vw, page_table, cmp_k, cmp_v, g_kc):
    f32, bf16 = jnp.float32, jnp.bfloat16
    B, S = q.shape[:2]
    assert S <= SQ_PAD and S < L_CMP
    n_pages = page_table.shape[1]
    past = n_pages * PAGE_SIZE
    wb = buf_kw.shape[1]
    kvw = A_KV * A_DH
    pad_q = lambda a: jnp.pad(a, ((0, 0), (0, SQ_PAD - S)) + ((0, 0),) * (a.ndim - 2))
    qs = pad_q((q.astype(f32) * A_DH ** -0.5).astype(bf16)).reshape(B, SQ_PAD, A_KV, A_HPG, A_DH)
    qs = qs.transpose(0, 2, 3, 1, 4).reshape(B, A_KV, A_HPG * SQ_PAD, A_DH)
    gl = pad_q(gate_logits).reshape(B, SQ_PAD, A_KV, A_HPG, 3).transpose(0, 2, 3, 1, 4)
    gl = gl.reshape(B, A_KV, A_HPG * SQ_PAD, 3)
    new = lambda a: pad_q(a.reshape(B, S, kvw))
    pe_k, w1_k, b1_k, w2_k, b2_k = cmp_k
    pe_v, w1_v, b1_v, w2_v, b2_v = cmp_v
    row = lambda a: a.reshape(1, -1).astype(f32)
    n_keys = past + SEL_TAIL
    expand = (jnp.arange(P_NKEYS, dtype=jnp.int32)[:, None]
              == (jnp.arange(n_keys, dtype=jnp.int32) // L_SEL)[None, :]).astype(bf16)

    hbm = pl.BlockSpec(memory_space=pl.ANY)
    seq3 = lambda r, c: pl.BlockSpec((1, r, c), lambda b, pt: (b, 0, 0))
    seq4 = lambda a, r, c: pl.BlockSpec((1, a, r, c), lambda b, pt: (b, 0, 0, 0))
    full = lambda a: pl.BlockSpec(a.shape, lambda b, pt: (0,) * a.ndim, pipeline_mode=pl.Buffered(1))
    consts = [pe_k.astype(f32), pe_v.astype(f32), w1_k.astype(bf16), row(b1_k), w2_k.astype(bf16), row(b2_k),
              w1_v.astype(bf16), row(b1_v), w2_v.astype(bf16), row(b2_v), row(g_kc), expand]
    slots = 2
    out = pl.pallas_call(
        functools.partial(_nsa_sample_kernel, past=past, n_pages=n_pages, n_new=S),
        grid_spec=pltpu.PrefetchScalarGridSpec(
            num_scalar_prefetch=1,
            grid=(B,),
            in_specs=[hbm] * 6
                     + [seq4(A_KV, A_HPG * SQ_PAD, A_DH), seq4(A_KV, A_HPG * SQ_PAD, 3)]
                     + [seq3(SQ_PAD, kvw)] * 4
                     + [full(c) for c in consts],
            out_specs=pl.BlockSpec((1, SQ_PAD, A_W), lambda b, pt: (b, 0, 0)),
            scratch_shapes=[pltpu.VMEM((slots, A_KV, past, A_DH), f32), pltpu.VMEM((slots, A_KV, past, A_DH), f32),
                            pltpu.VMEM((slots, A_KV, n_keys, A_DH), f32), pltpu.VMEM((slots, A_KV, n_keys, A_DH), f32),
                            pltpu.VMEM((slots, A_KV, wb + SEL_TAIL, A_DH), f32),
                            pltpu.VMEM((slots, A_KV, wb + SEL_TAIL, A_DH), f32),
                            pltpu.SemaphoreType.DMA((slots,))]),
        out_shape=jax.ShapeDtypeStruct((B, SQ_PAD, A_W), f32),
        compiler_params=pltpu.CompilerParams(
            dimension_semantics=("arbitrary",), vmem_limit_bytes=VMEM_LIMIT_BYTES),
        name="nsa_sample",
    )(page_table, pool_kc, pool_vc, pool_ks, pool_vs, buf_kw, buf_vw, qs, gl,
      new(ksn), new(vsn), new(kwn), new(vwn), *consts)
    return out[:, :S]


def merge_groups(ret_o, rg, nsa_o, r_gn, w_out):
    B, T = rg.shape[:2]
    r = rmsnorm(ret_o, r_gn).reshape(B, T, R_W) * jax.nn.silu(rg.astype(jnp.float32))
    mix = jnp.concatenate([r.astype(rg.dtype), nsa_o.reshape(B, T, A_W).astype(rg.dtype)], axis=-1)
    return mmnd(mix, w_out)


def memory_kv(mem, g_memtok, w_mk, w_mv, m_kn):
    B, M, _ = mem.shape
    m = rmsnorm(mem, g_memtok)
    k = rmsnorm(mmnd(m, w_mk).reshape(B, M, M_HEADS, M_DH), m_kn)
    v = mmnd(m, w_mv).reshape(B, M, M_HEADS, M_DH)
    return k, v


def memory_attend(h, mk, mv, w_mq, m_qn, w_mo):
    B, T, _ = h.shape
    q = rmsnorm(mmnd(h, w_mq).reshape(B, T, M_HEADS, M_DH), m_qn)
    s = jnp.einsum('bthd,bmhd->bhtm', q.astype(jnp.float32), mk.astype(jnp.float32)) * M_DH ** -0.5
    p = jax.nn.softmax(s, axis=-1)
    o = jnp.einsum('bhtm,bmhd->bthd', p, mv.astype(jnp.float32)).astype(h.dtype)
    return mmnd(o.reshape(B, T, M_W), w_mo)


def kernel(x_prompt, x_sample, mem_prompt, cache_k_cmp, cache_v_cmp, cache_k_sel, cache_v_sel, state_k_win, state_v_win, state_ret, cache_mem_k, cache_mem_v, page_table, g_mix, w_in, r_gn, a_qn, a_kcn, a_ksn, a_kwn, cmp_pe_k, cmp_w1_k, cmp_b1_k, cmp_w2_k, cmp_b2_k, cmp_pe_v, cmp_w1_v, cmp_b1_v, cmp_w2_v, cmp_b2_v, w_out, g_xattn, g_memtok, w_mq, w_mk, w_mv, m_qn, m_kn, w_mo, g_ffn, w_pq, sub_k1, sub_k2, exp_u, exp_v):
    cmp_k = (cmp_pe_k, cmp_w1_k, cmp_b1_k, cmp_w2_k, cmp_b2_k)
    cmp_v = (cmp_pe_v, cmp_w1_v, cmp_b1_v, cmp_w2_v, cmp_b2_v)
    xp, xs = x_prompt, x_sample

    T = xp.shape[1]
    w_in = w_in.astype(jnp.bfloat16)
    w_out = w_out.astype(jnp.bfloat16)
    rq, rk, rv, rg, aq, akc, avc, aks, avs, akw, avw, ag, ag_p = split_mix(
        rmsnorm(xp, g_mix).astype(jnp.bfloat16), w_in)
    pos = jnp.arange(T)
    ret_o, ret_p = retention_prompt(rotary(rq, pos) * R_DK ** -0.5, rotary(rk, pos), rv)
    aq, aks, akw = rmsnorm(aq, a_qn), rmsnorm(aks, a_ksn), rmsnorm(akw, a_kwn)
    kc = rmsnorm(compress(akc, *cmp_k), a_kcn)
    vc = compress(avc, *cmp_v)
    nsa_o = nsa_prompt_pallas(aq[0], kc[0], vc[0], aks[0], avs[0], akw[0], avw[0], ag_p[0])[None]
    hp = xp + merge_groups(ret_o, rg, nsa_o, r_gn, w_out)
    mem_k_p, mem_v_p = memory_kv(mem_prompt, g_memtok, w_mk, w_mv, m_kn)
    hp = hp + memory_attend(rmsnorm(hp, g_xattn), mem_k_p, mem_v_p, w_mq, m_qn, w_mo)
    wl = min(WINDOW, T)
    k_cmp_p, v_cmp_p, k_sel_p, v_sel_p = akc, avc, aks, avs
    k_win_p, v_win_p = akw[:, T - wl:], avw[:, T - wl:]
    ret_p = ret_p.astype(xp.dtype)

    S = xs.shape[1]
    past = page_table.shape[1] * PAGE_SIZE
    rq, rk, rv, rg, aq, akc, avc, aks, avs, akw, avw, ag, ag_s = split_mix(
        rmsnorm(xs, g_mix).astype(jnp.bfloat16), w_in)
    pos = past + jnp.arange(S)
    ret_s, ret_o = retention_chunk(state_ret.astype(jnp.float32), rotary(rq, pos) * R_DK ** -0.5,
                                   rotary(rk, pos), rv)
    aq, aks, akw = rmsnorm(aq, a_qn), rmsnorm(aks, a_ksn), rmsnorm(akw, a_kwn)
    nsa_o = nsa_sample_pallas(aq, aks, avs, akw, avw, ag_s, cache_k_cmp, cache_v_cmp, cache_k_sel,
                              cache_v_sel, state_k_win, state_v_win, page_table, cmp_k, cmp_v, a_kcn)
    hs = xs + merge_groups(ret_o, rg, nsa_o, r_gn, w_out)
    hs = hs + memory_attend(rmsnorm(hs, g_xattn), cache_mem_k, cache_mem_v, w_mq, m_qn, w_mo)
    n_p = hp.shape[0] * hp.shape[1]
    tokens = jnp.concatenate([rmsnorm(hp, g_ffn).reshape(-1, D_MODEL), rmsnorm(hs, g_ffn).reshape(-1, D_MODEL)], axis=0)
    exp_vt = jnp.swapaxes(exp_v.reshape(-1, PEER_TE, D_MODEL), 1, 2).astype(jnp.bfloat16)
    y = peer_pallas(tokens, w_pq, sub_k1, sub_k2, exp_u.astype(jnp.bfloat16), exp_vt)
    xp = hp + y[:n_p].reshape(hp.shape)
    xs = hs + y[n_p:].reshape(hs.shape)
    wb = state_k_win.shape[1]
    k_cmp_s, v_cmp_s, k_sel_s, v_sel_s = akc, avc, aks, avs
    k_win_s = jnp.concatenate([state_k_win, akw.astype(state_k_win.dtype)], axis=1)[:, -wb:]
    v_win_s = jnp.concatenate([state_v_win, avw.astype(state_v_win.dtype)], axis=1)[:, -wb:]
    ret_s = ret_s.astype(state_ret.dtype)
    return (xp, xs, k_cmp_p, v_cmp_p, k_sel_p, v_sel_p, k_win_p, v_win_p, ret_p, mem_k_p, mem_v_p,
            k_cmp_s, v_cmp_s, k_sel_s, v_sel_s, k_win_s, v_win_s, ret_s)
```

```python
import functools

import jax
import jax.numpy as jnp
import numpy as np
from jax import lax
from jax.experimental import pallas as pl
from jax.experimental.pallas import tpu as pltpu

D_MODEL = 4096
PAGE_SIZE = 128
R_HEADS = 8
R_DK = 256
R_DV = 256
R_CHUNK = 128
ROPE_BASE = 10000.0
A_HEADS = 16
A_KV = 4
A_DH = 128
A_HPG = A_HEADS // A_KV
L_CMP = 32
CMP_HID = 256
L_SEL = 64
N_SEL = 16
WINDOW = 512
Q_BLOCK = 128
FORCE_SCORE = 1e4
NEG = -1e30
M_HEADS = 4
M_DH = 128
M_W = M_HEADS * M_DH
P_HEADS = 8
P_NKEYS = 128
P_DQ = 256
P_TOPK = 16
P_BLOCK = 128
EPS = 1e-6
R_W = R_HEADS * R_DV
A_W = A_HEADS * A_DH
IN_SIZES = (R_HEADS * R_DK, R_HEADS * R_DK, R_W, R_W, A_W) + (A_KV * A_DH,) * 6 + (A_HEADS * 3,)

VMEM_LIMIT_BYTES = 56 * 1024 * 1024


def _mm_kernel(a_ref, b_ref, o_ref, acc_ref):
    k = pl.program_id(2)

    @pl.when(k == 0)
    def _():
        acc_ref[...] = jnp.zeros_like(acc_ref)

    acc_ref[...] += jnp.dot(a_ref[...].astype(jnp.bfloat16), b_ref[...].astype(jnp.bfloat16),
                            preferred_element_type=jnp.float32)

    @pl.when(k == pl.num_programs(2) - 1)
    def _():
        o_ref[...] = acc_ref[...]


def mm(a, b, tm=1024, tn=1024, tk=1024):
    m, kdim = a.shape
    _, n = b.shape
    tm = min(tm, m)
    tn = min(tn, n)
    tk = min(tk, kdim)
    assert m % tm == 0 and kdim % tk == 0
    grid = (m // tm, pl.cdiv(n, tn), kdim // tk)
    return pl.pallas_call(
        _mm_kernel,
        grid=grid,
        in_specs=[pl.BlockSpec((tm, tk), lambda i, j, k: (i, k)),
                  pl.BlockSpec((tk, tn), lambda i, j, k: (k, j))],
        out_specs=pl.BlockSpec((tm, tn), lambda i, j, k: (i, j)),
        out_shape=jax.ShapeDtypeStruct((m, n), jnp.float32),
        scratch_shapes=[pltpu.VMEM((tm, tn), jnp.float32)],
        compiler_params=pltpu.CompilerParams(
            dimension_semantics=("parallel", "parallel", "arbitrary"),
            vmem_limit_bytes=VMEM_LIMIT_BYTES),
        name="mm",
    )(a, b)


def mmnd(a, b):
    lead = a.shape[:-1]
    out = mm(a.reshape(-1, a.shape[-1]), b)
    return out.reshape(lead + (b.shape[-1],))


def _mm_nt_kernel(a_ref, b_ref, o_ref):
    o_ref[...] = lax.dot_general(a_ref[...].astype(jnp.bfloat16), b_ref[...].astype(jnp.bfloat16),
                                 (((1,), (1,)), ((), ())), preferred_element_type=jnp.float32)


def mm_nt(a, b, tm=512, tn=512):
    m, kdim = a.shape
    n, _ = b.shape
    tm, tn = min(tm, m), min(tn, n)
    assert m % tm == 0 and n % tn == 0
    return pl.pallas_call(
        _mm_nt_kernel,
        grid=(m // tm, n // tn),
        in_specs=[pl.BlockSpec((tm, kdim), lambda i, j: (i, 0)),
                  pl.BlockSpec((tn, kdim), lambda i, j: (j, 0))],
        out_specs=pl.BlockSpec((tm, tn), lambda i, j: (i, j)),
        out_shape=jax.ShapeDtypeStruct((m, n), jnp.float32),
        compiler_params=pltpu.CompilerParams(
            dimension_semantics=("parallel", "parallel"),
            vmem_limit_bytes=VMEM_LIMIT_BYTES),
        name="mm_nt",
    )(a, b)


PEER_TR = 256
PEER_TM = 512
PEER_TE = 512
PEER_SLABS = PEER_TE // P_NKEYS
TAKEN = -3.0e38


def _extract_top(s, n):
    rows = s.shape[0]
    row_id = lax.broadcasted_iota(jnp.int32, s.shape, 0).astype(jnp.float32)
    rank = jnp.full(s.shape, float(n), jnp.float32)
    vals = []
    for k in range(n):
        m = jnp.max(s, axis=0, keepdims=True)
        first = jnp.min(jnp.where(s == m, row_id, float(rows)), axis=0, keepdims=True)
        hit = row_id == first
        rank = jnp.where(hit, float(k), rank)
        s = jnp.where(hit, TAKEN, s)
        vals.append(m)
    return rank, jnp.concatenate(vals, axis=0)


def _peer_route_kernel(qt_ref, k1_ref, k2_ref, r2_ref, e2_ref, n_ref, e1n_ref):
    f32, bf16 = jnp.float32, jnp.bfloat16
    half = P_DQ // 2
    qt = qt_ref[...].astype(bf16)
    s1 = jnp.dot(k1_ref[0].astype(bf16), qt[:half], preferred_element_type=f32)
    s2 = jnp.dot(k2_ref[0].astype(bf16), qt[half:], preferred_element_type=f32)
    rank1, v1 = _extract_top(s1, P_TOPK)
    rank2, v2 = _extract_top(s2, P_TOPK)
    r1_id = lax.broadcasted_iota(jnp.int32, v1.shape, 0).astype(f32)
    cnt = jnp.zeros_like(v1)
    for _ in range(P_TOPK):
        nxt = jnp.full_like(v1, TAKEN)
        for j in range(P_TOPK):
            nxt = jnp.where(cnt == float(j), v2[j:j + 1], nxt)
        front = jnp.where(cnt < float(P_TOPK), v1 + nxt, TAKEN)
        m = jnp.max(front, axis=0, keepdims=True)
        first = jnp.min(jnp.where(front == m, r1_id, float(P_TOPK)), axis=0, keepdims=True)
        cnt = cnt + jnp.where(r1_id == first, 1.0, 0.0)
    ev1 = jnp.exp(v1 - v1[0:1])
    ev2 = jnp.exp(v2 - v2[0:1])
    inner = jnp.zeros_like(v1)
    for j in range(P_TOPK):
        inner = inner + jnp.where(cnt > float(j), ev2[j:j + 1], 0.0)
    z = jnp.sum(ev1 * inner, axis=0, keepdims=True)
    n_dense = jnp.zeros_like(s1)
    for r in range(P_TOPK):
        n_dense = n_dense + jnp.where(rank1 == float(r), cnt[r:r + 1], 0.0)
    r2_ref[0] = rank2
    e2_ref[0] = jnp.exp(s2 - v2[0:1])
    n_ref[0] = n_dense
    e1n_ref[0] = jnp.exp(s1 - v1[0:1]) * (1.0 / z)


def _peer_mix_kernel(x_ref, u_ref, vt_ref, r2_ref, e2_ref, n_ref, e1n_ref, yt_ref, a_scr):
    f32, bf16 = jnp.float32, jnp.bfloat16
    j = pl.program_id(1)

    @pl.when(j == 0)
    def _():
        yt_ref[...] = jnp.zeros_like(yt_ref)
        a_scr[1] = jnp.zeros(a_scr.shape[1:], f32)

    act = jax.nn.gelu(a_scr[(j + 1) % 2])
    a_scr[j % 2] = lax.dot_general(u_ref[...], x_ref[...], (((1,), (1,)), ((), ())),
                                   preferred_element_type=f32)

    tile = jnp.maximum(j - 1, 0)
    parts = []
    for s in range(PEER_SLABS):
        first_key = tile * PEER_SLABS + s
        g = jnp.zeros((P_NKEYS, act.shape[1]), f32)
        for h in range(P_HEADS):
            n_row = n_ref[h, pl.ds(first_key, 1), :]
            w_row = e1n_ref[h, pl.ds(first_key, 1), :]
            g = g + jnp.where(r2_ref[h] < n_row, e2_ref[h] * w_row, 0.0)
        parts.append((g * act[s * P_NKEYS:(s + 1) * P_NKEYS]).astype(bf16))
    w_t = jnp.concatenate(parts, axis=0)
    yt_ref[...] += jnp.dot(vt_ref[0], w_t, preferred_element_type=f32)


def peer_pallas(x, w_pq, sub_k1, sub_k2, exp_u_b, exp_vt_b):
    bf16 = jnp.bfloat16
    n, d = x.shape
    n_exp = exp_u_b.shape[0]
    assert n % PEER_TM == 0 and n % PEER_TR == 0 and n_exp % PEER_TE == 0
    xb = x.astype(bf16)
    qt = mm_nt(jnp.swapaxes(w_pq, 0, 1), xb)
    route_shape = jax.ShapeDtypeStruct((P_HEADS, P_NKEYS, n), jnp.float32)
    key_spec = pl.BlockSpec((1, P_NKEYS, P_DQ // 2), lambda i, h: (h, 0, 0))
    route_out = pl.BlockSpec((1, P_NKEYS, PEER_TR), lambda i, h: (h, 0, i))
    r2, e2, nd, e1n = pl.pallas_call(
        _peer_route_kernel,
        grid=(n // PEER_TR, P_HEADS),
        in_specs=[pl.BlockSpec((P_DQ, PEER_TR), lambda i, h: (h, i)), key_spec, key_spec],
        out_specs=[route_out] * 4,
        out_shape=[route_shape] * 4,
        compiler_params=pltpu.CompilerParams(
            dimension_semantics=("parallel", "parallel"), vmem_limit_bytes=VMEM_LIMIT_BYTES),
        name="peer_route",
    )(qt, sub_k1, sub_k2)
    route_in = pl.BlockSpec((P_HEADS, P_NKEYS, PEER_TM), lambda i, j: (0, 0, i),
                            pipeline_mode=pl.Buffered(1))
    n_tiles = n_exp // PEER_TE
    yt = pl.pallas_call(
        _peer_mix_kernel,
        grid=(n // PEER_TM, n_tiles + 1),
        in_specs=[pl.BlockSpec((PEER_TM, d), lambda i, j: (i, 0), pipeline_mode=pl.Buffered(1)),
                  pl.BlockSpec((PEER_TE, d), lambda i, j: (jnp.minimum(j, n_tiles - 1), 0)),
                  pl.BlockSpec((1, d, PEER_TE), lambda i, j: (jnp.maximum(j - 1, 0), 0, 0)),
                  route_in, route_in, route_in, route_in],
        out_specs=pl.BlockSpec((d, PEER_TM), lambda i, j: (0, i)),
        out_shape=jax.ShapeDtypeStruct((d, n), jnp.float32),
        scratch_shapes=[pltpu.VMEM((2, PEER_TE, PEER_TM), jnp.float32)],
        compiler_params=pltpu.CompilerParams(
            dimension_semantics=("parallel", "arbitrary"), vmem_limit_bytes=VMEM_LIMIT_BYTES),
        name="peer_mix",
    )(xb, exp_u_b, exp_vt_b, r2, e2, nd, e1n)
    return jnp.swapaxes(yt, 0, 1)


def _retention_prompt_kernel(q_ref, k_ref, v_ref, gate_ref, cos_ref, sin_ref, dmask_ref, dq_ref, dk_ref,
                             cdec_ref, gn_ref, o_ref, state_ref, s_scr):
    f32, bf16 = jnp.float32, jnp.bfloat16
    c = pl.program_id(1)

    @pl.when(c == 0)
    def _():
        s_scr[...] = jnp.zeros_like(s_scr)

    cos, sin = cos_ref[...], sin_ref[...]
    half = R_DK // 2

    def rot(x):
        x1, x2 = x[:, :half], x[:, half:]
        return jnp.concatenate([x1 * cos - x2 * sin, x1 * sin + x2 * cos], axis=1)

    q = (rot(q_ref[...]) * R_DK ** -0.5).astype(bf16)
    k = rot(k_ref[...])
    v = v_ref[...].astype(bf16)
    s_old = s_scr[...]
    inner = lax.dot_general(q, k.astype(bf16), (((1,), (1,)), ((), ())),
                            preferred_element_type=f32) * dmask_ref[0]
    o = jnp.dot(inner.astype(bf16), v, preferred_element_type=f32)
    o = o + jnp.dot(q, s_old.astype(bf16), preferred_element_type=f32) * dq_ref[0]
    kdec = (k * dk_ref[0]).astype(bf16)
    s_new = s_old * cdec_ref[0] + lax.dot_general(kdec, v, (((0,), (0,)), ((), ())),
                                                  preferred_element_type=f32)
    s_scr[...] = s_new
    y = o * lax.rsqrt(jnp.mean(o * o, axis=-1, keepdims=True) + EPS) * gn_ref[0]
    gate = gate_ref[...]
    o_ref[...] = y * (gate * (1.0 / (1.0 + jnp.exp(-gate))))

    @pl.when(c == pl.num_programs(1) - 1)
    def _():
        state_ref[0] = s_new


def retention_prompt_pallas(proj, r_gn):
    f32 = jnp.float32
    T = proj.shape[0]
    assert T % R_CHUNK == 0 and R_DK == R_DV
    half = R_DK // 2
    inv = ROPE_BASE ** (-jnp.arange(half, dtype=f32) / half)
    ang = jnp.arange(T, dtype=f32)[:, None] * inv[None, :]
    lg = jnp.log1p(-jnp.exp2(-5.0 - jnp.arange(R_HEADS, dtype=f32)))
    idx = jnp.arange(R_CHUNK, dtype=f32)
    rel = idx[:, None] - idx[None, :]
    dmask = jnp.where(rel[None] >= 0, jnp.exp(jnp.maximum(rel, 0.0)[None] * lg[:, None, None]), 0.0)
    dq = jnp.exp((idx + 1.0)[None, :] * lg[:, None])[:, :, None]
    dk = jnp.exp((R_CHUNK - 1.0 - idx)[None, :] * lg[:, None])[:, :, None]
    cdec = jnp.broadcast_to(jnp.exp(R_CHUNK * lg)[:, None, None], (R_HEADS, 1, R_DV))
    col = lambda base: pl.BlockSpec((R_CHUNK, R_DK), lambda h, c, base=base: (c, base + h))
    per_head = lambda shape: pl.BlockSpec((1,) + shape, lambda h, c: (h, 0, 0))
    trig = pl.BlockSpec((R_CHUNK, half), lambda h, c: (c, 0))
    return pl.pallas_call(
        _retention_prompt_kernel,
        grid=(R_HEADS, T // R_CHUNK),
        in_specs=[col(0), col(R_HEADS), col(2 * R_HEADS), col(3 * R_HEADS), trig, trig,
                  per_head((R_CHUNK, R_CHUNK)), per_head((R_CHUNK, 1)), per_head((R_CHUNK, 1)),
                  per_head((1, R_DV)), per_head((1, R_DV))],
        out_specs=[pl.BlockSpec((R_CHUNK, R_DV), lambda h, c: (c, h)),
                   pl.BlockSpec((1, R_DK, R_DV), lambda h, c: (h, 0, 0))],
        out_shape=[jax.ShapeDtypeStruct((T, R_W), f32),
                   jax.ShapeDtypeStruct((R_HEADS, R_DK, R_DV), f32)],
        scratch_shapes=[pltpu.VMEM((R_DK, R_DV), f32)],
        compiler_params=pltpu.CompilerParams(
            dimension_semantics=("parallel", "arbitrary"), vmem_limit_bytes=VMEM_LIMIT_BYTES),
        name="retention_prompt",
    )(proj, proj, proj, proj, jnp.cos(ang), jnp.sin(ang), dmask, dq, dk, cdec,
      r_gn.astype(f32).reshape(R_HEADS, 1, R_DV))


def rmsnorm(x, g):
    xf = x.astype(jnp.float32)
    y = xf * lax.rsqrt(jnp.mean(xf * xf, axis=-1, keepdims=True) + EPS)
    return (y * g.astype(jnp.float32)).astype(x.dtype)


def rotary(x, pos):
    half = x.shape[-1] // 2
    inv = ROPE_BASE ** (-jnp.arange(half, dtype=jnp.float32) / half)
    ang = pos.astype(jnp.float32)[:, None] * inv[None, :]
    cos, sin = jnp.cos(ang)[:, None, :], jnp.sin(ang)[:, None, :]
    xf = x.astype(jnp.float32)
    x1, x2 = xf[..., :half], xf[..., half:]
    return jnp.concatenate([x1 * cos - x2 * sin, x1 * sin + x2 * cos], axis=-1).astype(x.dtype)


def masked_softmax(s, mask):
    p = jax.nn.softmax(jnp.where(mask, s, NEG), axis=-1)
    return jnp.where(mask, p, 0.0)


def split_mix(h, w_in):
    B, T, _ = h.shape
    cuts = np.cumsum(IN_SIZES)[:-1].tolist()
    rq, rk, rv, rg, aq, akc, avc, aks, avs, akw, avw, ag = jnp.split(mmnd(h, w_in), cuts, axis=-1)
    r = lambda a, n, d: a.reshape(B, T, n, d)
    return (r(rq, R_HEADS, R_DK), r(rk, R_HEADS, R_DK), r(rv, R_HEADS, R_DV), rg,
            r(aq, A_HEADS, A_DH), r(akc, A_KV, A_DH), r(avc, A_KV, A_DH), r(aks, A_KV, A_DH),
            r(avs, A_KV, A_DH), r(akw, A_KV, A_DH), r(avw, A_KV, A_DH),
            jax.nn.sigmoid(ag.astype(jnp.float32)).reshape(B, T, A_HEADS, 3), ag)


def retention_chunk(S, q, k, v):
    q, k, v = q.astype(jnp.float32), k.astype(jnp.float32), v.astype(jnp.float32)
    C = q.shape[1]
    lg = jnp.log1p(-jnp.exp2(-5.0 - jnp.arange(R_HEADS, dtype=jnp.float32)))
    idx = jnp.arange(C, dtype=jnp.float32)
    rel = idx[:, None] - idx[None, :]
    dmask = jnp.where(rel[None] >= 0, jnp.exp(jnp.maximum(rel, 0.0)[None] * lg[:, None, None]), 0.0)
    inner = jnp.einsum('bihd,bjhd->bhij', q, k) * dmask[None]
    o = jnp.einsum('bhij,bjhe->bihe', inner, v)
    o = o + jnp.einsum('bihd,bhde->bihe', q, S) * jnp.exp((idx + 1.0)[:, None] * lg[None, :])[None, :, :, None]
    kdec = k * jnp.exp((C - 1.0 - idx)[:, None] * lg[None, :])[None, :, :, None]
    S_new = S * jnp.exp(C * lg)[None, :, None, None] + jnp.einsum('bjhd,bjhe->bhde', kdec, v)
    return S_new, o


def compress(rows, pe, w1, b1, w2, b2):
    T = rows.shape[-3]
    nc = T // L_CMP
    lead = rows.shape[:-3]
    blk = rows[..., :nc * L_CMP, :, :].reshape(lead + (nc, L_CMP, A_KV, A_DH)) + pe[:, None, :]
    blk = jnp.swapaxes(blk, -3, -2).reshape(lead + (nc, A_KV, L_CMP * A_DH))
    return mmnd(jax.nn.gelu(mmnd(blk, w1) + b1), w2) + b2


NSA_TK = 512
N_TOPK_NEG = -3.0e38


def _softmax_rows(s, mask):
    s = jnp.where(mask, s, NEG)
    m = jnp.max(s, axis=-1, keepdims=True)
    e = jnp.where(mask, jnp.exp(s - m), 0.0)
    l = jnp.sum(e, axis=-1, keepdims=True)
    return e * (1.0 / jnp.maximum(l, 1e-30))


def _top_n_mask_cols(score_t, n):
    rows = score_t.shape[0]
    row_id = lax.broadcasted_iota(jnp.int32, score_t.shape, 0).astype(jnp.float32)

    def one(_, c):
        sc, sel = c
        m = jnp.max(sc, axis=0, keepdims=True)
        first = jnp.min(jnp.where(sc == m, row_id, float(rows)), axis=0, keepdims=True)
        hit = row_id == first
        return jnp.where(hit, N_TOPK_NEG, sc), jnp.where(hit, 1.0, sel)

    _, sel = lax.fori_loop(0, n, one, (score_t, jnp.zeros_like(score_t)), unroll=True)
    return sel


def _nsa_prompt_kernel(q_ref, gate_ref, kc_ref, vc_ref, ks_ref, vs_ref, kw_ref, vw_ref, e_ref, o_ref,
                       *, n_win_keys):
    f32, bf16 = jnp.float32, jnp.bfloat16
    qb = pl.program_id(1)
    start = qb * Q_BLOCK
    nt_dims = (((1,), (1,)), ((), ()))
    q = q_ref[...]
    qg = jnp.concatenate([q[:, h * A_DH:(h + 1) * A_DH] for h in range(A_HPG)], axis=0)
    q_pos = start + lax.broadcasted_iota(jnp.int32, (Q_BLOCK, 1), 0)
    rep = lambda a: jnp.concatenate([a] * A_HPG, axis=0)

    kc, vc = kc_ref[0], vc_ref[0]
    nc = kc.shape[0]
    nsb = nc // 2
    s_c = lax.dot_general(qg, kc, nt_dims, preferred_element_type=f32)
    lane_c = lax.broadcasted_iota(jnp.int32, (1, nc), 1)
    c_orig = jnp.where(lane_c < nsb, 2 * lane_c, 2 * (lane_c - nsb) + 1)
    cmask = ((c_orig + 1) * L_CMP - 1) <= q_pos
    p_c = _softmax_rows(s_c, rep(cmask))
    o_c = jnp.dot(p_c.astype(bf16), vc, preferred_element_type=f32)

    ph = p_c[0:Q_BLOCK]
    for h in range(1, A_HPG):
        ph = ph + p_c[h * Q_BLOCK:(h + 1) * Q_BLOCK]
    imp = ph[:, :nsb] + ph[:, nsb:]
    blk = lax.broadcasted_iota(jnp.int32, (1, nsb), 1)
    cur = q_pos // L_SEL
    forced = (blk == 0) | (blk == cur) | (blk == cur - 1)
    imp = jnp.where(blk <= cur, jnp.where(forced, FORCE_SCORE, imp), NEG)
    sel = _top_n_mask_cols(imp.T, min(N_SEL, nsb)).T
    sel_b = sel.astype(bf16)

    n_tiles = (start + Q_BLOCK + NSA_TK - 1) // NSA_TK

    def sel_step(j, carry):
        m, l, acc = carry
        off = pl.multiple_of(j * NSA_TK, NSA_TK)
        k = ks_ref[0, pl.ds(off, NSA_TK), :]
        v = vs_ref[0, pl.ds(off, NSA_TK), :]
        s = lax.dot_general(qg, k, nt_dims, preferred_element_type=f32)
        picked = jnp.dot(sel_b, e_ref[j], preferred_element_type=f32)
        k_pos = off + lax.broadcasted_iota(jnp.int32, (1, NSA_TK), 1)
        bias = jnp.where((picked > 0.5) & (k_pos <= q_pos), 0.0, NEG)
        s = s + rep(bias)
        m_new = jnp.maximum(m, jnp.max(s, axis=-1, keepdims=True))
        alpha = jnp.exp(m - m_new)
        p = jnp.exp(s - m_new)
        l = alpha * l + jnp.sum(p, axis=-1, keepdims=True)
        acc = alpha * acc + jnp.dot(p.astype(bf16), v, preferred_element_type=f32)
        return m_new, l, acc

    rows = A_HPG * Q_BLOCK
    m_s, l_s, acc_s = lax.fori_loop(
        0, n_tiles, sel_step,
        (jnp.full((rows, 1), NEG, f32), jnp.zeros((rows, 1), f32), jnp.zeros((rows, A_DH), f32)))
    o_s = acc_s * (1.0 / l_s)

    base = pl.multiple_of(jnp.maximum(start + Q_BLOCK - n_win_keys, 0), Q_BLOCK)
    kw = kw_ref[0, pl.ds(base, n_win_keys), :]
    vw = vw_ref[0, pl.ds(base, n_win_keys), :]
    s_w = lax.dot_general(qg, kw, nt_dims, preferred_element_type=f32)
    rel = q_pos - (base + lax.broadcasted_iota(jnp.int32, (1, n_win_keys), 1))
    p_w = _softmax_rows(s_w, rep((rel >= 0) & (rel < WINDOW)))
    o_w = jnp.dot(p_w.astype(bf16), vw, preferred_element_type=f32)

    gl = gate_ref[0]
    g = 1.0 / (1.0 + jnp.exp(-gl))
    outs = []
    for h in range(A_HPG):
        r = slice(h * Q_BLOCK, (h + 1) * Q_BLOCK)
        outs.append(o_c[r] * g[:, 3 * h:3 * h + 1] + o_s[r] * g[:, 3 * h + 1:3 * h + 2]
                    + o_w[r] * g[:, 3 * h + 2:3 * h + 3])
    o_ref[...] = jnp.concatenate(outs, axis=-1)


def nsa_prompt_pallas(q, kc, vc, ks, vs, kw, vw, gate_logits):
    bf16 = jnp.bfloat16
    T = q.shape[0]
    nc = kc.shape[0]
    nsb = T // L_SEL
    assert T % NSA_TK == 0 and nc == 2 * nsb
    n_win_keys = min(WINDOW + Q_BLOCK, T)
    qs = (q.astype(jnp.float32) * A_DH ** -0.5).astype(bf16).reshape(T, A_W)
    grp = lambda a: jnp.swapaxes(a, 0, 1).astype(bf16)
    perm = lambda a: jnp.concatenate([a[0::2], a[1::2]], axis=0)
    gl = jnp.swapaxes(gate_logits.reshape(T, A_KV, A_HPG * 3), 0, 1)
    n_t = T // NSA_TK
    key_blk = (jnp.arange(T, dtype=jnp.int32) // L_SEL).reshape(n_t, 1, NSA_TK)
    expand = (jnp.arange(nsb, dtype=jnp.int32)[None, :, None] == key_blk).astype(bf16)
    kv_spec = lambda rows: pl.BlockSpec((1, rows, A_DH), lambda g, i: (g, 0, 0))
    return pl.pallas_call(
        functools.partial(_nsa_prompt_kernel, n_win_keys=n_win_keys),
        grid=(A_KV, T // Q_BLOCK),
        in_specs=[pl.BlockSpec((Q_BLOCK, A_HPG * A_DH), lambda g, i: (i, g)),
                  pl.BlockSpec((1, Q_BLOCK, A_HPG * 3), lambda g, i: (g, i, 0)),
                  kv_spec(nc), kv_spec(nc), kv_spec(T), kv_spec(T), kv_spec(T), kv_spec(T),
                  pl.BlockSpec((n_t, nsb, NSA_TK), lambda g, i: (0, 0, 0))],
        out_specs=pl.BlockSpec((Q_BLOCK, A_HPG * A_DH), lambda g, i: (i, g)),
        out_shape=jax.ShapeDtypeStruct((T, A_W), jnp.float32),
        compiler_params=pltpu.CompilerParams(
            dimension_semantics=("parallel", "arbitrary"),
            vmem_limit_bytes=VMEM_LIMIT_BYTES),
        name="nsa_prompt",
    )(qs, gl, grp(perm(kc)), grp(perm(vc)), grp(ks), grp(vs), grp(kw), grp(vw), expand)


SQ_PAD = 8
SEL_TAIL = 128


def _top_n_mask_rows(score, n):
    cols = score.shape[1]
    col_id = lax.broadcasted_iota(jnp.int32, score.shape, 1).astype(jnp.float32)

    def one(_, c):
        sc, sel = c
        m = jnp.max(sc, axis=1, keepdims=True)
        first = jnp.min(jnp.where(sc == m, col_id, float(cols)), axis=1, keepdims=True)
        hit = col_id == first
        return jnp.where(hit, N_TOPK_NEG, sc), jnp.where(hit, 1.0, sel)

    _, sel = lax.fori_loop(0, n, one, (score, jnp.zeros_like(score)), unroll=True)
    return sel


CMP_CHUNK = 8


def _compress_seq(src, pe_ref, perm_ref, w1_ref, b1_ref, w2_ref, b2_ref, n_blk):
    f32, bf16 = jnp.float32, jnp.bfloat16
    chunk_rows = CMP_CHUNK * L_CMP
    regrouped = []
    for g in range(A_KV):
        for j in range(n_blk // CMP_CHUNK):
            x = (src[g, j * chunk_rows:(j + 1) * chunk_rows, :] + pe_ref[...]).astype(bf16)
            regrouped.append(jnp.dot(perm_ref[...], x, preferred_element_type=f32))
    acc = jnp.zeros((A_KV * n_blk, CMP_HID), f32)
    for l in range(L_CMP):
        xg = jnp.concatenate([y[l * CMP_CHUNK:(l + 1) * CMP_CHUNK] for y in regrouped], axis=0).astype(bf16)
        acc = acc + jnp.dot(xg, w1_ref[l * A_DH:(l + 1) * A_DH, :], preferred_element_type=jnp.float32)
    hid = jax.nn.gelu(acc + b1_ref[...])
    return jnp.dot(hid.astype(bf16), w2_ref[...], preferred_element_type=jnp.float32) + b2_ref[...]


def _nsa_sample_kernel(pt_ref, kcp_ref, vcp_ref, ksp_ref, vsp_ref, kwb_ref, vwb_ref, q_ref, gate_ref,
                       ksn_ref, vsn_ref, kwn_ref, vwn_ref, pek_ref, pev_ref, perm_ref,
                       w1k_ref, b1k_ref, w2k_ref, b2k_ref, w1v_ref, b1v_ref, w2v_ref, b2v_ref, gkc_ref,
                       e_ref, o_ref, kc_scr, vc_scr, ks_scr, vs_scr, kw_scr, vw_scr, sem,
                       *, past, n_pages, n_new):
    f32, bf16 = jnp.float32, jnp.bfloat16
    b = pl.program_id(0)
    slot = b % 2

    def seq_copies(seq, dst_slot):
        cps = []
        for pg in range(n_pages):
            page = pt_ref[seq, pg]
            rows = pl.ds(pg * PAGE_SIZE, PAGE_SIZE)
            for g in range(A_KV):
                for pool_ref, scr in ((kcp_ref, kc_scr), (vcp_ref, vc_scr), (ksp_ref, ks_scr), (vsp_ref, vs_scr)):
                    cps.append(pltpu.make_async_copy(pool_ref.at[page, :, g, :], scr.at[dst_slot, g, rows, :],
                                                     sem.at[dst_slot]))
        for g in range(A_KV):
            for buf_ref, scr in ((kwb_ref, kw_scr), (vwb_ref, vw_scr)):
                cps.append(pltpu.make_async_copy(buf_ref.at[seq, :, g, :], scr.at[dst_slot, g, pl.ds(0, wb), :],
                                                 sem.at[dst_slot]))
        return cps

    wb = kwb_ref.shape[1]

    @pl.when(b == 0)
    def _():
        for cp in seq_copies(0, 0):
            cp.start()

    @pl.when(b + 1 < pl.num_programs(0))
    def _():
        for cp in seq_copies(b + 1, 1 - slot):
            cp.start()

    for cp in seq_copies(b, slot):
        cp.wait()

    def compute():
        nt_dims = (((1,), (1,)), ((), ()))
        n_blk = past // L_CMP
        half = n_blk // 2
        rows = A_HPG * SQ_PAD
        pad_rows = jnp.zeros((SEL_TAIL - SQ_PAD, A_DH), f32)
        tail = lambda new_ref, g: jnp.concatenate(
            [new_ref[0, :, g * A_DH:(g + 1) * A_DH], pad_rows], axis=0)
        for g in range(A_KV):
            ks_scr[slot, g, pl.ds(past, SEL_TAIL), :] = tail(ksn_ref, g)
            vs_scr[slot, g, pl.ds(past, SEL_TAIL), :] = tail(vsn_ref, g)
            kw_scr[slot, g, pl.ds(wb, SEL_TAIL), :] = tail(kwn_ref, g)
            vw_scr[slot, g, pl.ds(wb, SEL_TAIL), :] = tail(vwn_ref, g)
        kcmp = _compress_seq(kc_scr.at[slot], pek_ref, perm_ref, w1k_ref, b1k_ref, w2k_ref, b2k_ref, n_blk)
        kcmp = kcmp * lax.rsqrt(jnp.mean(kcmp * kcmp, axis=-1, keepdims=True) + EPS) * gkc_ref[...]
        vcmp = _compress_seq(vc_scr.at[slot], pev_ref, perm_ref, w1v_ref, b1v_ref, w2v_ref, b2v_ref, n_blk)

        q_pos = past + (lax.broadcasted_iota(jnp.int32, (SQ_PAD, 1), 0))
        rep = lambda a: jnp.concatenate([a] * A_HPG, axis=0)
        lanes = P_NKEYS
        zero_blk = jnp.zeros((lanes - n_blk, A_DH), f32)
        lane_c = lax.broadcasted_iota(jnp.int32, (1, lanes), 1)
        cmask = jnp.broadcast_to(lane_c < n_blk, (rows, lanes))

        o_c, imps = [], []
        for g in range(A_KV):
            qg = q_ref[0, g]
            kc_g = jnp.concatenate([kcmp[g * n_blk:(g + 1) * n_blk], zero_blk], axis=0).astype(bf16)
            vc_g = jnp.concatenate([vcmp[g * n_blk:(g + 1) * n_blk], zero_blk], axis=0).astype(bf16)
            s_c = lax.dot_general(qg, kc_g, nt_dims, preferred_element_type=f32)
            p_c = _softmax_rows(s_c, cmask)
            o_c.append(jnp.dot(p_c.astype(bf16), vc_g, preferred_element_type=f32))
            ph = p_c[0:SQ_PAD]
            for h in range(1, A_HPG):
                ph = ph + p_c[h * SQ_PAD:(h + 1) * SQ_PAD]
            imps.append(ph + pltpu.roll(ph, lanes - 1, axis=1))
        imp = jnp.concatenate(imps, axis=0)
        blk = lane_c // 2
        q_pos4 = jnp.concatenate([q_pos] * A_KV, axis=0)
        cur = q_pos4 // L_SEL
        forced = (blk == 0) | (blk == cur) | (blk == cur - 1)
        imp = jnp.where(blk <= cur, jnp.where(forced, FORCE_SCORE, imp), NEG)
        imp = jnp.where(lane_c % 2 == 0, imp, N_TOPK_NEG)
        nsb = -(-(past + n_new) // L_SEL)
        sel = _top_n_mask_rows(imp, min(N_SEL, nsb))
        picked = jnp.dot(sel.astype(bf16), e_ref[...], preferred_element_type=f32)
        n_keys = past + SEL_TAIL
        k_pos = lax.broadcasted_iota(jnp.int32, (1, n_keys), 1)
        w_pos = past - wb + lax.broadcasted_iota(jnp.int32, (1, wb + SEL_TAIL), 1)
        rel = q_pos - w_pos
        w_mask = rep((rel >= 0) & (rel < WINDOW) & (w_pos >= 0))

        gl = gate_ref[0]
        outs = []
        for g in range(A_KV):
            qg = q_ref[0, g]
            s_mask = rep((picked[g * SQ_PAD:(g + 1) * SQ_PAD] > 0.5) & (k_pos <= q_pos))
            s_s = lax.dot_general(qg, ks_scr[slot, g].astype(bf16), nt_dims, preferred_element_type=f32)
            p_s = _softmax_rows(s_s, s_mask)
            o_s = jnp.dot(p_s.astype(bf16), vs_scr[slot, g].astype(bf16), preferred_element_type=f32)
            s_w = lax.dot_general(qg, kw_scr[slot, g].astype(bf16), nt_dims, preferred_element_type=f32)
            p_w = _softmax_rows(s_w, w_mask)
            o_w = jnp.dot(p_w.astype(bf16), vw_scr[slot, g].astype(bf16), preferred_element_type=f32)
            gs = 1.0 / (1.0 + jnp.exp(-gl[g]))
            o = o_c[g] * gs[:, 0:1] + o_s * gs[:, 1:2] + o_w * gs[:, 2:3]
            outs.extend([o[h * SQ_PAD:(h + 1) * SQ_PAD] for h in range(A_HPG)])
        o_ref[0] = jnp.concatenate(outs, axis=1)

    compute()


def nsa_sample_pallas(q, ksn, vsn, kwn, vwn, gate_logits, pool_kc, pool_vc, pool_ks, pool_vs,
                      buf_kw, buf_vw, page_table, cmp_k, cmp_v, g_kc):
    f32, bf16 = jnp.float32, jnp.bfloat16
    B, S = q.shape[:2]
    assert S <= SQ_PAD and S < L_CMP
    n_pages = page_table.shape[1]
    past = n_pages * PAGE_SIZE
    wb = buf_kw.shape[1]
    kvw = A_KV * A_DH
    pad_q = lambda a: jnp.pad(a, ((0, 0), (0, SQ_PAD - S)) + ((0, 0),) * (a.ndim - 2))
    qs = pad_q((q.astype(f32) * A_DH ** -0.5).astype(bf16)).reshape(B, SQ_PAD, A_KV, A_HPG, A_DH)
    qs = qs.transpose(0, 2, 3, 1, 4).reshape(B, A_KV, A_HPG * SQ_PAD, A_DH)
    gl = pad_q(gate_logits).reshape(B, SQ_PAD, A_KV, A_HPG, 3).transpose(0, 2, 3, 1, 4)
    gl = gl.reshape(B, A_KV, A_HPG * SQ_PAD, 3)
    new = lambda a: pad_q(a.reshape(B, S, kvw))
    pe_k, w1_k, b1_k, w2_k, b2_k = cmp_k
    pe_v, w1_v, b1_v, w2_v, b2_v = cmp_v
    row = lambda a: a.reshape(1, -1).astype(f32)
    n_keys = past + SEL_TAIL
    lane = jnp.arange(P_NKEYS, dtype=jnp.int32)[:, None]
    key_blk = (jnp.arange(n_keys, dtype=jnp.int32) // L_SEL)[None, :]
    expand = ((lane % 2 == 0) & (lane // 2 == key_blk)).astype(bf16)
    chunk_rows = CMP_CHUNK * L_CMP
    dst = jnp.arange(chunk_rows, dtype=jnp.int32)
    src_row = (dst % CMP_CHUNK) * L_CMP + dst // CMP_CHUNK
    perm = (src_row[:, None] == jnp.arange(chunk_rows, dtype=jnp.int32)[None, :]).astype(bf16)
    tile_pe = lambda pe: jnp.tile(pe.astype(f32), (CMP_CHUNK, 1))

    hbm = pl.BlockSpec(memory_space=pl.ANY)
    seq3 = lambda r, c: pl.BlockSpec((1, r, c), lambda b, pt: (b, 0, 0))
    seq4 = lambda a, r, c: pl.BlockSpec((1, a, r, c), lambda b, pt: (b, 0, 0, 0))
    full = lambda a: pl.BlockSpec(a.shape, lambda b, pt: (0,) * a.ndim, pipeline_mode=pl.Buffered(1))
    consts = [tile_pe(pe_k), tile_pe(pe_v), perm, w1_k.astype(bf16), row(b1_k), w2_k.astype(bf16), row(b2_k),
              w1_v.astype(bf16), row(b1_v), w2_v.astype(bf16), row(b2_v), row(g_kc), expand]
    slots = 2
    out = pl.pallas_call(
        functools.partial(_nsa_sample_kernel, past=past, n_pages=n_pages, n_new=S),
        grid_spec=pltpu.PrefetchScalarGridSpec(
            num_scalar_prefetch=1,
            grid=(B,),
            in_specs=[hbm] * 6
                     + [seq4(A_KV, A_HPG * SQ_PAD, A_DH), seq4(A_KV, A_HPG * SQ_PAD, 3)]
                     + [seq3(SQ_PAD, kvw)] * 4
                     + [full(c) for c in consts],
            out_specs=pl.BlockSpec((1, SQ_PAD, A_W), lambda b, pt: (b, 0, 0)),
            scratch_shapes=[pltpu.VMEM((slots, A_KV, past, A_DH), f32), pltpu.VMEM((slots, A_KV, past, A_DH), f32),
                            pltpu.VMEM((slots, A_KV, n_keys, A_DH), f32), pltpu.VMEM((slots, A_KV, n_keys, A_DH), f32),
                            pltpu.VMEM((slots, A_KV, wb + SEL_TAIL, A_DH), f32),
                            pltpu.VMEM((slots, A_KV, wb + SEL_TAIL, A_DH), f32),
                            pltpu.SemaphoreType.DMA((slots,))]),
        out_shape=jax.ShapeDtypeStruct((B, SQ_PAD, A_W), f32),
        compiler_params=pltpu.CompilerParams(
            dimension_semantics=("arbitrary",), vmem_limit_bytes=VMEM_LIMIT_BYTES),
        name="nsa_sample",
    )(page_table, pool_kc, pool_vc, pool_ks, pool_vs, buf_kw, buf_vw, qs, gl,
      new(ksn), new(vsn), new(kwn), new(vwn), *consts)
    return out[:, :S]


def merge_groups(ret_o, rg, nsa_o, r_gn, w_out):
    B, T = rg.shape[:2]
    r = rmsnorm(ret_o, r_gn).reshape(B, T, R_W) * jax.nn.silu(rg.astype(jnp.float32))
    mix = jnp.concatenate([r.astype(rg.dtype), nsa_o.reshape(B, T, A_W).astype(rg.dtype)], axis=-1)
    return mmnd(mix, w_out)


def memory_kv(mem, g_memtok, w_mk, w_mv, m_kn):
    B, M, _ = mem.shape
    m = rmsnorm(mem, g_memtok)
    k = rmsnorm(mmnd(m, w_mk).reshape(B, M, M_HEADS, M_DH), m_kn)
    v = mmnd(m, w_mv).reshape(B, M, M_HEADS, M_DH)
    return k, v


def memory_attend(h, mk, mv, w_mq, m_qn, w_mo):
    B, T, _ = h.shape
    q = rmsnorm(mmnd(h, w_mq).reshape(B, T, M_HEADS, M_DH), m_qn)
    s = jnp.einsum('bthd,bmhd->bhtm', q.astype(jnp.float32), mk.astype(jnp.float32)) * M_DH ** -0.5
    p = jax.nn.softmax(s, axis=-1)
    o = jnp.einsum('bhtm,bmhd->bthd', p, mv.astype(jnp.float32)).astype(h.dtype)
    return mmnd(o.reshape(B, T, M_W), w_mo)


def kernel(x_prompt, x_sample, mem_prompt, cache_k_cmp, cache_v_cmp, cache_k_sel, cache_v_sel, state_k_win, state_v_win, state_ret, cache_mem_k, cache_mem_v, page_table, g_mix, w_in, r_gn, a_qn, a_kcn, a_ksn, a_kwn, cmp_pe_k, cmp_w1_k, cmp_b1_k, cmp_w2_k, cmp_b2_k, cmp_pe_v, cmp_w1_v, cmp_b1_v, cmp_w2_v, cmp_b2_v, w_out, g_xattn, g_memtok, w_mq, w_mk, w_mv, m_qn, m_kn, w_mo, g_ffn, w_pq, sub_k1, sub_k2, exp_u, exp_v):
    cmp_k = (cmp_pe_k, cmp_w1_k, cmp_b1_k, cmp_w2_k, cmp_b2_k)
    cmp_v = (cmp_pe_v, cmp_w1_v, cmp_b1_v, cmp_w2_v, cmp_b2_v)
    xp, xs = x_prompt, x_sample

    T = xp.shape[1]
    w_in = w_in.astype(jnp.bfloat16)
    w_out = w_out.astype(jnp.bfloat16)
    assert xp.shape[0] == 1
    proj_p = mm(rmsnorm(xp, g_mix).astype(jnp.bfloat16)[0], w_in)
    ret_mix, ret_p = retention_prompt_pallas(proj_p, r_gn)
    ret_p = ret_p[None]
    cuts = np.cumsum(IN_SIZES)[3:-1].tolist()
    aq, akc, avc, aks, avs, akw, avw, ag_p = jnp.split(proj_p[None, :, cuts[0]:], [c - cuts[0] for c in cuts[1:]], axis=-1)
    kv = lambda a: a.reshape(1, T, A_KV, A_DH)
    aq, akc, avc, aks, avs, akw, avw = aq.reshape(1, T, A_HEADS, A_DH), kv(akc), kv(avc), kv(aks), kv(avs), kv(akw), kv(avw)
    aq, aks, akw = rmsnorm(aq, a_qn), rmsnorm(aks, a_ksn), rmsnorm(akw, a_kwn)
    kc = rmsnorm(compress(akc, *cmp_k), a_kcn)
    vc = compress(avc, *cmp_v)
    nsa_o = nsa_prompt_pallas(aq[0], kc[0], vc[0], aks[0], avs[0], akw[0], avw[0], ag_p[0])
    hp = xp + mm(jnp.concatenate([ret_mix, nsa_o], axis=-1), w_out)[None]
    mem_k_p, mem_v_p = memory_kv(mem_prompt, g_memtok, w_mk, w_mv, m_kn)
    hp = hp + memory_attend(rmsnorm(hp, g_xattn), mem_k_p, mem_v_p, w_mq, m_qn, w_mo)
    wl = min(WINDOW, T)
    k_cmp_p, v_cmp_p, k_sel_p, v_sel_p = akc, avc, aks, avs
    k_win_p, v_win_p = akw[:, T - wl:], avw[:, T - wl:]
    ret_p = ret_p.astype(xp.dtype)

    S = xs.shape[1]
    past = page_table.shape[1] * PAGE_SIZE
    rq, rk, rv, rg, aq, akc, avc, aks, avs, akw, avw, ag, ag_s = split_mix(
        rmsnorm(xs, g_mix).astype(jnp.bfloat16), w_in)
    pos = past + jnp.arange(S)
    ret_s, ret_o = retention_chunk(state_ret.astype(jnp.float32), rotary(rq, pos) * R_DK ** -0.5,
                                   rotary(rk, pos), rv)
    aq, aks, akw = rmsnorm(aq, a_qn), rmsnorm(aks, a_ksn), rmsnorm(akw, a_kwn)
    nsa_o = nsa_sample_pallas(aq, aks, avs, akw, avw, ag_s, cache_k_cmp, cache_v_cmp, cache_k_sel,
                              cache_v_sel, state_k_win, state_v_win, page_table, cmp_k, cmp_v, a_kcn)
    hs = xs + merge_groups(ret_o, rg, nsa_o, r_gn, w_out)
    hs = hs + memory_attend(rmsnorm(hs, g_xattn), cache_mem_k, cache_mem_v, w_mq, m_qn, w_mo)
    n_p = hp.shape[0] * hp.shape[1]
    tokens = jnp.concatenate([rmsnorm(hp, g_ffn).reshape(-1, D_MODEL), rmsnorm(hs, g_ffn).reshape(-1, D_MODEL)], axis=0)
    exp_vt = jnp.swapaxes(exp_v.reshape(-1, PEER_TE, D_MODEL), 1, 2).astype(jnp.bfloat16)
    y = peer_pallas(tokens, w_pq, sub_k1, sub_k2, exp_u.astype(jnp.bfloat16), exp_vt)
    xp = hp + y[:n_p].reshape(hp.shape)
    xs = hs + y[n_p:].reshape(hs.shape)
    wb = state_k_win.shape[1]
    k_cmp_s, v_cmp_s, k_sel_s, v_sel_s = akc, avc, aks, avs
    k_win_s = jnp.concatenate([state_k_win, akw.astype(state_k_win.dtype)], axis=1)[:, -wb:]
    v_win_s = jnp.concatenate([state_v_win, avw.astype(state_v_win.dtype)], axis=1)[:, -wb:]
    ret_s = ret_s.astype(state_ret.dtype)
    return (xp, xs, k_cmp_p, v_cmp_p, k_sel_p, v_sel_p, k_win_p, v_win_p, ret_p, mem_k_p, mem_v_p,
            k_cmp_s, v_cmp_s, k_sel_s, v_sel_s, k_win_s, v_win_s, ret_s)
```

```python
import functools

import jax
import jax.numpy as jnp
import numpy as np
from jax import lax
from jax.experimental import pallas as pl
from jax.experimental.pallas import tpu as pltpu

D_MODEL = 4096
PAGE_SIZE = 128
R_HEADS = 8
R_DK = 256
R_DV = 256
R_CHUNK = 128
ROPE_BASE = 10000.0
A_HEADS = 16
A_KV = 4
A_DH = 128
A_HPG = A_HEADS // A_KV
L_CMP = 32
CMP_HID = 256
L_SEL = 64
N_SEL = 16
WINDOW = 512
Q_BLOCK = 128
FORCE_SCORE = 1e4
NEG = -1e30
M_HEADS = 4
M_DH = 128
M_W = M_HEADS * M_DH
P_HEADS = 8
P_NKEYS = 128
P_DQ = 256
P_TOPK = 16
P_BLOCK = 128
EPS = 1e-6
R_W = R_HEADS * R_DV
A_W = A_HEADS * A_DH
IN_SIZES = (R_HEADS * R_DK, R_HEADS * R_DK, R_W, R_W, A_W) + (A_KV * A_DH,) * 6 + (A_HEADS * 3,)

VMEM_LIMIT_BYTES = 56 * 1024 * 1024


def _mm_kernel(a_ref, b_ref, o_ref, acc_ref):
    k = pl.program_id(2)

    @pl.when(k == 0)
    def _():
        acc_ref[...] = jnp.zeros_like(acc_ref)

    acc_ref[...] += jnp.dot(a_ref[...].astype(jnp.bfloat16), b_ref[...].astype(jnp.bfloat16),
                            preferred_element_type=jnp.float32)

    @pl.when(k == pl.num_programs(2) - 1)
    def _():
        o_ref[...] = acc_ref[...]


def _mm_fullk_kernel(a_ref, b_ref, o_ref):
    o_ref[...] = jnp.dot(a_ref[...].astype(jnp.bfloat16), b_ref[...].astype(jnp.bfloat16),
                         preferred_element_type=jnp.float32)


MM_FULLK_TM = 1024
MM_FULLK_TN = 512


def mm(a, b, tm=1024, tn=1024, tk=1024):
    m, kdim = a.shape
    _, n = b.shape
    if (a.dtype == jnp.bfloat16 and b.dtype == jnp.bfloat16 and m % MM_FULLK_TM == 0
            and n >= MM_FULLK_TN and kdim <= 4096):
        return pl.pallas_call(
            _mm_fullk_kernel,
            grid=(m // MM_FULLK_TM, pl.cdiv(n, MM_FULLK_TN)),
            in_specs=[pl.BlockSpec((MM_FULLK_TM, kdim), lambda i, j: (i, 0)),
                      pl.BlockSpec((kdim, MM_FULLK_TN), lambda i, j: (0, j))],
            out_specs=pl.BlockSpec((MM_FULLK_TM, MM_FULLK_TN), lambda i, j: (i, j)),
            out_shape=jax.ShapeDtypeStruct((m, n), jnp.float32),
            compiler_params=pltpu.CompilerParams(
                dimension_semantics=("parallel", "parallel"), vmem_limit_bytes=VMEM_LIMIT_BYTES),
            name="mm_fullk",
        )(a, b)
    tm = min(tm, m)
    tn = min(tn, n)
    tk = min(tk, kdim)
    assert m % tm == 0 and kdim % tk == 0
    grid = (m // tm, pl.cdiv(n, tn), kdim // tk)
    return pl.pallas_call(
        _mm_kernel,
        grid=grid,
        in_specs=[pl.BlockSpec((tm, tk), lambda i, j, k: (i, k)),
                  pl.BlockSpec((tk, tn), lambda i, j, k: (k, j))],
        out_specs=pl.BlockSpec((tm, tn), lambda i, j, k: (i, j)),
        out_shape=jax.ShapeDtypeStruct((m, n), jnp.float32),
        scratch_shapes=[pltpu.VMEM((tm, tn), jnp.float32)],
        compiler_params=pltpu.CompilerParams(
            dimension_semantics=("parallel", "parallel", "arbitrary"),
            vmem_limit_bytes=VMEM_LIMIT_BYTES),
        name="mm",
    )(a, b)


def mmnd(a, b):
    lead = a.shape[:-1]
    out = mm(a.reshape(-1, a.shape[-1]), b)
    return out.reshape(lead + (b.shape[-1],))


def _mm_nt_kernel(a_ref, b_ref, o_ref):
    o_ref[...] = lax.dot_general(a_ref[...].astype(jnp.bfloat16), b_ref[...].astype(jnp.bfloat16),
                                 (((1,), (1,)), ((), ())), preferred_element_type=jnp.float32)


def mm_nt(a, b, tm=512, tn=512):
    m, kdim = a.shape
    n, _ = b.shape
    tm, tn = min(tm, m), min(tn, n)
    assert m % tm == 0 and n % tn == 0
    return pl.pallas_call(
        _mm_nt_kernel,
        grid=(m // tm, n // tn),
        in_specs=[pl.BlockSpec((tm, kdim), lambda i, j: (i, 0)),
                  pl.BlockSpec((tn, kdim), lambda i, j: (j, 0))],
        out_specs=pl.BlockSpec((tm, tn), lambda i, j: (i, j)),
        out_shape=jax.ShapeDtypeStruct((m, n), jnp.float32),
        compiler_params=pltpu.CompilerParams(
            dimension_semantics=("parallel", "parallel"),
            vmem_limit_bytes=VMEM_LIMIT_BYTES),
        name="mm_nt",
    )(a, b)


PEER_TR = 256
PEER_TM = 512
PEER_TE = 512
PEER_SLABS = PEER_TE // P_NKEYS
TAKEN = -3.0e38


def _extract_top(s, n):
    rows = s.shape[0]
    row_id = lax.broadcasted_iota(jnp.int32, s.shape, 0).astype(jnp.float32)
    rank = jnp.full(s.shape, float(n), jnp.float32)
    vals = []
    for k in range(n):
        m = jnp.max(s, axis=0, keepdims=True)
        first = jnp.min(jnp.where(s == m, row_id, float(rows)), axis=0, keepdims=True)
        hit = row_id == first
        rank = jnp.where(hit, float(k), rank)
        s = jnp.where(hit, TAKEN, s)
        vals.append(m)
    return rank, jnp.concatenate(vals, axis=0)


def _peer_route_kernel(qt_ref, k1_ref, k2_ref, r2_ref, e2_ref, n_ref, e1n_ref):
    f32, bf16 = jnp.float32, jnp.bfloat16
    half = P_DQ // 2
    qt = qt_ref[...].astype(bf16)
    s1 = jnp.dot(k1_ref[0].astype(bf16), qt[:half], preferred_element_type=f32)
    s2 = jnp.dot(k2_ref[0].astype(bf16), qt[half:], preferred_element_type=f32)
    rank1, v1 = _extract_top(s1, P_TOPK)
    rank2, v2 = _extract_top(s2, P_TOPK)
    r1_id = lax.broadcasted_iota(jnp.int32, v1.shape, 0).astype(f32)
    cnt = jnp.zeros_like(v1)
    for _ in range(P_TOPK):
        nxt = jnp.full_like(v1, TAKEN)
        for j in range(P_TOPK):
            nxt = jnp.where(cnt == float(j), v2[j:j + 1], nxt)
        front = jnp.where(cnt < float(P_TOPK), v1 + nxt, TAKEN)
        m = jnp.max(front, axis=0, keepdims=True)
        first = jnp.min(jnp.where(front == m, r1_id, float(P_TOPK)), axis=0, keepdims=True)
        cnt = cnt + jnp.where(r1_id == first, 1.0, 0.0)
    ev1 = jnp.exp(v1 - v1[0:1])
    ev2 = jnp.exp(v2 - v2[0:1])
    inner = jnp.zeros_like(v1)
    for j in range(P_TOPK):
        inner = inner + jnp.where(cnt > float(j), ev2[j:j + 1], 0.0)
    z = jnp.sum(ev1 * inner, axis=0, keepdims=True)
    n_dense = jnp.zeros_like(s1)
    for r in range(P_TOPK):
        n_dense = n_dense + jnp.where(rank1 == float(r), cnt[r:r + 1], 0.0)
    r2_ref[0] = rank2
    e2_ref[0] = jnp.exp(s2 - v2[0:1])
    n_ref[0] = n_dense
    e1n_ref[0] = jnp.exp(s1 - v1[0:1]) * (1.0 / z)


def _peer_mix_kernel(x_ref, u_ref, v_ref, r2_ref, e2_ref, n_ref, e1n_ref, y_ref, a_scr):
    f32, bf16 = jnp.float32, jnp.bfloat16
    j = pl.program_id(1)

    @pl.when(j == 0)
    def _():
        y_ref[...] = jnp.zeros_like(y_ref)
        a_scr[1] = jnp.zeros(a_scr.shape[1:], f32)

    act = jax.nn.gelu(a_scr[(j + 1) % 2])
    a_scr[j % 2] = lax.dot_general(u_ref[...], x_ref[...], (((1,), (1,)), ((), ())),
                                   preferred_element_type=f32)

    tile = jnp.maximum(j - 1, 0)
    parts = []
    for s in range(PEER_SLABS):
        first_key = tile * PEER_SLABS + s
        g = jnp.zeros((P_NKEYS, act.shape[1]), f32)
        for h in range(P_HEADS):
            n_row = n_ref[h, pl.ds(first_key, 1), :]
            w_row = e1n_ref[h, pl.ds(first_key, 1), :]
            g = g + jnp.where(r2_ref[h] < n_row, e2_ref[h] * w_row, 0.0)
        parts.append((g * act[s * P_NKEYS:(s + 1) * P_NKEYS]).astype(bf16))
    w_t = jnp.concatenate(parts, axis=0)
    y_ref[...] += lax.dot_general(w_t, v_ref[...], (((0,), (0,)), ((), ())),
                                  preferred_element_type=f32)


def peer_pallas(x, w_pq, sub_k1, sub_k2, exp_u_b, exp_v_b):
    bf16 = jnp.bfloat16
    n, d = x.shape
    n_exp = exp_u_b.shape[0]
    assert n % PEER_TM == 0 and n % PEER_TR == 0 and n_exp % PEER_TE == 0
    xb = x.astype(bf16)
    qt = mm_nt(jnp.swapaxes(w_pq, 0, 1), xb)
    route_shape = jax.ShapeDtypeStruct((P_HEADS, P_NKEYS, n), jnp.float32)
    key_spec = pl.BlockSpec((1, P_NKEYS, P_DQ // 2), lambda i, h: (h, 0, 0))
    route_out = pl.BlockSpec((1, P_NKEYS, PEER_TR), lambda i, h: (h, 0, i))
    r2, e2, nd, e1n = pl.pallas_call(
        _peer_route_kernel,
        grid=(n // PEER_TR, P_HEADS),
        in_specs=[pl.BlockSpec((P_DQ, PEER_TR), lambda i, h: (h, i)), key_spec, key_spec],
        out_specs=[route_out] * 4,
        out_shape=[route_shape] * 4,
        compiler_params=pltpu.CompilerParams(
            dimension_semantics=("parallel", "parallel"), vmem_limit_bytes=VMEM_LIMIT_BYTES),
        name="peer_route",
    )(qt, sub_k1, sub_k2)
    route_in = pl.BlockSpec((P_HEADS, P_NKEYS, PEER_TM), lambda i, j: (0, 0, i),
                            pipeline_mode=pl.Buffered(1))
    n_tiles = n_exp // PEER_TE
    return pl.pallas_call(
        _peer_mix_kernel,
        grid=(n // PEER_TM, n_tiles + 1),
        in_specs=[pl.BlockSpec((PEER_TM, d), lambda i, j: (i, 0), pipeline_mode=pl.Buffered(1)),
                  pl.BlockSpec((PEER_TE, d), lambda i, j: (jnp.minimum(j, n_tiles - 1), 0)),
                  pl.BlockSpec((PEER_TE, d), lambda i, j: (jnp.maximum(j - 1, 0), 0)),
                  route_in, route_in, route_in, route_in],
        out_specs=pl.BlockSpec((PEER_TM, d), lambda i, j: (i, 0)),
        out_shape=jax.ShapeDtypeStruct((n, d), jnp.float32),
        scratch_shapes=[pltpu.VMEM((2, PEER_TE, PEER_TM), jnp.float32)],
        compiler_params=pltpu.CompilerParams(
            dimension_semantics=("parallel", "arbitrary"), vmem_limit_bytes=VMEM_LIMIT_BYTES),
        name="peer_mix",
    )(xb, exp_u_b, exp_v_b, r2, e2, nd, e1n)


def _retention_prompt_kernel(q_ref, k_ref, v_ref, gate_ref, cos_ref, sin_ref, dmask_ref, dq_ref, dk_ref,
                             cdec_ref, gn_ref, o_ref, state_ref, s_scr):
    f32, bf16 = jnp.float32, jnp.bfloat16
    c = pl.program_id(1)

    @pl.when(c == 0)
    def _():
        s_scr[...] = jnp.zeros_like(s_scr)

    cos, sin = cos_ref[...], sin_ref[...]
    half = R_DK // 2

    def rot(x):
        x1, x2 = x[:, :half], x[:, half:]
        return jnp.concatenate([x1 * cos - x2 * sin, x1 * sin + x2 * cos], axis=1)

    q = (rot(q_ref[...]) * R_DK ** -0.5).astype(bf16)
    k = rot(k_ref[...])
    v = v_ref[...].astype(bf16)
    s_old = s_scr[...]
    inner = lax.dot_general(q, k.astype(bf16), (((1,), (1,)), ((), ())),
                            preferred_element_type=f32) * dmask_ref[0]
    o = jnp.dot(inner.astype(bf16), v, preferred_element_type=f32)
    o = o + jnp.dot(q, s_old.astype(bf16), preferred_element_type=f32) * dq_ref[0]
    kdec = (k * dk_ref[0]).astype(bf16)
    s_new = s_old * cdec_ref[0] + lax.dot_general(kdec, v, (((0,), (0,)), ((), ())),
                                                  preferred_element_type=f32)
    s_scr[...] = s_new
    y = o * lax.rsqrt(jnp.mean(o * o, axis=-1, keepdims=True) + EPS) * gn_ref[0]
    gate = gate_ref[...]
    o_ref[...] = y * (gate * (1.0 / (1.0 + jnp.exp(-gate))))

    @pl.when(c == pl.num_programs(1) - 1)
    def _():
        state_ref[0] = s_new


def retention_prompt_pallas(proj, r_gn):
    f32 = jnp.float32
    T = proj.shape[0]
    assert T % R_CHUNK == 0 and R_DK == R_DV
    half = R_DK // 2
    inv = ROPE_BASE ** (-jnp.arange(half, dtype=f32) / half)
    ang = jnp.arange(T, dtype=f32)[:, None] * inv[None, :]
    lg = jnp.log1p(-jnp.exp2(-5.0 - jnp.arange(R_HEADS, dtype=f32)))
    idx = jnp.arange(R_CHUNK, dtype=f32)
    rel = idx[:, None] - idx[None, :]
    dmask = jnp.where(rel[None] >= 0, jnp.exp(jnp.maximum(rel, 0.0)[None] * lg[:, None, None]), 0.0)
    dq = jnp.exp((idx + 1.0)[None, :] * lg[:, None])[:, :, None]
    dk = jnp.exp((R_CHUNK - 1.0 - idx)[None, :] * lg[:, None])[:, :, None]
    cdec = jnp.broadcast_to(jnp.exp(R_CHUNK * lg)[:, None, None], (R_HEADS, 1, R_DV))
    col = lambda base: pl.BlockSpec((R_CHUNK, R_DK), lambda h, c, base=base: (c, base + h))
    per_head = lambda shape: pl.BlockSpec((1,) + shape, lambda h, c: (h, 0, 0))
    trig = pl.BlockSpec((R_CHUNK, half), lambda h, c: (c, 0))
    return pl.pallas_call(
        _retention_prompt_kernel,
        grid=(R_HEADS, T // R_CHUNK),
        in_specs=[col(0), col(R_HEADS), col(2 * R_HEADS), col(3 * R_HEADS), trig, trig,
                  per_head((R_CHUNK, R_CHUNK)), per_head((R_CHUNK, 1)), per_head((R_CHUNK, 1)),
                  per_head((1, R_DV)), per_head((1, R_DV))],
        out_specs=[pl.BlockSpec((R_CHUNK, R_DV), lambda h, c: (c, h)),
                   pl.BlockSpec((1, R_DK, R_DV), lambda h, c: (h, 0, 0))],
        out_shape=[jax.ShapeDtypeStruct((T, R_W), f32),
                   jax.ShapeDtypeStruct((R_HEADS, R_DK, R_DV), f32)],
        scratch_shapes=[pltpu.VMEM((R_DK, R_DV), f32)],
        compiler_params=pltpu.CompilerParams(
            dimension_semantics=("parallel", "arbitrary"), vmem_limit_bytes=VMEM_LIMIT_BYTES),
        name="retention_prompt",
    )(proj, proj, proj, proj, jnp.cos(ang), jnp.sin(ang), dmask, dq, dk, cdec,
      r_gn.astype(f32).reshape(R_HEADS, 1, R_DV))


def rmsnorm(x, g):
    xf = x.astype(jnp.float32)
    y = xf * lax.rsqrt(jnp.mean(xf * xf, axis=-1, keepdims=True) + EPS)
    return (y * g.astype(jnp.float32)).astype(x.dtype)


def rotary(x, pos):
    half = x.shape[-1] // 2
    inv = ROPE_BASE ** (-jnp.arange(half, dtype=jnp.float32) / half)
    ang = pos.astype(jnp.float32)[:, None] * inv[None, :]
    cos, sin = jnp.cos(ang)[:, None, :], jnp.sin(ang)[:, None, :]
    xf = x.astype(jnp.float32)
    x1, x2 = xf[..., :half], xf[..., half:]
    return jnp.concatenate([x1 * cos - x2 * sin, x1 * sin + x2 * cos], axis=-1).astype(x.dtype)


def masked_softmax(s, mask):
    p = jax.nn.softmax(jnp.where(mask, s, NEG), axis=-1)
    return jnp.where(mask, p, 0.0)


def split_mix(h, w_in):
    B, T, _ = h.shape
    cuts = np.cumsum(IN_SIZES)[:-1].tolist()
    rq, rk, rv, rg, aq, akc, avc, aks, avs, akw, avw, ag = jnp.split(mmnd(h, w_in), cuts, axis=-1)
    r = lambda a, n, d: a.reshape(B, T, n, d)
    return (r(rq, R_HEADS, R_DK), r(rk, R_HEADS, R_DK), r(rv, R_HEADS, R_DV), rg,
            r(aq, A_HEADS, A_DH), r(akc, A_KV, A_DH), r(avc, A_KV, A_DH), r(aks, A_KV, A_DH),
            r(avs, A_KV, A_DH), r(akw, A_KV, A_DH), r(avw, A_KV, A_DH),
            jax.nn.sigmoid(ag.astype(jnp.float32)).reshape(B, T, A_HEADS, 3), ag)


def retention_chunk(S, q, k, v):
    q, k, v = q.astype(jnp.float32), k.astype(jnp.float32), v.astype(jnp.float32)
    C = q.shape[1]
    lg = jnp.log1p(-jnp.exp2(-5.0 - jnp.arange(R_HEADS, dtype=jnp.float32)))
    idx = jnp.arange(C, dtype=jnp.float32)
    rel = idx[:, None] - idx[None, :]
    dmask = jnp.where(rel[None] >= 0, jnp.exp(jnp.maximum(rel, 0.0)[None] * lg[:, None, None]), 0.0)
    inner = jnp.einsum('bihd,bjhd->bhij', q, k) * dmask[None]
    o = jnp.einsum('bhij,bjhe->bihe', inner, v)
    o = o + jnp.einsum('bihd,bhde->bihe', q, S) * jnp.exp((idx + 1.0)[:, None] * lg[None, :])[None, :, :, None]
    kdec = k * jnp.exp((C - 1.0 - idx)[:, None] * lg[None, :])[None, :, :, None]
    S_new = S * jnp.exp(C * lg)[None, :, None, None] + jnp.einsum('bjhd,bjhe->bhde', kdec, v)
    return S_new, o


def compress(rows, pe, w1, b1, w2, b2):
    T = rows.shape[-3]
    nc = T // L_CMP
    lead = rows.shape[:-3]
    blk = rows[..., :nc * L_CMP, :, :].reshape(lead + (nc, L_CMP, A_KV, A_DH)) + pe[:, None, :]
    blk = jnp.swapaxes(blk, -3, -2).reshape(lead + (nc, A_KV, L_CMP * A_DH))
    return mmnd(jax.nn.gelu(mmnd(blk, w1) + b1), w2) + b2


NSA_TK = 512
N_TOPK_NEG = -3.0e38


def _softmax_rows(s, mask):
    s = jnp.where(mask, s, NEG)
    m = jnp.max(s, axis=-1, keepdims=True)
    e = jnp.where(mask, jnp.exp(s - m), 0.0)
    l = jnp.sum(e, axis=-1, keepdims=True)
    return e * (1.0 / jnp.maximum(l, 1e-30))


def _top_n_mask_cols(score_t, n):
    rows = score_t.shape[0]
    row_id = lax.broadcasted_iota(jnp.int32, score_t.shape, 0).astype(jnp.float32)

    def one(_, c):
        sc, sel = c
        m = jnp.max(sc, axis=0, keepdims=True)
        first = jnp.min(jnp.where(sc == m, row_id, float(rows)), axis=0, keepdims=True)
        hit = row_id == first
        return jnp.where(hit, N_TOPK_NEG, sc), jnp.where(hit, 1.0, sel)

    _, sel = lax.fori_loop(0, n, one, (score_t, jnp.zeros_like(score_t)), unroll=True)
    return sel


def _nsa_prompt_kernel(q_ref, gate_ref, kc_ref, vc_ref, ks_ref, vs_ref, kw_ref, vw_ref, e_ref, o_ref,
                       *, n_win_keys):
    f32, bf16 = jnp.float32, jnp.bfloat16
    qb = pl.program_id(1)
    start = qb * Q_BLOCK
    nt_dims = (((1,), (1,)), ((), ()))
    q = q_ref[...]
    qg = jnp.concatenate([q[:, h * A_DH:(h + 1) * A_DH] for h in range(A_HPG)], axis=0)
    q_pos = start + lax.broadcasted_iota(jnp.int32, (Q_BLOCK, 1), 0)
    rep = lambda a: jnp.concatenate([a] * A_HPG, axis=0)

    kc, vc = kc_ref[0], vc_ref[0]
    nc = kc.shape[0]
    nsb = nc // 2
    s_c = lax.dot_general(qg, kc, nt_dims, preferred_element_type=f32)
    lane_c = lax.broadcasted_iota(jnp.int32, (1, nc), 1)
    c_orig = jnp.where(lane_c < nsb, 2 * lane_c, 2 * (lane_c - nsb) + 1)
    cmask = ((c_orig + 1) * L_CMP - 1) <= q_pos
    p_c = _softmax_rows(s_c, rep(cmask))
    o_c = jnp.dot(p_c.astype(bf16), vc, preferred_element_type=f32)

    ph = p_c[0:Q_BLOCK]
    for h in range(1, A_HPG):
        ph = ph + p_c[h * Q_BLOCK:(h + 1) * Q_BLOCK]
    imp = ph[:, :nsb] + ph[:, nsb:]
    blk = lax.broadcasted_iota(jnp.int32, (1, nsb), 1)
    cur = q_pos // L_SEL
    forced = (blk == 0) | (blk == cur) | (blk == cur - 1)
    imp = jnp.where(blk <= cur, jnp.where(forced, FORCE_SCORE, imp), NEG)
    sel = _top_n_mask_cols(imp.T, min(N_SEL, nsb)).T
    sel_b = sel.astype(bf16)

    n_tiles = (start + Q_BLOCK + NSA_TK - 1) // NSA_TK

    def sel_step(j, carry):
        m, l, acc = carry
        off = pl.multiple_of(j * NSA_TK, NSA_TK)
        k = ks_ref[0, pl.ds(off, NSA_TK), :]
        v = vs_ref[0, pl.ds(off, NSA_TK), :]
        s = lax.dot_general(qg, k, nt_dims, preferred_element_type=f32)
        picked = jnp.dot(sel_b, e_ref[j], preferred_element_type=f32)
        k_pos = off + lax.broadcasted_iota(jnp.int32, (1, NSA_TK), 1)
        bias = jnp.where((picked > 0.5) & (k_pos <= q_pos), 0.0, NEG)
        s = s + rep(bias)
        m_new = jnp.maximum(m, jnp.max(s, axis=-1, keepdims=True))
        alpha = jnp.exp(m - m_new)
        p = jnp.exp(s - m_new)
        l = alpha * l + jnp.sum(p, axis=-1, keepdims=True)
        acc = alpha * acc + jnp.dot(p.astype(bf16), v, preferred_element_type=f32)
        return m_new, l, acc

    rows = A_HPG * Q_BLOCK
    m_s, l_s, acc_s = lax.fori_loop(
        0, n_tiles, sel_step,
        (jnp.full((rows, 1), NEG, f32), jnp.zeros((rows, 1), f32), jnp.zeros((rows, A_DH), f32)))
    o_s = acc_s * (1.0 / l_s)

    base = pl.multiple_of(jnp.maximum(start + Q_BLOCK - n_win_keys, 0), Q_BLOCK)
    kw = kw_ref[0, pl.ds(base, n_win_keys), :]
    vw = vw_ref[0, pl.ds(base, n_win_keys), :]
    s_w = lax.dot_general(qg, kw, nt_dims, preferred_element_type=f32)
    rel = q_pos - (base + lax.broadcasted_iota(jnp.int32, (1, n_win_keys), 1))
    p_w = _softmax_rows(s_w, rep((rel >= 0) & (rel < WINDOW)))
    o_w = jnp.dot(p_w.astype(bf16), vw, preferred_element_type=f32)

    gl = gate_ref[0]
    g = 1.0 / (1.0 + jnp.exp(-gl))
    outs = []
    for h in range(A_HPG):
        r = slice(h * Q_BLOCK, (h + 1) * Q_BLOCK)
        outs.append(o_c[r] * g[:, 3 * h:3 * h + 1] + o_s[r] * g[:, 3 * h + 1:3 * h + 2]
                    + o_w[r] * g[:, 3 * h + 2:3 * h + 3])
    o_ref[...] = jnp.concatenate(outs, axis=-1)


def nsa_prompt_pallas(q, kc, vc, ks, vs, kw, vw, gate_logits):
    bf16 = jnp.bfloat16
    T = q.shape[0]
    nc = kc.shape[0]
    nsb = T // L_SEL
    assert T % NSA_TK == 0 and nc == 2 * nsb
    n_win_keys = min(WINDOW + Q_BLOCK, T)
    qs = (q.astype(jnp.float32) * A_DH ** -0.5).astype(bf16).reshape(T, A_W)
    grp = lambda a: jnp.swapaxes(a, 0, 1).astype(bf16)
    perm = lambda a: jnp.concatenate([a[0::2], a[1::2]], axis=0)
    gl = jnp.swapaxes(gate_logits.reshape(T, A_KV, A_HPG * 3), 0, 1)
    n_t = T // NSA_TK
    key_blk = (jnp.arange(T, dtype=jnp.int32) // L_SEL).reshape(n_t, 1, NSA_TK)
    expand = (jnp.arange(nsb, dtype=jnp.int32)[None, :, None] == key_blk).astype(bf16)
    kv_spec = lambda rows: pl.BlockSpec((1, rows, A_DH), lambda g, i: (g, 0, 0))
    return pl.pallas_call(
        functools.partial(_nsa_prompt_kernel, n_win_keys=n_win_keys),
        grid=(A_KV, T // Q_BLOCK),
        in_specs=[pl.BlockSpec((Q_BLOCK, A_HPG * A_DH), lambda g, i: (i, g)),
                  pl.BlockSpec((1, Q_BLOCK, A_HPG * 3), lambda g, i: (g, i, 0)),
                  kv_spec(nc), kv_spec(nc), kv_spec(T), kv_spec(T), kv_spec(T), kv_spec(T),
                  pl.BlockSpec((n_t, nsb, NSA_TK), lambda g, i: (0, 0, 0))],
        out_specs=pl.BlockSpec((Q_BLOCK, A_HPG * A_DH), lambda g, i: (i, g)),
        out_shape=jax.ShapeDtypeStruct((T, A_W), jnp.float32),
        compiler_params=pltpu.CompilerParams(
            dimension_semantics=("parallel", "arbitrary"),
            vmem_limit_bytes=VMEM_LIMIT_BYTES),
        name="nsa_prompt",
    )(qs, gl, grp(perm(kc)), grp(perm(vc)), grp(ks), grp(vs), grp(kw), grp(vw), expand)


SQ_PAD = 8
SEL_TAIL = 128


def _top_n_mask_rows(score, n):
    cols = score.shape[1]
    col_id = lax.broadcasted_iota(jnp.int32, score.shape, 1).astype(jnp.float32)

    def one(_, c):
        sc, sel = c
        m = jnp.max(sc, axis=1, keepdims=True)
        first = jnp.min(jnp.where(sc == m, col_id, float(cols)), axis=1, keepdims=True)
        hit = col_id == first
        return jnp.where(hit, N_TOPK_NEG, sc), jnp.where(hit, 1.0, sel)

    _, sel = lax.fori_loop(0, n, one, (score, jnp.zeros_like(score)), unroll=True)
    return sel


CMP_CHUNK = 8


def _compress_seq(src, pe_ref, perm_ref, w1_ref, b1_ref, w2_ref, b2_ref, n_blk):
    f32, bf16 = jnp.float32, jnp.bfloat16
    chunk_rows = CMP_CHUNK * L_CMP
    regrouped = []
    for g in range(A_KV):
        for j in range(n_blk // CMP_CHUNK):
            x = (src[g, j * chunk_rows:(j + 1) * chunk_rows, :] + pe_ref[...]).astype(bf16)
            regrouped.append(jnp.dot(perm_ref[...], x, preferred_element_type=f32))
    acc = jnp.zeros((A_KV * n_blk, CMP_HID), f32)
    for l in range(L_CMP):
        xg = jnp.concatenate([y[l * CMP_CHUNK:(l + 1) * CMP_CHUNK] for y in regrouped], axis=0).astype(bf16)
        acc = acc + jnp.dot(xg, w1_ref[l * A_DH:(l + 1) * A_DH, :], preferred_element_type=jnp.float32)
    hid = jax.nn.gelu(acc + b1_ref[...])
    return jnp.dot(hid.astype(bf16), w2_ref[...], preferred_element_type=jnp.float32) + b2_ref[...]


def _nsa_sample_kernel(pt_ref, kcp_ref, vcp_ref, ksp_ref, vsp_ref, kwb_ref, vwb_ref, q_ref, gate_ref,
                       ksn_ref, vsn_ref, kwn_ref, vwn_ref, pek_ref, pev_ref, perm_ref,
                       w1k_ref, b1k_ref, w2k_ref, b2k_ref, w1v_ref, b1v_ref, w2v_ref, b2v_ref, gkc_ref,
                       e_ref, o_ref, kc_scr, vc_scr, ks_scr, vs_scr, kw_scr, vw_scr, sem,
                       *, past, n_pages, n_new):
    f32, bf16 = jnp.float32, jnp.bfloat16
    b = pl.program_id(0)
    slot = b % 2

    def seq_copies(seq, dst_slot):
        cps = []
        for pg in range(n_pages):
            page = pt_ref[seq, pg]
            rows = pl.ds(pg * PAGE_SIZE, PAGE_SIZE)
            for g in range(A_KV):
                for pool_ref, scr in ((kcp_ref, kc_scr), (vcp_ref, vc_scr), (ksp_ref, ks_scr), (vsp_ref, vs_scr)):
                    cps.append(pltpu.make_async_copy(pool_ref.at[page, :, g, :], scr.at[dst_slot, g, rows, :],
                                                     sem.at[dst_slot]))
        for g in range(A_KV):
            for buf_ref, scr in ((kwb_ref, kw_scr), (vwb_ref, vw_scr)):
                cps.append(pltpu.make_async_copy(buf_ref.at[seq, :, g, :], scr.at[dst_slot, g, pl.ds(0, wb), :],
                                                 sem.at[dst_slot]))
        return cps

    wb = kwb_ref.shape[1]

    @pl.when(b == 0)
    def _():
        for cp in seq_copies(0, 0):
            cp.start()

    @pl.when(b + 1 < pl.num_programs(0))
    def _():
        for cp in seq_copies(b + 1, 1 - slot):
            cp.start()

    for cp in seq_copies(b, slot):
        cp.wait()

    def compute():
        nt_dims = (((1,), (1,)), ((), ()))
        n_blk = past // L_CMP
        half = n_blk // 2
        rows = A_HPG * SQ_PAD
        pad_rows = jnp.zeros((SEL_TAIL - SQ_PAD, A_DH), f32)
        tail = lambda new_ref, g: jnp.concatenate(
            [new_ref[0, :, g * A_DH:(g + 1) * A_DH], pad_rows], axis=0)
        for g in range(A_KV):
            ks_scr[slot, g, pl.ds(past, SEL_TAIL), :] = tail(ksn_ref, g)
            vs_scr[slot, g, pl.ds(past, SEL_TAIL), :] = tail(vsn_ref, g)
            kw_scr[slot, g, pl.ds(wb, SEL_TAIL), :] = tail(kwn_ref, g)
            vw_scr[slot, g, pl.ds(wb, SEL_TAIL), :] = tail(vwn_ref, g)
        kcmp = _compress_seq(kc_scr.at[slot], pek_ref, perm_ref, w1k_ref, b1k_ref, w2k_ref, b2k_ref, n_blk)
        kcmp = kcmp * lax.rsqrt(jnp.mean(kcmp * kcmp, axis=-1, keepdims=True) + EPS) * gkc_ref[...]
        vcmp = _compress_seq(vc_scr.at[slot], pev_ref, perm_ref, w1v_ref, b1v_ref, w2v_ref, b2v_ref, n_blk)

        q_pos = past + (lax.broadcasted_iota(jnp.int32, (SQ_PAD, 1), 0))
        rep = lambda a: jnp.concatenate([a] * A_HPG, axis=0)
        lanes = P_NKEYS
        zero_blk = jnp.zeros((lanes - n_blk, A_DH), f32)
        lane_c = lax.broadcasted_iota(jnp.int32, (1, lanes), 1)
        cmask = jnp.broadcast_to(lane_c < n_blk, (rows, lanes))

        o_c, imps = [], []
        for g in range(A_KV):
            qg = q_ref[0, g]
            kc_g = jnp.concatenate([kcmp[g * n_blk:(g + 1) * n_blk], zero_blk], axis=0).astype(bf16)
            vc_g = jnp.concatenate([vcmp[g * n_blk:(g + 1) * n_blk], zero_blk], axis=0).astype(bf16)
            s_c = lax.dot_general(qg, kc_g, nt_dims, preferred_element_type=f32)
            p_c = _softmax_rows(s_c, cmask)
            o_c.append(jnp.dot(p_c.astype(bf16), vc_g, preferred_element_type=f32))
            ph = p_c[0:SQ_PAD]
            for h in range(1, A_HPG):
                ph = ph + p_c[h * SQ_PAD:(h + 1) * SQ_PAD]
            imps.append(ph + pltpu.roll(ph, lanes - 1, axis=1))
        imp = jnp.concatenate(imps, axis=0)
        blk = lane_c // 2
        q_pos4 = jnp.concatenate([q_pos] * A_KV, axis=0)
        cur = q_pos4 // L_SEL
        forced = (blk == 0) | (blk == cur) | (blk == cur - 1)
        imp = jnp.where(blk <= cur, jnp.where(forced, FORCE_SCORE, imp), NEG)
        imp = jnp.where(lane_c % 2 == 0, imp, N_TOPK_NEG)
        nsb = -(-(past + n_new) // L_SEL)
        sel = _top_n_mask_rows(imp, min(N_SEL, nsb))
        picked = jnp.dot(sel.astype(bf16), e_ref[...], preferred_element_type=f32)
        n_keys = past + SEL_TAIL
        k_pos = lax.broadcasted_iota(jnp.int32, (1, n_keys), 1)
        w_pos = past - wb + lax.broadcasted_iota(jnp.int32, (1, wb + SEL_TAIL), 1)
        rel = q_pos - w_pos
        w_mask = rep((rel >= 0) & (rel < WINDOW) & (w_pos >= 0))

        gl = gate_ref[0]
        outs = []
        for g in range(A_KV):
            qg = q_ref[0, g]
            s_mask = rep((picked[g * SQ_PAD:(g + 1) * SQ_PAD] > 0.5) & (k_pos <= q_pos))
            s_s = lax.dot_general(qg, ks_scr[slot, g].astype(bf16), nt_dims, preferred_element_type=f32)
            p_s = _softmax_rows(s_s, s_mask)
            o_s = jnp.dot(p_s.astype(bf16), vs_scr[slot, g].astype(bf16), preferred_element_type=f32)
            s_w = lax.dot_general(qg, kw_scr[slot, g].astype(bf16), nt_dims, preferred_element_type=f32)
            p_w = _softmax_rows(s_w, w_mask)
            o_w = jnp.dot(p_w.astype(bf16), vw_scr[slot, g].astype(bf16), preferred_element_type=f32)
            gs = 1.0 / (1.0 + jnp.exp(-gl[g]))
            o = o_c[g] * gs[:, 0:1] + o_s * gs[:, 1:2] + o_w * gs[:, 2:3]
            outs.extend([o[h * SQ_PAD:(h + 1) * SQ_PAD] for h in range(A_HPG)])
        o_ref[0] = jnp.concatenate(outs, axis=1)

    compute()


def nsa_sample_pallas(q, ksn, vsn, kwn, vwn, gate_logits, pool_kc, pool_vc, pool_ks, pool_vs,
                      buf_kw, buf_vw, page_table, cmp_k, cmp_v, g_kc):
    f32, bf16 = jnp.float32, jnp.bfloat16
    B, S = q.shape[:2]
    assert S <= SQ_PAD and S < L_CMP
    n_pages = page_table.shape[1]
    past = n_pages * PAGE_SIZE
    wb = buf_kw.shape[1]
    kvw = A_KV * A_DH
    pad_q = lambda a: jnp.pad(a, ((0, 0), (0, SQ_PAD - S)) + ((0, 0),) * (a.ndim - 2))
    qs = pad_q((q.astype(f32) * A_DH ** -0.5).astype(bf16)).reshape(B, SQ_PAD, A_KV, A_HPG, A_DH)
    qs = qs.transpose(0, 2, 3, 1, 4).reshape(B, A_KV, A_HPG * SQ_PAD, A_DH)
    gl = pad_q(gate_logits).reshape(B, SQ_PAD, A_KV, A_HPG, 3).transpose(0, 2, 3, 1, 4)
    gl = gl.reshape(B, A_KV, A_HPG * SQ_PAD, 3)
    new = lambda a: pad_q(a.reshape(B, S, kvw))
    pe_k, w1_k, b1_k, w2_k, b2_k = cmp_k
    pe_v, w1_v, b1_v, w2_v, b2_v = cmp_v
    row = lambda a: a.reshape(1, -1).astype(f32)
    n_keys = past + SEL_TAIL
    lane = jnp.arange(P_NKEYS, dtype=jnp.int32)[:, None]
    key_blk = (jnp.arange(n_keys, dtype=jnp.int32) // L_SEL)[None, :]
    expand = ((lane % 2 == 0) & (lane // 2 == key_blk)).astype(bf16)
    chunk_rows = CMP_CHUNK * L_CMP
    dst = jnp.arange(chunk_rows, dtype=jnp.int32)
    src_row = (dst % CMP_CHUNK) * L_CMP + dst // CMP_CHUNK
    perm = (src_row[:, None] == jnp.arange(chunk_rows, dtype=jnp.int32)[None, :]).astype(bf16)
    tile_pe = lambda pe: jnp.tile(pe.astype(f32), (CMP_CHUNK, 1))

    hbm = pl.BlockSpec(memory_space=pl.ANY)
    seq3 = lambda r, c: pl.BlockSpec((1, r, c), lambda b, pt: (b, 0, 0))
    seq4 = lambda a, r, c: pl.BlockSpec((1, a, r, c), lambda b, pt: (b, 0, 0, 0))
    full = lambda a: pl.BlockSpec(a.shape, lambda b, pt: (0,) * a.ndim, pipeline_mode=pl.Buffered(1))
    consts = [tile_pe(pe_k), tile_pe(pe_v), perm, w1_k.astype(bf16), row(b1_k), w2_k.astype(bf16), row(b2_k),
              w1_v.astype(bf16), row(b1_v), w2_v.astype(bf16), row(b2_v), row(g_kc), expand]
    slots = 2
    out = pl.pallas_call(
        functools.partial(_nsa_sample_kernel, past=past, n_pages=n_pages, n_new=S),
        grid_spec=pltpu.PrefetchScalarGridSpec(
            num_scalar_prefetch=1,
            grid=(B,),
            in_specs=[hbm] * 6
                     + [seq4(A_KV, A_HPG * SQ_PAD, A_DH), seq4(A_KV, A_HPG * SQ_PAD, 3)]
                     + [seq3(SQ_PAD, kvw)] * 4
                     + [full(c) for c in consts],
            out_specs=pl.BlockSpec((1, SQ_PAD, A_W), lambda b, pt: (b, 0, 0)),
            scratch_shapes=[pltpu.VMEM((slots, A_KV, past, A_DH), f32), pltpu.VMEM((slots, A_KV, past, A_DH), f32),
                            pltpu.VMEM((slots, A_KV, n_keys, A_DH), f32), pltpu.VMEM((slots, A_KV, n_keys, A_DH), f32),
                            pltpu.VMEM((slots, A_KV, wb + SEL_TAIL, A_DH), f32),
                            pltpu.VMEM((slots, A_KV, wb + SEL_TAIL, A_DH), f32),
                            pltpu.SemaphoreType.DMA((slots,))]),
        out_shape=jax.ShapeDtypeStruct((B, SQ_PAD, A_W), f32),
        compiler_params=pltpu.CompilerParams(
            dimension_semantics=("arbitrary",), vmem_limit_bytes=VMEM_LIMIT_BYTES),
        name="nsa_sample",
    )(page_table, pool_kc, pool_vc, pool_ks, pool_vs, buf_kw, buf_vw, qs, gl,
      new(ksn), new(vsn), new(kwn), new(vwn), *consts)
    return out[:, :S]


def merge_groups(ret_o, rg, nsa_o, r_gn, w_out):
    B, T = rg.shape[:2]
    r = rmsnorm(ret_o, r_gn).reshape(B, T, R_W) * jax.nn.silu(rg.astype(jnp.float32))
    mix = jnp.concatenate([r.astype(rg.dtype), nsa_o.reshape(B, T, A_W).astype(rg.dtype)], axis=-1)
    return mmnd(mix, w_out)


def memory_kv(mem, g_memtok, w_mk, w_mv, m_kn):
    B, M, _ = mem.shape
    m = rmsnorm(mem, g_memtok)
    k = rmsnorm(mmnd(m, w_mk).reshape(B, M, M_HEADS, M_DH), m_kn)
    v = mmnd(m, w_mv).reshape(B, M, M_HEADS, M_DH)
    return k, v


def memory_attend(h, mk, mv, w_mq, m_qn, w_mo):
    B, T, _ = h.shape
    q = rmsnorm(mmnd(h, w_mq).reshape(B, T, M_HEADS, M_DH), m_qn)
    s = jnp.einsum('bthd,bmhd->bhtm', q.astype(jnp.float32), mk.astype(jnp.float32)) * M_DH ** -0.5
    p = jax.nn.softmax(s, axis=-1)
    o = jnp.einsum('bhtm,bmhd->bthd', p, mv.astype(jnp.float32)).astype(h.dtype)
    return mmnd(o.reshape(B, T, M_W), w_mo)


def kernel(x_prompt, x_sample, mem_prompt, cache_k_cmp, cache_v_cmp, cache_k_sel, cache_v_sel, state_k_win, state_v_win, state_ret, cache_mem_k, cache_mem_v, page_table, g_mix, w_in, r_gn, a_qn, a_kcn, a_ksn, a_kwn, cmp_pe_k, cmp_w1_k, cmp_b1_k, cmp_w2_k, cmp_b2_k, cmp_pe_v, cmp_w1_v, cmp_b1_v, cmp_w2_v, cmp_b2_v, w_out, g_xattn, g_memtok, w_mq, w_mk, w_mv, m_qn, m_kn, w_mo, g_ffn, w_pq, sub_k1, sub_k2, exp_u, exp_v):
    cmp_k = (cmp_pe_k, cmp_w1_k, cmp_b1_k, cmp_w2_k, cmp_b2_k)
    cmp_v = (cmp_pe_v, cmp_w1_v, cmp_b1_v, cmp_w2_v, cmp_b2_v)
    xp, xs = x_prompt, x_sample

    T = xp.shape[1]
    w_in = w_in.astype(jnp.bfloat16)
    w_out = w_out.astype(jnp.bfloat16)
    assert xp.shape[0] == 1
    proj_p = mm(rmsnorm(xp, g_mix).astype(jnp.bfloat16)[0], w_in)
    ret_mix, ret_p = retention_prompt_pallas(proj_p, r_gn)
    ret_p = ret_p[None]
    cuts = np.cumsum(IN_SIZES)[3:-1].tolist()
    aq, akc, avc, aks, avs, akw, avw, ag_p = jnp.split(proj_p[None, :, cuts[0]:], [c - cuts[0] for c in cuts[1:]], axis=-1)
    kv = lambda a: a.reshape(1, T, A_KV, A_DH)
    aq, akc, avc, aks, avs, akw, avw = aq.reshape(1, T, A_HEADS, A_DH), kv(akc), kv(avc), kv(aks), kv(avs), kv(akw), kv(avw)
    aq, aks, akw = rmsnorm(aq, a_qn), rmsnorm(aks, a_ksn), rmsnorm(akw, a_kwn)
    kc = rmsnorm(compress(akc, *cmp_k), a_kcn)
    vc = compress(avc, *cmp_v)
    nsa_o = nsa_prompt_pallas(aq[0], kc[0], vc[0], aks[0], avs[0], akw[0], avw[0], ag_p[0])
    hp = xp + mm(jnp.concatenate([ret_mix, nsa_o], axis=-1).astype(jnp.bfloat16), w_out)[None]
    mem_k_p, mem_v_p = memory_kv(mem_prompt, g_memtok, w_mk, w_mv, m_kn)
    hp = hp + memory_attend(rmsnorm(hp, g_xattn), mem_k_p, mem_v_p, w_mq, m_qn, w_mo)
    wl = min(WINDOW, T)
    k_cmp_p, v_cmp_p, k_sel_p, v_sel_p = akc, avc, aks, avs
    k_win_p, v_win_p = akw[:, T - wl:], avw[:, T - wl:]
    ret_p = ret_p.astype(xp.dtype)

    S = xs.shape[1]
    past = page_table.shape[1] * PAGE_SIZE
    rq, rk, rv, rg, aq, akc, avc, aks, avs, akw, avw, ag, ag_s = split_mix(
        rmsnorm(xs, g_mix).astype(jnp.bfloat16), w_in)
    pos = past + jnp.arange(S)
    ret_s, ret_o = retention_chunk(state_ret.astype(jnp.float32), rotary(rq, pos) * R_DK ** -0.5,
                                   rotary(rk, pos), rv)
    aq, aks, akw = rmsnorm(aq, a_qn), rmsnorm(aks, a_ksn), rmsnorm(akw, a_kwn)
    nsa_o = nsa_sample_pallas(aq, aks, avs, akw, avw, ag_s, cache_k_cmp, cache_v_cmp, cache_k_sel,
                              cache_v_sel, state_k_win, state_v_win, page_table, cmp_k, cmp_v, a_kcn)
    hs = xs + merge_groups(ret_o, rg, nsa_o, r_gn, w_out)
    hs = hs + memory_attend(rmsnorm(hs, g_xattn), cache_mem_k, cache_mem_v, w_mq, m_qn, w_mo)
    n_p = hp.shape[0] * hp.shape[1]
    tokens = jnp.concatenate([rmsnorm(hp, g_ffn).reshape(-1, D_MODEL), rmsnorm(hs, g_ffn).reshape(-1, D_MODEL)], axis=0)
    y = peer_pallas(tokens, w_pq, sub_k1, sub_k2, exp_u.astype(jnp.bfloat16), exp_v.astype(jnp.bfloat16))
    xp = hp + y[:n_p].reshape(hp.shape)
    xs = hs + y[n_p:].reshape(hs.shape)
    wb = state_k_win.shape[1]
    k_cmp_s, v_cmp_s, k_sel_s, v_sel_s = akc, avc, aks, avs
    k_win_s = jnp.concatenate([state_k_win, akw.astype(state_k_win.dtype)], axis=1)[:, -wb:]
    v_win_s = jnp.concatenate([state_v_win, avw.astype(state_v_win.dtype)], axis=1)[:, -wb:]
    ret_s = ret_s.astype(state_ret.dtype)
    return (xp, xs, k_cmp_p, v_cmp_p, k_sel_p, v_sel_p, k_win_p, v_win_p, ret_p, mem_k_p, mem_v_p,
            k_cmp_s, v_cmp_s, k_sel_s, v_sel_s, k_win_s, v_win_s, ret_s)
```

```python
import functools

import jax
import jax.numpy as jnp
import numpy as np
from jax import lax
from jax.experimental import pallas as pl
from jax.experimental.pallas import tpu as pltpu

D_MODEL = 4096
PAGE_SIZE = 128
R_HEADS = 8
R_DK = 256
R_DV = 256
R_CHUNK = 128
ROPE_BASE = 10000.0
A_HEADS = 16
A_KV = 4
A_DH = 128
A_HPG = A_HEADS // A_KV
L_CMP = 32
CMP_HID = 256
L_SEL = 64
N_SEL = 16
WINDOW = 512
Q_BLOCK = 128
FORCE_SCORE = 1e4
NEG = -1e30
M_HEADS = 4
M_DH = 128
M_W = M_HEADS * M_DH
P_HEADS = 8
P_NKEYS = 128
P_DQ = 256
P_TOPK = 16
P_BLOCK = 128
EPS = 1e-6
R_W = R_HEADS * R_DV
A_W = A_HEADS * A_DH
IN_SIZES = (R_HEADS * R_DK, R_HEADS * R_DK, R_W, R_W, A_W) + (A_KV * A_DH,) * 6 + (A_HEADS * 3,)

VMEM_LIMIT_BYTES = 56 * 1024 * 1024


def _mm_kernel(a_ref, b_ref, o_ref, acc_ref):
    k = pl.program_id(2)

    @pl.when(k == 0)
    def _():
        acc_ref[...] = jnp.zeros_like(acc_ref)

    acc_ref[...] += jnp.dot(a_ref[...].astype(jnp.bfloat16), b_ref[...].astype(jnp.bfloat16),
                            preferred_element_type=jnp.float32)

    @pl.when(k == pl.num_programs(2) - 1)
    def _():
        o_ref[...] = acc_ref[...]


def _mm_fullk_kernel(a_ref, b_ref, o_ref):
    o_ref[...] = jnp.dot(a_ref[...].astype(jnp.bfloat16), b_ref[...].astype(jnp.bfloat16),
                         preferred_element_type=jnp.float32)


MM_FULLK_TM = 1024
MM_FULLK_TN = 512


def mm(a, b, tm=1024, tn=1024, tk=1024):
    m, kdim = a.shape
    _, n = b.shape
    if (a.dtype == jnp.bfloat16 and b.dtype == jnp.bfloat16 and m % MM_FULLK_TM == 0
            and n >= MM_FULLK_TN and kdim <= 4096):
        return pl.pallas_call(
            _mm_fullk_kernel,
            grid=(m // MM_FULLK_TM, pl.cdiv(n, MM_FULLK_TN)),
            in_specs=[pl.BlockSpec((MM_FULLK_TM, kdim), lambda i, j: (i, 0)),
                      pl.BlockSpec((kdim, MM_FULLK_TN), lambda i, j: (0, j))],
            out_specs=pl.BlockSpec((MM_FULLK_TM, MM_FULLK_TN), lambda i, j: (i, j)),
            out_shape=jax.ShapeDtypeStruct((m, n), jnp.float32),
            compiler_params=pltpu.CompilerParams(
                dimension_semantics=("parallel", "parallel"), vmem_limit_bytes=VMEM_LIMIT_BYTES),
            name="mm_fullk",
        )(a, b)
    tm = min(tm, m)
    tn = min(tn, n)
    tk = min(tk, kdim)
    assert m % tm == 0 and kdim % tk == 0
    grid = (m // tm, pl.cdiv(n, tn), kdim // tk)
    return pl.pallas_call(
        _mm_kernel,
        grid=grid,
        in_specs=[pl.BlockSpec((tm, tk), lambda i, j, k: (i, k)),
                  pl.BlockSpec((tk, tn), lambda i, j, k: (k, j))],
        out_specs=pl.BlockSpec((tm, tn), lambda i, j, k: (i, j)),
        out_shape=jax.ShapeDtypeStruct((m, n), jnp.float32),
        scratch_shapes=[pltpu.VMEM((tm, tn), jnp.float32)],
        compiler_params=pltpu.CompilerParams(
            dimension_semantics=("parallel", "parallel", "arbitrary"),
            vmem_limit_bytes=VMEM_LIMIT_BYTES),
        name="mm",
    )(a, b)


def mmnd(a, b):
    lead = a.shape[:-1]
    out = mm(a.reshape(-1, a.shape[-1]), b)
    return out.reshape(lead + (b.shape[-1],))


def _mm_nt_kernel(a_ref, b_ref, o_ref):
    o_ref[...] = lax.dot_general(a_ref[...].astype(jnp.bfloat16), b_ref[...].astype(jnp.bfloat16),
                                 (((1,), (1,)), ((), ())), preferred_element_type=jnp.float32)


def mm_nt(a, b, tm=512, tn=512):
    m, kdim = a.shape
    n, _ = b.shape
    tm, tn = min(tm, m), min(tn, n)
    assert m % tm == 0 and n % tn == 0
    return pl.pallas_call(
        _mm_nt_kernel,
        grid=(m // tm, n // tn),
        in_specs=[pl.BlockSpec((tm, kdim), lambda i, j: (i, 0)),
                  pl.BlockSpec((tn, kdim), lambda i, j: (j, 0))],
        out_specs=pl.BlockSpec((tm, tn), lambda i, j: (i, j)),
        out_shape=jax.ShapeDtypeStruct((m, n), jnp.float32),
        compiler_params=pltpu.CompilerParams(
            dimension_semantics=("parallel", "parallel"),
            vmem_limit_bytes=VMEM_LIMIT_BYTES),
        name="mm_nt",
    )(a, b)


PEER_TR = 256
PEER_TM = 512
PEER_TE = 512
PEER_SLABS = PEER_TE // P_NKEYS
TAKEN = -3.0e38


def _extract_top(s, n):
    rows = s.shape[0]
    row_id = lax.broadcasted_iota(jnp.int32, s.shape, 0).astype(jnp.float32)
    rank = jnp.full(s.shape, float(n), jnp.float32)
    vals = []
    for k in range(n):
        m = jnp.max(s, axis=0, keepdims=True)
        first = jnp.min(jnp.where(s == m, row_id, float(rows)), axis=0, keepdims=True)
        hit = row_id == first
        rank = jnp.where(hit, float(k), rank)
        s = jnp.where(hit, TAKEN, s)
        vals.append(m)
    return rank, jnp.concatenate(vals, axis=0)


def _peer_route_kernel(qt_ref, k1_ref, k2_ref, r2_ref, e2_ref, n_ref, e1n_ref):
    f32, bf16 = jnp.float32, jnp.bfloat16
    half = P_DQ // 2
    qt = qt_ref[...].astype(bf16)
    s1 = jnp.dot(k1_ref[0].astype(bf16), qt[:half], preferred_element_type=f32)
    s2 = jnp.dot(k2_ref[0].astype(bf16), qt[half:], preferred_element_type=f32)
    rank1, v1 = _extract_top(s1, P_TOPK)
    rank2, v2 = _extract_top(s2, P_TOPK)
    r1_id = lax.broadcasted_iota(jnp.int32, v1.shape, 0).astype(f32)
    cnt = jnp.zeros_like(v1)
    for _ in range(P_TOPK):
        nxt = jnp.full_like(v1, TAKEN)
        for j in range(P_TOPK):
            nxt = jnp.where(cnt == float(j), v2[j:j + 1], nxt)
        front = jnp.where(cnt < float(P_TOPK), v1 + nxt, TAKEN)
        m = jnp.max(front, axis=0, keepdims=True)
        first = jnp.min(jnp.where(front == m, r1_id, float(P_TOPK)), axis=0, keepdims=True)
        cnt = cnt + jnp.where(r1_id == first, 1.0, 0.0)
    ev1 = jnp.exp(v1 - v1[0:1])
    ev2 = jnp.exp(v2 - v2[0:1])
    inner = jnp.zeros_like(v1)
    for j in range(P_TOPK):
        inner = inner + jnp.where(cnt > float(j), ev2[j:j + 1], 0.0)
    z = jnp.sum(ev1 * inner, axis=0, keepdims=True)
    n_dense = jnp.zeros_like(s1)
    for r in range(P_TOPK):
        n_dense = n_dense + jnp.where(rank1 == float(r), cnt[r:r + 1], 0.0)
    r2_ref[0] = rank2
    e2_ref[0] = jnp.exp(s2 - v2[0:1])
    n_ref[0] = n_dense
    e1n_ref[0] = jnp.exp(s1 - v1[0:1]) * (1.0 / z)


def _peer_mix_kernel(x_ref, ua_ref, ub_ref, va_ref, vb_ref, r2_ref, e2_ref, n_ref, e1n_ref, y_ref, a_scr):
    f32, bf16 = jnp.float32, jnp.bfloat16
    j = pl.program_id(1)
    half_rows = PEER_TE // 2

    @pl.when(j == 0)
    def _():
        y_ref[...] = jnp.zeros_like(y_ref)
        a_scr[1] = jnp.zeros(a_scr.shape[1:], f32)

    act = jax.nn.gelu(a_scr[(j + 1) % 2])
    nt_dims = (((1,), (1,)), ((), ()))
    a_scr[j % 2, :half_rows, :] = lax.dot_general(ua_ref[...], x_ref[...], nt_dims, preferred_element_type=f32)
    a_scr[j % 2, half_rows:, :] = lax.dot_general(ub_ref[...], x_ref[...], nt_dims, preferred_element_type=f32)

    tile = jnp.maximum(j - 1, 0)
    parts = []
    for s in range(PEER_SLABS):
        first_key = tile * PEER_SLABS + s
        g = jnp.zeros((P_NKEYS, act.shape[1]), f32)
        for h in range(P_HEADS):
            n_row = n_ref[h, pl.ds(first_key, 1), :]
            w_row = e1n_ref[h, pl.ds(first_key, 1), :]
            g = g + jnp.where(r2_ref[h] < n_row, e2_ref[h] * w_row, 0.0)
        parts.append((g * act[s * P_NKEYS:(s + 1) * P_NKEYS]).astype(bf16))
    tn_dims = (((0,), (0,)), ((), ()))
    w_a = jnp.concatenate(parts[:PEER_SLABS // 2], axis=0)
    w_b = jnp.concatenate(parts[PEER_SLABS // 2:], axis=0)
    y_ref[...] += (lax.dot_general(w_a, va_ref[...], tn_dims, preferred_element_type=f32)
                   + lax.dot_general(w_b, vb_ref[...], tn_dims, preferred_element_type=f32))


def peer_pallas(x, w_pq, sub_k1, sub_k2, exp_u_b, exp_v_b):
    bf16 = jnp.bfloat16
    n, d = x.shape
    n_exp = exp_u_b.shape[0]
    assert n % PEER_TM == 0 and n % PEER_TR == 0 and n_exp % PEER_TE == 0
    xb = x.astype(bf16)
    qt = mm_nt(jnp.swapaxes(w_pq, 0, 1), xb)
    route_shape = jax.ShapeDtypeStruct((P_HEADS, P_NKEYS, n), jnp.float32)
    key_spec = pl.BlockSpec((1, P_NKEYS, P_DQ // 2), lambda i, h: (h, 0, 0))
    route_out = pl.BlockSpec((1, P_NKEYS, PEER_TR), lambda i, h: (h, 0, i))
    r2, e2, nd, e1n = pl.pallas_call(
        _peer_route_kernel,
        grid=(n // PEER_TR, P_HEADS),
        in_specs=[pl.BlockSpec((P_DQ, PEER_TR), lambda i, h: (h, i)), key_spec, key_spec],
        out_specs=[route_out] * 4,
        out_shape=[route_shape] * 4,
        compiler_params=pltpu.CompilerParams(
            dimension_semantics=("parallel", "parallel"), vmem_limit_bytes=VMEM_LIMIT_BYTES),
        name="peer_route",
    )(qt, sub_k1, sub_k2)
    route_in = pl.BlockSpec((P_HEADS, P_NKEYS, PEER_TM), lambda i, j: (0, 0, i),
                            pipeline_mode=pl.Buffered(1))
    n_tiles = n_exp // PEER_TE
    return pl.pallas_call(
        _peer_mix_kernel,
        grid=(n // PEER_TM, n_tiles + 1),
        in_specs=[pl.BlockSpec((PEER_TM, d), lambda i, j: (i, 0), pipeline_mode=pl.Buffered(1)),
                  pl.BlockSpec((PEER_TE // 2, d), lambda i, j: (2 * jnp.minimum(j, n_tiles - 1), 0)),
                  pl.BlockSpec((PEER_TE // 2, d), lambda i, j: (2 * jnp.minimum(j, n_tiles - 1) + 1, 0)),
                  pl.BlockSpec((PEER_TE // 2, d), lambda i, j: (2 * jnp.maximum(j - 1, 0), 0)),
                  pl.BlockSpec((PEER_TE // 2, d), lambda i, j: (2 * jnp.maximum(j - 1, 0) + 1, 0)),
                  route_in, route_in, route_in, route_in],
        out_specs=pl.BlockSpec((PEER_TM, d), lambda i, j: (i, 0)),
        out_shape=jax.ShapeDtypeStruct((n, d), jnp.float32),
        scratch_shapes=[pltpu.VMEM((2, PEER_TE, PEER_TM), jnp.float32)],
        compiler_params=pltpu.CompilerParams(
            dimension_semantics=("parallel", "arbitrary"), vmem_limit_bytes=VMEM_LIMIT_BYTES),
        name="peer_mix",
    )(xb, exp_u_b, exp_u_b, exp_v_b, exp_v_b, r2, e2, nd, e1n)


def _retention_prompt_kernel(q_ref, k_ref, v_ref, gate_ref, cos_ref, sin_ref, dmask_ref, dq_ref, dk_ref,
                             cdec_ref, gn_ref, o_ref, state_ref, s_scr):
    f32, bf16 = jnp.float32, jnp.bfloat16
    c = pl.program_id(1)

    @pl.when(c == 0)
    def _():
        s_scr[...] = jnp.zeros_like(s_scr)

    cos, sin = cos_ref[...], sin_ref[...]
    half = R_DK // 2

    def rot(x):
        x1, x2 = x[:, :half], x[:, half:]
        return jnp.concatenate([x1 * cos - x2 * sin, x1 * sin + x2 * cos], axis=1)

    q = (rot(q_ref[...]) * R_DK ** -0.5).astype(bf16)
    k = rot(k_ref[...])
    v = v_ref[...].astype(bf16)
    s_old = s_scr[...]
    inner = lax.dot_general(q, k.astype(bf16), (((1,), (1,)), ((), ())),
                            preferred_element_type=f32) * dmask_ref[0]
    o = jnp.dot(inner.astype(bf16), v, preferred_element_type=f32)
    o = o + jnp.dot(q, s_old.astype(bf16), preferred_element_type=f32) * dq_ref[0]
    kdec = (k * dk_ref[0]).astype(bf16)
    s_new = s_old * cdec_ref[0] + lax.dot_general(kdec, v, (((0,), (0,)), ((), ())),
                                                  preferred_element_type=f32)
    s_scr[...] = s_new
    y = o * lax.rsqrt(jnp.mean(o * o, axis=-1, keepdims=True) + EPS) * gn_ref[0]
    gate = gate_ref[...]
    o_ref[...] = y * (gate * (1.0 / (1.0 + jnp.exp(-gate))))

    @pl.when(c == pl.num_programs(1) - 1)
    def _():
        state_ref[0] = s_new


def retention_prompt_pallas(proj, r_gn):
    f32 = jnp.float32
    T = proj.shape[0]
    assert T % R_CHUNK == 0 and R_DK == R_DV
    half = R_DK // 2
    inv = ROPE_BASE ** (-jnp.arange(half, dtype=f32) / half)
    ang = jnp.arange(T, dtype=f32)[:, None] * inv[None, :]
    lg = jnp.log1p(-jnp.exp2(-5.0 - jnp.arange(R_HEADS, dtype=f32)))
    idx = jnp.arange(R_CHUNK, dtype=f32)
    rel = idx[:, None] - idx[None, :]
    dmask = jnp.where(rel[None] >= 0, jnp.exp(jnp.maximum(rel, 0.0)[None] * lg[:, None, None]), 0.0)
    dq = jnp.exp((idx + 1.0)[None, :] * lg[:, None])[:, :, None]
    dk = jnp.exp((R_CHUNK - 1.0 - idx)[None, :] * lg[:, None])[:, :, None]
    cdec = jnp.broadcast_to(jnp.exp(R_CHUNK * lg)[:, None, None], (R_HEADS, 1, R_DV))
    col = lambda base: pl.BlockSpec((R_CHUNK, R_DK), lambda h, c, base=base: (c, base + h))
    per_head = lambda shape: pl.BlockSpec((1,) + shape, lambda h, c: (h, 0, 0))
    trig = pl.BlockSpec((R_CHUNK, half), lambda h, c: (c, 0))
    return pl.pallas_call(
        _retention_prompt_kernel,
        grid=(R_HEADS, T // R_CHUNK),
        in_specs=[col(0), col(R_HEADS), col(2 * R_HEADS), col(3 * R_HEADS), trig, trig,
                  per_head((R_CHUNK, R_CHUNK)), per_head((R_CHUNK, 1)), per_head((R_CHUNK, 1)),
                  per_head((1, R_DV)), per_head((1, R_DV))],
        out_specs=[pl.BlockSpec((R_CHUNK, R_DV), lambda h, c: (c, h)),
                   pl.BlockSpec((1, R_DK, R_DV), lambda h, c: (h, 0, 0))],
        out_shape=[jax.ShapeDtypeStruct((T, R_W), f32),
                   jax.ShapeDtypeStruct((R_HEADS, R_DK, R_DV), f32)],
        scratch_shapes=[pltpu.VMEM((R_DK, R_DV), f32)],
        compiler_params=pltpu.CompilerParams(
            dimension_semantics=("parallel", "arbitrary"), vmem_limit_bytes=VMEM_LIMIT_BYTES),
        name="retention_prompt",
    )(proj, proj, proj, proj, jnp.cos(ang), jnp.sin(ang), dmask, dq, dk, cdec,
      r_gn.astype(f32).reshape(R_HEADS, 1, R_DV))


def rmsnorm(x, g):
    xf = x.astype(jnp.float32)
    y = xf * lax.rsqrt(jnp.mean(xf * xf, axis=-1, keepdims=True) + EPS)
    return (y * g.astype(jnp.float32)).astype(x.dtype)


def rotary(x, pos):
    half = x.shape[-1] // 2
    inv = ROPE_BASE ** (-jnp.arange(half, dtype=jnp.float32) / half)
    ang = pos.astype(jnp.float32)[:, None] * inv[None, :]
    cos, sin = jnp.cos(ang)[:, None, :], jnp.sin(ang)[:, None, :]
    xf = x.astype(jnp.float32)
    x1, x2 = xf[..., :half], xf[..., half:]
    return jnp.concatenate([x1 * cos - x2 * sin, x1 * sin + x2 * cos], axis=-1).astype(x.dtype)


def masked_softmax(s, mask):
    p = jax.nn.softmax(jnp.where(mask, s, NEG), axis=-1)
    return jnp.where(mask, p, 0.0)


def split_mix(h, w_in):
    B, T, _ = h.shape
    cuts = np.cumsum(IN_SIZES)[:-1].tolist()
    rq, rk, rv, rg, aq, akc, avc, aks, avs, akw, avw, ag = jnp.split(mmnd(h, w_in), cuts, axis=-1)
    r = lambda a, n, d: a.reshape(B, T, n, d)
    return (r(rq, R_HEADS, R_DK), r(rk, R_HEADS, R_DK), r(rv, R_HEADS, R_DV), rg,
            r(aq, A_HEADS, A_DH), r(akc, A_KV, A_DH), r(avc, A_KV, A_DH), r(aks, A_KV, A_DH),
            r(avs, A_KV, A_DH), r(akw, A_KV, A_DH), r(avw, A_KV, A_DH),
            jax.nn.sigmoid(ag.astype(jnp.float32)).reshape(B, T, A_HEADS, 3), ag)


def retention_chunk(S, q, k, v):
    q, k, v = q.astype(jnp.float32), k.astype(jnp.float32), v.astype(jnp.float32)
    C = q.shape[1]
    lg = jnp.log1p(-jnp.exp2(-5.0 - jnp.arange(R_HEADS, dtype=jnp.float32)))
    idx = jnp.arange(C, dtype=jnp.float32)
    rel = idx[:, None] - idx[None, :]
    dmask = jnp.where(rel[None] >= 0, jnp.exp(jnp.maximum(rel, 0.0)[None] * lg[:, None, None]), 0.0)
    inner = jnp.einsum('bihd,bjhd->bhij', q, k) * dmask[None]
    o = jnp.einsum('bhij,bjhe->bihe', inner, v)
    o = o + jnp.einsum('bihd,bhde->bihe', q, S) * jnp.exp((idx + 1.0)[:, None] * lg[None, :])[None, :, :, None]
    kdec = k * jnp.exp((C - 1.0 - idx)[:, None] * lg[None, :])[None, :, :, None]
    S_new = S * jnp.exp(C * lg)[None, :, None, None] + jnp.einsum('bjhd,bjhe->bhde', kdec, v)
    return S_new, o


def compress(rows, pe, w1, b1, w2, b2):
    T = rows.shape[-3]
    nc = T // L_CMP
    lead = rows.shape[:-3]
    blk = rows[..., :nc * L_CMP, :, :].reshape(lead + (nc, L_CMP, A_KV, A_DH)) + pe[:, None, :]
    blk = jnp.swapaxes(blk, -3, -2).reshape(lead + (nc, A_KV, L_CMP * A_DH))
    return mmnd(jax.nn.gelu(mmnd(blk, w1) + b1), w2) + b2


NSA_TK = 512
N_TOPK_NEG = -3.0e38


def _softmax_rows(s, mask):
    s = jnp.where(mask, s, NEG)
    m = jnp.max(s, axis=-1, keepdims=True)
    e = jnp.where(mask, jnp.exp(s - m), 0.0)
    l = jnp.sum(e, axis=-1, keepdims=True)
    return e * (1.0 / jnp.maximum(l, 1e-30))


def _top_n_mask_cols(score_t, n):
    rows = score_t.shape[0]
    row_id = lax.broadcasted_iota(jnp.int32, score_t.shape, 0).astype(jnp.float32)

    def one(_, c):
        sc, sel = c
        m = jnp.max(sc, axis=0, keepdims=True)
        first = jnp.min(jnp.where(sc == m, row_id, float(rows)), axis=0, keepdims=True)
        hit = row_id == first
        return jnp.where(hit, N_TOPK_NEG, sc), jnp.where(hit, 1.0, sel)

    _, sel = lax.fori_loop(0, n, one, (score_t, jnp.zeros_like(score_t)), unroll=True)
    return sel


def _nsa_prompt_kernel(q_ref, gate_ref, kc_ref, vc_ref, ks_ref, vs_ref, kw_ref, vw_ref, e_ref, o_ref,
                       *, n_win_keys):
    f32, bf16 = jnp.float32, jnp.bfloat16
    qb = pl.program_id(1)
    start = qb * Q_BLOCK
    nt_dims = (((1,), (1,)), ((), ()))
    q = q_ref[...]
    qg = jnp.concatenate([q[:, h * A_DH:(h + 1) * A_DH] for h in range(A_HPG)], axis=0)
    q_pos = start + lax.broadcasted_iota(jnp.int32, (Q_BLOCK, 1), 0)
    rep = lambda a: jnp.concatenate([a] * A_HPG, axis=0)

    kc, vc = kc_ref[0], vc_ref[0]
    nc = kc.shape[0]
    nsb = nc // 2
    s_c = lax.dot_general(qg, kc, nt_dims, preferred_element_type=f32)
    lane_c = lax.broadcasted_iota(jnp.int32, (1, nc), 1)
    c_orig = jnp.where(lane_c < nsb, 2 * lane_c, 2 * (lane_c - nsb) + 1)
    cmask = ((c_orig + 1) * L_CMP - 1) <= q_pos
    p_c = _softmax_rows(s_c, rep(cmask))
    o_c = jnp.dot(p_c.astype(bf16), vc, preferred_element_type=f32)

    ph = p_c[0:Q_BLOCK]
    for h in range(1, A_HPG):
        ph = ph + p_c[h * Q_BLOCK:(h + 1) * Q_BLOCK]
    imp = ph[:, :nsb] + ph[:, nsb:]
    blk = lax.broadcasted_iota(jnp.int32, (1, nsb), 1)
    cur = q_pos // L_SEL
    forced = (blk == 0) | (blk == cur) | (blk == cur - 1)
    imp = jnp.where(blk <= cur, jnp.where(forced, FORCE_SCORE, imp), NEG)
    sel = _top_n_mask_cols(imp.T, min(N_SEL, nsb)).T
    sel_b = sel.astype(bf16)

    n_tiles = (start + Q_BLOCK + NSA_TK - 1) // NSA_TK

    def sel_step(j, carry):
        m, l, acc = carry
        off = pl.multiple_of(j * NSA_TK, NSA_TK)
        k = ks_ref[0, pl.ds(off, NSA_TK), :]
        v = vs_ref[0, pl.ds(off, NSA_TK), :]
        s = lax.dot_general(qg, k, nt_dims, preferred_element_type=f32)
        picked = jnp.dot(sel_b, e_ref[j], preferred_element_type=f32)
        k_pos = off + lax.broadcasted_iota(jnp.int32, (1, NSA_TK), 1)
        bias = jnp.where((picked > 0.5) & (k_pos <= q_pos), 0.0, NEG)
        s = s + rep(bias)
        m_new = jnp.maximum(m, jnp.max(s, axis=-1, keepdims=True))
        alpha = jnp.exp(m - m_new)
        p = jnp.exp(s - m_new)
        l = alpha * l + jnp.sum(p, axis=-1, keepdims=True)
        acc = alpha * acc + jnp.dot(p.astype(bf16), v, preferred_element_type=f32)
        return m_new, l, acc

    rows = A_HPG * Q_BLOCK
    m_s, l_s, acc_s = lax.fori_loop(
        0, n_tiles, sel_step,
        (jnp.full((rows, 1), NEG, f32), jnp.zeros((rows, 1), f32), jnp.zeros((rows, A_DH), f32)))
    o_s = acc_s * (1.0 / l_s)

    base = pl.multiple_of(jnp.maximum(start + Q_BLOCK - n_win_keys, 0), Q_BLOCK)
    kw = kw_ref[0, pl.ds(base, n_win_keys), :]
    vw = vw_ref[0, pl.ds(base, n_win_keys), :]
    s_w = lax.dot_general(qg, kw, nt_dims, preferred_element_type=f32)
    rel = q_pos - (base + lax.broadcasted_iota(jnp.int32, (1, n_win_keys), 1))
    p_w = _softmax_rows(s_w, rep((rel >= 0) & (rel < WINDOW)))
    o_w = jnp.dot(p_w.astype(bf16), vw, preferred_element_type=f32)

    gl = gate_ref[0]
    g = 1.0 / (1.0 + jnp.exp(-gl))
    outs = []
    for h in range(A_HPG):
        r = slice(h * Q_BLOCK, (h + 1) * Q_BLOCK)
        outs.append(o_c[r] * g[:, 3 * h:3 * h + 1] + o_s[r] * g[:, 3 * h + 1:3 * h + 2]
                    + o_w[r] * g[:, 3 * h + 2:3 * h + 3])
    o_ref[...] = jnp.concatenate(outs, axis=-1)


def nsa_prompt_pallas(q, kc, vc, ks, vs, kw, vw, gate_logits):
    bf16 = jnp.bfloat16
    T = q.shape[0]
    nc = kc.shape[0]
    nsb = T // L_SEL
    assert T % NSA_TK == 0 and nc == 2 * nsb
    n_win_keys = min(WINDOW + Q_BLOCK, T)
    qs = (q.astype(jnp.float32) * A_DH ** -0.5).astype(bf16).reshape(T, A_W)
    grp = lambda a: jnp.swapaxes(a, 0, 1).astype(bf16)
    perm = lambda a: jnp.concatenate([a[0::2], a[1::2]], axis=0)
    gl = jnp.swapaxes(gate_logits.reshape(T, A_KV, A_HPG * 3), 0, 1)
    n_t = T // NSA_TK
    key_blk = (jnp.arange(T, dtype=jnp.int32) // L_SEL).reshape(n_t, 1, NSA_TK)
    expand = (jnp.arange(nsb, dtype=jnp.int32)[None, :, None] == key_blk).astype(bf16)
    kv_spec = lambda rows: pl.BlockSpec((1, rows, A_DH), lambda g, i: (g, 0, 0))
    return pl.pallas_call(
        functools.partial(_nsa_prompt_kernel, n_win_keys=n_win_keys),
        grid=(A_KV, T // Q_BLOCK),
        in_specs=[pl.BlockSpec((Q_BLOCK, A_HPG * A_DH), lambda g, i: (i, g)),
                  pl.BlockSpec((1, Q_BLOCK, A_HPG * 3), lambda g, i: (g, i, 0)),
                  kv_spec(nc), kv_spec(nc), kv_spec(T), kv_spec(T), kv_spec(T), kv_spec(T),
                  pl.BlockSpec((n_t, nsb, NSA_TK), lambda g, i: (0, 0, 0))],
        out_specs=pl.BlockSpec((Q_BLOCK, A_HPG * A_DH), lambda g, i: (i, g)),
        out_shape=jax.ShapeDtypeStruct((T, A_W), jnp.float32),
        compiler_params=pltpu.CompilerParams(
            dimension_semantics=("parallel", "arbitrary"),
            vmem_limit_bytes=VMEM_LIMIT_BYTES),
        name="nsa_prompt",
    )(qs, gl, grp(perm(kc)), grp(perm(vc)), grp(ks), grp(vs), grp(kw), grp(vw), expand)


SQ_PAD = 8
SEL_TAIL = 128


def _top_n_mask_rows(score, n):
    cols = score.shape[1]
    col_id = lax.broadcasted_iota(jnp.int32, score.shape, 1).astype(jnp.float32)

    def one(_, c):
        sc, sel = c
        m = jnp.max(sc, axis=1, keepdims=True)
        first = jnp.min(jnp.where(sc == m, col_id, float(cols)), axis=1, keepdims=True)
        hit = col_id == first
        return jnp.where(hit, N_TOPK_NEG, sc), jnp.where(hit, 1.0, sel)

    _, sel = lax.fori_loop(0, n, one, (score, jnp.zeros_like(score)), unroll=True)
    return sel


CMP_CHUNK = 8


def _compress_seq(src, pe_ref, perm_ref, w1_ref, b1_ref, w2_ref, b2_ref, n_blk):
    f32, bf16 = jnp.float32, jnp.bfloat16
    chunk_rows = CMP_CHUNK * L_CMP
    regrouped = []
    for g in range(A_KV):
        for j in range(n_blk // CMP_CHUNK):
            x = (src[g, j * chunk_rows:(j + 1) * chunk_rows, :] + pe_ref[...]).astype(bf16)
            regrouped.append(jnp.dot(perm_ref[...], x, preferred_element_type=f32))
    acc = jnp.zeros((A_KV * n_blk, CMP_HID), f32)
    for l in range(L_CMP):
        xg = jnp.concatenate([y[l * CMP_CHUNK:(l + 1) * CMP_CHUNK] for y in regrouped], axis=0).astype(bf16)
        acc = acc + jnp.dot(xg, w1_ref[l * A_DH:(l + 1) * A_DH, :], preferred_element_type=jnp.float32)
    hid = jax.nn.gelu(acc + b1_ref[...])
    return jnp.dot(hid.astype(bf16), w2_ref[...], preferred_element_type=jnp.float32) + b2_ref[...]


def _nsa_sample_kernel(pt_ref, kcp_ref, vcp_ref, ksp_ref, vsp_ref, kwb_ref, vwb_ref, q_ref, gate_ref,
                       ksn_ref, vsn_ref, kwn_ref, vwn_ref, pek_ref, pev_ref, perm_ref,
                       w1k_ref, b1k_ref, w2k_ref, b2k_ref, w1v_ref, b1v_ref, w2v_ref, b2v_ref, gkc_ref,
                       e_ref, o_ref, kc_scr, vc_scr, ks_scr, vs_scr, kw_scr, vw_scr, sem,
                       *, past, n_pages, n_new):
    f32, bf16 = jnp.float32, jnp.bfloat16
    b = pl.program_id(0)
    slot = b % 2

    def seq_copies(seq, dst_slot):
        cps = []
        for pg in range(n_pages):
            page = pt_ref[seq, pg]
            rows = pl.ds(pg * PAGE_SIZE, PAGE_SIZE)
            for g in range(A_KV):
                for pool_ref, scr in ((kcp_ref, kc_scr), (vcp_ref, vc_scr), (ksp_ref, ks_scr), (vsp_ref, vs_scr)):
                    cps.append(pltpu.make_async_copy(pool_ref.at[page, :, g, :], scr.at[dst_slot, g, rows, :],
                                                     sem.at[dst_slot]))
        for g in range(A_KV):
            for buf_ref, scr in ((kwb_ref, kw_scr), (vwb_ref, vw_scr)):
                cps.append(pltpu.make_async_copy(buf_ref.at[seq, :, g, :], scr.at[dst_slot, g, pl.ds(0, wb), :],
                                                 sem.at[dst_slot]))
        return cps

    wb = kwb_ref.shape[1]

    @pl.when(b == 0)
    def _():
        for cp in seq_copies(0, 0):
            cp.start()

    @pl.when(b + 1 < pl.num_programs(0))
    def _():
        for cp in seq_copies(b + 1, 1 - slot):
            cp.start()

    for cp in seq_copies(b, slot):
        cp.wait()

    def compute():
        nt_dims = (((1,), (1,)), ((), ()))
        n_blk = past // L_CMP
        half = n_blk // 2
        rows = A_HPG * SQ_PAD
        pad_rows = jnp.zeros((SEL_TAIL - SQ_PAD, A_DH), f32)
        tail = lambda new_ref, g: jnp.concatenate(
            [new_ref[0, :, g * A_DH:(g + 1) * A_DH], pad_rows], axis=0)
        for g in range(A_KV):
            ks_scr[slot, g, pl.ds(past, SEL_TAIL), :] = tail(ksn_ref, g)
            vs_scr[slot, g, pl.ds(past, SEL_TAIL), :] = tail(vsn_ref, g)
            kw_scr[slot, g, pl.ds(wb, SEL_TAIL), :] = tail(kwn_ref, g)
            vw_scr[slot, g, pl.ds(wb, SEL_TAIL), :] = tail(vwn_ref, g)
        kcmp = _compress_seq(kc_scr.at[slot], pek_ref, perm_ref, w1k_ref, b1k_ref, w2k_ref, b2k_ref, n_blk)
        kcmp = kcmp * lax.rsqrt(jnp.mean(kcmp * kcmp, axis=-1, keepdims=True) + EPS) * gkc_ref[...]
        vcmp = _compress_seq(vc_scr.at[slot], pev_ref, perm_ref, w1v_ref, b1v_ref, w2v_ref, b2v_ref, n_blk)

        q_pos = past + (lax.broadcasted_iota(jnp.int32, (SQ_PAD, 1), 0))
        rep = lambda a: jnp.concatenate([a] * A_HPG, axis=0)
        lanes = P_NKEYS
        zero_blk = jnp.zeros((lanes - n_blk, A_DH), f32)
        lane_c = lax.broadcasted_iota(jnp.int32, (1, lanes), 1)
        cmask = jnp.broadcast_to(lane_c < n_blk, (rows, lanes))

        o_c, imps = [], []
        for g in range(A_KV):
            qg = q_ref[0, g]
            kc_g = jnp.concatenate([kcmp[g * n_blk:(g + 1) * n_blk], zero_blk], axis=0).astype(bf16)
            vc_g = jnp.concatenate([vcmp[g * n_blk:(g + 1) * n_blk], zero_blk], axis=0).astype(bf16)
            s_c = lax.dot_general(qg, kc_g, nt_dims, preferred_element_type=f32)
            p_c = _softmax_rows(s_c, cmask)
            o_c.append(jnp.dot(p_c.astype(bf16), vc_g, preferred_element_type=f32))
            ph = p_c[0:SQ_PAD]
            for h in range(1, A_HPG):
                ph = ph + p_c[h * SQ_PAD:(h + 1) * SQ_PAD]
            imps.append(ph + pltpu.roll(ph, lanes - 1, axis=1))
        imp = jnp.concatenate(imps, axis=0)
        blk = lane_c // 2
        q_pos4 = jnp.concatenate([q_pos] * A_KV, axis=0)
        cur = q_pos4 // L_SEL
        forced = (blk == 0) | (blk == cur) | (blk == cur - 1)
        imp = jnp.where(blk <= cur, jnp.where(forced, FORCE_SCORE, imp), NEG)
        imp = jnp.where(lane_c % 2 == 0, imp, N_TOPK_NEG)
        nsb = -(-(past + n_new) // L_SEL)
        sel = _top_n_mask_rows(imp, min(N_SEL, nsb))
        picked = jnp.dot(sel.astype(bf16), e_ref[...], preferred_element_type=f32)
        n_keys = past + SEL_TAIL
        k_pos = lax.broadcasted_iota(jnp.int32, (1, n_keys), 1)
        w_pos = past - wb + lax.broadcasted_iota(jnp.int32, (1, wb + SEL_TAIL), 1)
        rel = q_pos - w_pos
        w_mask = rep((rel >= 0) & (rel < WINDOW) & (w_pos >= 0))

        gl = gate_ref[0]
        outs = []
        for g in range(A_KV):
            qg = q_ref[0, g]
            s_mask = rep((picked[g * SQ_PAD:(g + 1) * SQ_PAD] > 0.5) & (k_pos <= q_pos))
            s_s = lax.dot_general(qg, ks_scr[slot, g].astype(bf16), nt_dims, preferred_element_type=f32)
            p_s = _softmax_rows(s_s, s_mask)
            o_s = jnp.dot(p_s.astype(bf16), vs_scr[slot, g].astype(bf16), preferred_element_type=f32)
            s_w = lax.dot_general(qg, kw_scr[slot, g].astype(bf16), nt_dims, preferred_element_type=f32)
            p_w = _softmax_rows(s_w, w_mask)
            o_w = jnp.dot(p_w.astype(bf16), vw_scr[slot, g].astype(bf16), preferred_element_type=f32)
            gs = 1.0 / (1.0 + jnp.exp(-gl[g]))
            o = o_c[g] * gs[:, 0:1] + o_s * gs[:, 1:2] + o_w * gs[:, 2:3]
            outs.extend([o[h * SQ_PAD:(h + 1) * SQ_PAD] for h in range(A_HPG)])
        o_ref[0] = jnp.concatenate(outs, axis=1)

    compute()


def nsa_sample_pallas(q, ksn, vsn, kwn, vwn, gate_logits, pool_kc, pool_vc, pool_ks, pool_vs,
                      buf_kw, buf_vw, page_table, cmp_k, cmp_v, g_kc):
    f32, bf16 = jnp.float32, jnp.bfloat16
    B, S = q.shape[:2]
    assert S <= SQ_PAD and S < L_CMP
    n_pages = page_table.shape[1]
    past = n_pages * PAGE_SIZE
    wb = buf_kw.shape[1]
    kvw = A_KV * A_DH
    pad_q = lambda a: jnp.pad(a, ((0, 0), (0, SQ_PAD - S)) + ((0, 0),) * (a.ndim - 2))
    qs = pad_q((q.astype(f32) * A_DH ** -0.5).astype(bf16)).reshape(B, SQ_PAD, A_KV, A_HPG, A_DH)
    qs = qs.transpose(0, 2, 3, 1, 4).reshape(B, A_KV, A_HPG * SQ_PAD, A_DH)
    gl = pad_q(gate_logits).reshape(B, SQ_PAD, A_KV, A_HPG, 3).transpose(0, 2, 3, 1, 4)
    gl = gl.reshape(B, A_KV, A_HPG * SQ_PAD, 3)
    new = lambda a: pad_q(a.reshape(B, S, kvw))
    pe_k, w1_k, b1_k, w2_k, b2_k = cmp_k
    pe_v, w1_v, b1_v, w2_v, b2_v = cmp_v
    row = lambda a: a.reshape(1, -1).astype(f32)
    n_keys = past + SEL_TAIL
    lane = jnp.arange(P_NKEYS, dtype=jnp.int32)[:, None]
    key_blk = (jnp.arange(n_keys, dtype=jnp.int32) // L_SEL)[None, :]
    expand = ((lane % 2 == 0) & (lane // 2 == key_blk)).astype(bf16)
    chunk_rows = CMP_CHUNK * L_CMP
    dst = jnp.arange(chunk_rows, dtype=jnp.int32)
    src_row = (dst % CMP_CHUNK) * L_CMP + dst // CMP_CHUNK
    perm = (src_row[:, None] == jnp.arange(chunk_rows, dtype=jnp.int32)[None, :]).astype(bf16)
    tile_pe = lambda pe: jnp.tile(pe.astype(f32), (CMP_CHUNK, 1))

    hbm = pl.BlockSpec(memory_space=pl.ANY)
    seq3 = lambda r, c: pl.BlockSpec((1, r, c), lambda b, pt: (b, 0, 0))
    seq4 = lambda a, r, c: pl.BlockSpec((1, a, r, c), lambda b, pt: (b, 0, 0, 0))
    full = lambda a: pl.BlockSpec(a.shape, lambda b, pt: (0,) * a.ndim, pipeline_mode=pl.Buffered(1))
    consts = [tile_pe(pe_k), tile_pe(pe_v), perm, w1_k.astype(bf16), row(b1_k), w2_k.astype(bf16), row(b2_k),
              w1_v.astype(bf16), row(b1_v), w2_v.astype(bf16), row(b2_v), row(g_kc), expand]
    slots = 2
    out = pl.pallas_call(
        functools.partial(_nsa_sample_kernel, past=past, n_pages=n_pages, n_new=S),
        grid_spec=pltpu.PrefetchScalarGridSpec(
            num_scalar_prefetch=1,
            grid=(B,),
            in_specs=[hbm] * 6
                     + [seq4(A_KV, A_HPG * SQ_PAD, A_DH), seq4(A_KV, A_HPG * SQ_PAD, 3)]
                     + [seq3(SQ_PAD, kvw)] * 4
                     + [full(c) for c in consts],
            out_specs=pl.BlockSpec((1, SQ_PAD, A_W), lambda b, pt: (b, 0, 0)),
            scratch_shapes=[pltpu.VMEM((slots, A_KV, past, A_DH), f32), pltpu.VMEM((slots, A_KV, past, A_DH), f32),
                            pltpu.VMEM((slots, A_KV, n_keys, A_DH), f32), pltpu.VMEM((slots, A_KV, n_keys, A_DH), f32),
                            pltpu.VMEM((slots, A_KV, wb + SEL_TAIL, A_DH), f32),
                            pltpu.VMEM((slots, A_KV, wb + SEL_TAIL, A_DH), f32),
                            pltpu.SemaphoreType.DMA((slots,))]),
        out_shape=jax.ShapeDtypeStruct((B, SQ_PAD, A_W), f32),
        compiler_params=pltpu.CompilerParams(
            dimension_semantics=("arbitrary",), vmem_limit_bytes=VMEM_LIMIT_BYTES),
        name="nsa_sample",
    )(page_table, pool_kc, pool_vc, pool_ks, pool_vs, buf_kw, buf_vw, qs, gl,
      new(ksn), new(vsn), new(kwn), new(vwn), *consts)
    return out[:, :S]


def merge_groups(ret_o, rg, nsa_o, r_gn, w_out):
    B, T = rg.shape[:2]
    r = rmsnorm(ret_o, r_gn).reshape(B, T, R_W) * jax.nn.silu(rg.astype(jnp.float32))
    mix = jnp.concatenate([r.astype(rg.dtype), nsa_o.reshape(B, T, A_W).astype(rg.dtype)], axis=-1)
    return mmnd(mix, w_out)


def memory_kv(mem, g_memtok, w_mk, w_mv, m_kn):
    B, M, _ = mem.shape
    m = rmsnorm(mem, g_memtok)
    k = rmsnorm(mmnd(m, w_mk).reshape(B, M, M_HEADS, M_DH), m_kn)
    v = mmnd(m, w_mv).reshape(B, M, M_HEADS, M_DH)
    return k, v


def memory_attend(h, mk, mv, w_mq, m_qn, w_mo):
    B, T, _ = h.shape
    q = rmsnorm(mmnd(h, w_mq).reshape(B, T, M_HEADS, M_DH), m_qn)
    s = jnp.einsum('bthd,bmhd->bhtm', q.astype(jnp.float32), mk.astype(jnp.float32)) * M_DH ** -0.5
    p = jax.nn.softmax(s, axis=-1)
    o = jnp.einsum('bhtm,bmhd->bthd', p, mv.astype(jnp.float32)).astype(h.dtype)
    return mmnd(o.reshape(B, T, M_W), w_mo)


def kernel(x_prompt, x_sample, mem_prompt, cache_k_cmp, cache_v_cmp, cache_k_sel, cache_v_sel, state_k_win, state_v_win, state_ret, cache_mem_k, cache_mem_v, page_table, g_mix, w_in, r_gn, a_qn, a_kcn, a_ksn, a_kwn, cmp_pe_k, cmp_w1_k, cmp_b1_k, cmp_w2_k, cmp_b2_k, cmp_pe_v, cmp_w1_v, cmp_b1_v, cmp_w2_v, cmp_b2_v, w_out, g_xattn, g_memtok, w_mq, w_mk, w_mv, m_qn, m_kn, w_mo, g_ffn, w_pq, sub_k1, sub_k2, exp_u, exp_v):
    cmp_k = (cmp_pe_k, cmp_w1_k, cmp_b1_k, cmp_w2_k, cmp_b2_k)
    cmp_v = (cmp_pe_v, cmp_w1_v, cmp_b1_v, cmp_w2_v, cmp_b2_v)
    xp, xs = x_prompt, x_sample

    T = xp.shape[1]
    w_in = w_in.astype(jnp.bfloat16)
    w_out = w_out.astype(jnp.bfloat16)
    assert xp.shape[0] == 1
    proj_p = mm(rmsnorm(xp, g_mix).astype(jnp.bfloat16)[0], w_in)
    ret_mix, ret_p = retention_prompt_pallas(proj_p, r_gn)
    ret_p = ret_p[None]
    cuts = np.cumsum(IN_SIZES)[3:-1].tolist()
    aq, akc, avc, aks, avs, akw, avw, ag_p = jnp.split(proj_p[None, :, cuts[0]:], [c - cuts[0] for c in cuts[1:]], axis=-1)
    kv = lambda a: a.reshape(1, T, A_KV, A_DH)
    aq, akc, avc, aks, avs, akw, avw = aq.reshape(1, T, A_HEADS, A_DH), kv(akc), kv(avc), kv(aks), kv(avs), kv(akw), kv(avw)
    aq, aks, akw = rmsnorm(aq, a_qn), rmsnorm(aks, a_ksn), rmsnorm(akw, a_kwn)
    kc = rmsnorm(compress(akc, *cmp_k), a_kcn)
    vc = compress(avc, *cmp_v)
    nsa_o = nsa_prompt_pallas(aq[0], kc[0], vc[0], aks[0], avs[0], akw[0], avw[0], ag_p[0])
    hp = xp + mm(jnp.concatenate([ret_mix, nsa_o], axis=-1).astype(jnp.bfloat16), w_out)[None]
    mem_k_p, mem_v_p = memory_kv(mem_prompt, g_memtok, w_mk, w_mv, m_kn)
    hp = hp + memory_attend(rmsnorm(hp, g_xattn), mem_k_p, mem_v_p, w_mq, m_qn, w_mo)
    wl = min(WINDOW, T)
    k_cmp_p, v_cmp_p, k_sel_p, v_sel_p = akc, avc, aks, avs
    k_win_p, v_win_p = akw[:, T - wl:], avw[:, T - wl:]
    ret_p = ret_p.astype(xp.dtype)

    S = xs.shape[1]
    past = page_table.shape[1] * PAGE_SIZE
    rq, rk, rv, rg, aq, akc, avc, aks, avs, akw, avw, ag, ag_s = split_mix(
        rmsnorm(xs, g_mix).astype(jnp.bfloat16), w_in)
    pos = past + jnp.arange(S)
    ret_s, ret_o = retention_chunk(state_ret.astype(jnp.float32), rotary(rq, pos) * R_DK ** -0.5,
                                   rotary(rk, pos), rv)
    aq, aks, akw = rmsnorm(aq, a_qn), rmsnorm(aks, a_ksn), rmsnorm(akw, a_kwn)
    nsa_o = nsa_sample_pallas(aq, aks, avs, akw, avw, ag_s, cache_k_cmp, cache_v_cmp, cache_k_sel,
                              cache_v_sel, state_k_win, state_v_win, page_table, cmp_k, cmp_v, a_kcn)
    hs = xs + merge_groups(ret_o, rg, nsa_o, r_gn, w_out)
    hs = hs + memory_attend(rmsnorm(hs, g_xattn), cache_mem_k, cache_mem_v, w_mq, m_qn, w_mo)
    n_p = hp.shape[0] * hp.shape[1]
    tokens = jnp.concatenate([rmsnorm(hp, g_ffn).reshape(-1, D_MODEL), rmsnorm(hs, g_ffn).reshape(-1, D_MODEL)], axis=0)
    y = peer_pallas(tokens, w_pq, sub_k1, sub_k2, exp_u.astype(jnp.bfloat16), exp_v.astype(jnp.bfloat16))
    xp = hp + y[:n_p].reshape(hp.shape)
    xs = hs + y[n_p:].reshape(hs.shape)
    wb = state_k_win.shape[1]
    k_cmp_s, v_cmp_s, k_sel_s, v_sel_s = akc, avc, aks, avs
    k_win_s = jnp.concatenate([state_k_win, akw.astype(state_k_win.dtype)], axis=1)[:, -wb:]
    v_win_s = jnp.concatenate([state_v_win, avw.astype(state_v_win.dtype)], axis=1)[:, -wb:]
    ret_s = ret_s.astype(state_ret.dtype)
    return (xp, xs, k_cmp_p, v_cmp_p, k_sel_p, v_sel_p, k_win_p, v_win_p, ret_p, mem_k_p, mem_v_p,
            k_cmp_s, v_cmp_s, k_sel_s, v_sel_s, k_win_s, v_win_s, ret_s)
```

```python
import functools

import jax
import jax.numpy as jnp
import numpy as np
from jax import lax
from jax.experimental import pallas as pl
from jax.experimental.pallas import tpu as pltpu

D_MODEL = 4096
PAGE_SIZE = 128
R_HEADS = 8
R_DK = 256
R_DV = 256
R_CHUNK = 128
ROPE_BASE = 10000.0
A_HEADS = 16
A_KV = 4
A_DH = 128
A_HPG = A_HEADS // A_KV
L_CMP = 32
CMP_HID = 256
L_SEL = 64
N_SEL = 16
WINDOW = 512
NSA_QB = 256
FORCE_SCORE = 1e4
NEG = -1e30
M_HEADS = 4
M_DH = 128
M_W = M_HEADS * M_DH
P_HEADS = 8
P_NKEYS = 128
P_DQ = 256
P_TOPK = 16
P_BLOCK = 128
EPS = 1e-6
R_W = R_HEADS * R_DV
A_W = A_HEADS * A_DH
IN_SIZES = (R_HEADS * R_DK, R_HEADS * R_DK, R_W, R_W, A_W) + (A_KV * A_DH,) * 6 + (A_HEADS * 3,)

VMEM_LIMIT_BYTES = 56 * 1024 * 1024


def _mm_kernel(a_ref, b_ref, o_ref, acc_ref):
    k = pl.program_id(2)

    @pl.when(k == 0)
    def _():
        acc_ref[...] = jnp.zeros_like(acc_ref)

    acc_ref[...] += jnp.dot(a_ref[...].astype(jnp.bfloat16), b_ref[...].astype(jnp.bfloat16),
                            preferred_element_type=jnp.float32)

    @pl.when(k == pl.num_programs(2) - 1)
    def _():
        o_ref[...] = acc_ref[...]


def _mm_fullk_kernel(a_ref, b_ref, o_ref):
    o_ref[...] = jnp.dot(a_ref[...].astype(jnp.bfloat16), b_ref[...].astype(jnp.bfloat16),
                         preferred_element_type=jnp.float32)


MM_FULLK_TM = 1024
MM_FULLK_TN = 512


def mm(a, b, tm=1024, tn=1024, tk=1024):
    m, kdim = a.shape
    _, n = b.shape
    if (a.dtype == jnp.bfloat16 and b.dtype == jnp.bfloat16 and m % MM_FULLK_TM == 0
            and n >= MM_FULLK_TN and kdim <= 4096):
        return pl.pallas_call(
            _mm_fullk_kernel,
            grid=(m // MM_FULLK_TM, pl.cdiv(n, MM_FULLK_TN)),
            in_specs=[pl.BlockSpec((MM_FULLK_TM, kdim), lambda i, j: (i, 0)),
                      pl.BlockSpec((kdim, MM_FULLK_TN), lambda i, j: (0, j))],
            out_specs=pl.BlockSpec((MM_FULLK_TM, MM_FULLK_TN), lambda i, j: (i, j)),
            out_shape=jax.ShapeDtypeStruct((m, n), jnp.float32),
            compiler_params=pltpu.CompilerParams(
                dimension_semantics=("parallel", "parallel"), vmem_limit_bytes=VMEM_LIMIT_BYTES),
            name="mm_fullk",
        )(a, b)
    tm = min(tm, m)
    tn = min(tn, n)
    tk = min(tk, kdim)
    assert m % tm == 0 and kdim % tk == 0
    grid = (m // tm, pl.cdiv(n, tn), kdim // tk)
    return pl.pallas_call(
        _mm_kernel,
        grid=grid,
        in_specs=[pl.BlockSpec((tm, tk), lambda i, j, k: (i, k)),
                  pl.BlockSpec((tk, tn), lambda i, j, k: (k, j))],
        out_specs=pl.BlockSpec((tm, tn), lambda i, j, k: (i, j)),
        out_shape=jax.ShapeDtypeStruct((m, n), jnp.float32),
        scratch_shapes=[pltpu.VMEM((tm, tn), jnp.float32)],
        compiler_params=pltpu.CompilerParams(
            dimension_semantics=("parallel", "parallel", "arbitrary"),
            vmem_limit_bytes=VMEM_LIMIT_BYTES),
        name="mm",
    )(a, b)


def mmnd(a, b):
    lead = a.shape[:-1]
    out = mm(a.reshape(-1, a.shape[-1]), b)
    return out.reshape(lead + (b.shape[-1],))


def _mm_nt_kernel(a_ref, b_ref, o_ref):
    o_ref[...] = lax.dot_general(a_ref[...].astype(jnp.bfloat16), b_ref[...].astype(jnp.bfloat16),
                                 (((1,), (1,)), ((), ())), preferred_element_type=jnp.float32)


def mm_nt(a, b, tm=512, tn=512):
    m, kdim = a.shape
    n, _ = b.shape
    tm, tn = min(tm, m), min(tn, n)
    assert m % tm == 0 and n % tn == 0
    return pl.pallas_call(
        _mm_nt_kernel,
        grid=(m // tm, n // tn),
        in_specs=[pl.BlockSpec((tm, kdim), lambda i, j: (i, 0)),
                  pl.BlockSpec((tn, kdim), lambda i, j: (j, 0))],
        out_specs=pl.BlockSpec((tm, tn), lambda i, j: (i, j)),
        out_shape=jax.ShapeDtypeStruct((m, n), jnp.float32),
        compiler_params=pltpu.CompilerParams(
            dimension_semantics=("parallel", "parallel"),
            vmem_limit_bytes=VMEM_LIMIT_BYTES),
        name="mm_nt",
    )(a, b)


PEER_TR = 256
PEER_TM = 512
PEER_TE = 512
PEER_SLABS = PEER_TE // P_NKEYS
TAKEN = -3.0e38


def _extract_top(s, n):
    rows = s.shape[0]
    row_id = lax.broadcasted_iota(jnp.int32, s.shape, 0).astype(jnp.float32)
    rank = jnp.full(s.shape, float(n), jnp.float32)
    vals = []
    for k in range(n):
        m = jnp.max(s, axis=0, keepdims=True)
        first = jnp.min(jnp.where(s == m, row_id, float(rows)), axis=0, keepdims=True)
        hit = row_id == first
        rank = jnp.where(hit, float(k), rank)
        s = jnp.where(hit, TAKEN, s)
        vals.append(m)
    return rank, jnp.concatenate(vals, axis=0)


def _peer_route_kernel(qt_ref, k1_ref, k2_ref, r2_ref, e2_ref, n_ref, e1n_ref):
    f32, bf16 = jnp.float32, jnp.bfloat16
    half = P_DQ // 2
    qt = qt_ref[...].astype(bf16)
    s1 = jnp.dot(k1_ref[0].astype(bf16), qt[:half], preferred_element_type=f32)
    s2 = jnp.dot(k2_ref[0].astype(bf16), qt[half:], preferred_element_type=f32)
    rank1, v1 = _extract_top(s1, P_TOPK)
    rank2, v2 = _extract_top(s2, P_TOPK)
    r1_id = lax.broadcasted_iota(jnp.int32, v1.shape, 0).astype(f32)
    cnt = jnp.zeros_like(v1)
    for _ in range(P_TOPK):
        nxt = jnp.full_like(v1, TAKEN)
        for j in range(P_TOPK):
            nxt = jnp.where(cnt == float(j), v2[j:j + 1], nxt)
        front = jnp.where(cnt < float(P_TOPK), v1 + nxt, TAKEN)
        m = jnp.max(front, axis=0, keepdims=True)
        first = jnp.min(jnp.where(front == m, r1_id, float(P_TOPK)), axis=0, keepdims=True)
        cnt = cnt + jnp.where(r1_id == first, 1.0, 0.0)
    ev1 = jnp.exp(v1 - v1[0:1])
    ev2 = jnp.exp(v2 - v2[0:1])
    inner = jnp.zeros_like(v1)
    for j in range(P_TOPK):
        inner = inner + jnp.where(cnt > float(j), ev2[j:j + 1], 0.0)
    z = jnp.sum(ev1 * inner, axis=0, keepdims=True)
    n_dense = jnp.zeros_like(s1)
    for r in range(P_TOPK):
        n_dense = n_dense + jnp.where(rank1 == float(r), cnt[r:r + 1], 0.0)
    r2_ref[0] = rank2
    e2_ref[0] = jnp.exp(s2 - v2[0:1])
    n_ref[0] = n_dense
    e1n_ref[0] = jnp.exp(s1 - v1[0:1]) * (1.0 / z)


def _peer_mix_kernel(x_ref, u_ref, v_ref, r2_ref, e2_ref, n_ref, e1n_ref, y_ref, a_scr):
    f32, bf16 = jnp.float32, jnp.bfloat16
    j = pl.program_id(1)

    @pl.when(j == 0)
    def _():
        y_ref[...] = jnp.zeros_like(y_ref)
        a_scr[1] = jnp.zeros(a_scr.shape[1:], f32)

    act = jax.nn.gelu(a_scr[(j + 1) % 2])
    a_scr[j % 2] = lax.dot_general(u_ref[...], x_ref[...], (((1,), (1,)), ((), ())),
                                   preferred_element_type=f32)

    tile = jnp.maximum(j - 1, 0)
    parts = []
    for s in range(PEER_SLABS):
        first_key = tile * PEER_SLABS + s
        g = jnp.zeros((P_NKEYS, act.shape[1]), f32)
        for h in range(P_HEADS):
            n_row = n_ref[h, pl.ds(first_key, 1), :]
            w_row = e1n_ref[h, pl.ds(first_key, 1), :]
            g = g + jnp.where(r2_ref[h] < n_row, e2_ref[h] * w_row, 0.0)
        parts.append((g * act[s * P_NKEYS:(s + 1) * P_NKEYS]).astype(bf16))
    w_t = jnp.concatenate(parts, axis=0)
    y_ref[...] += lax.dot_general(w_t, v_ref[...], (((0,), (0,)), ((), ())),
                                  preferred_element_type=f32)


def peer_pallas(x, w_pq, sub_k1, sub_k2, exp_u_b, exp_v_b):
    bf16 = jnp.bfloat16
    n, d = x.shape
    n_exp = exp_u_b.shape[0]
    assert n % PEER_TM == 0 and n % PEER_TR == 0 and n_exp % PEER_TE == 0
    xb = x.astype(bf16)
    qt = mm_nt(jnp.swapaxes(w_pq, 0, 1), xb)
    route_shape = jax.ShapeDtypeStruct((P_HEADS, P_NKEYS, n), jnp.float32)
    key_spec = pl.BlockSpec((1, P_NKEYS, P_DQ // 2), lambda i, h: (h, 0, 0))
    route_out = pl.BlockSpec((1, P_NKEYS, PEER_TR), lambda i, h: (h, 0, i))
    r2, e2, nd, e1n = pl.pallas_call(
        _peer_route_kernel,
        grid=(n // PEER_TR, P_HEADS),
        in_specs=[pl.BlockSpec((P_DQ, PEER_TR), lambda i, h: (h, i)), key_spec, key_spec],
        out_specs=[route_out] * 4,
        out_shape=[route_shape] * 4,
        compiler_params=pltpu.CompilerParams(
            dimension_semantics=("parallel", "parallel"), vmem_limit_bytes=VMEM_LIMIT_BYTES),
        name="peer_route",
    )(qt, sub_k1, sub_k2)
    route_in = pl.BlockSpec((P_HEADS, P_NKEYS, PEER_TM), lambda i, j: (0, 0, i),
                            pipeline_mode=pl.Buffered(1))
    n_tiles = n_exp // PEER_TE
    return pl.pallas_call(
        _peer_mix_kernel,
        grid=(n // PEER_TM, n_tiles + 1),
        in_specs=[pl.BlockSpec((PEER_TM, d), lambda i, j: (i, 0), pipeline_mode=pl.Buffered(1)),
                  pl.BlockSpec((PEER_TE, d), lambda i, j: (jnp.minimum(j, n_tiles - 1), 0)),
                  pl.BlockSpec((PEER_TE, d), lambda i, j: (jnp.maximum(j - 1, 0), 0)),
                  route_in, route_in, route_in, route_in],
        out_specs=pl.BlockSpec((PEER_TM, d), lambda i, j: (i, 0)),
        out_shape=jax.ShapeDtypeStruct((n, d), jnp.float32),
        scratch_shapes=[pltpu.VMEM((2, PEER_TE, PEER_TM), jnp.float32)],
        compiler_params=pltpu.CompilerParams(
            dimension_semantics=("parallel", "arbitrary"), vmem_limit_bytes=VMEM_LIMIT_BYTES),
        name="peer_mix",
    )(xb, exp_u_b, exp_v_b, r2, e2, nd, e1n)


def _retention_prompt_kernel(q_ref, k_ref, v_ref, gate_ref, cos_ref, sin_ref, dmask_ref, dq_ref, dk_ref,
                             cdec_ref, gn_ref, o_ref, state_ref, s_scr):
    f32, bf16 = jnp.float32, jnp.bfloat16
    c = pl.program_id(1)

    @pl.when(c == 0)
    def _():
        s_scr[...] = jnp.zeros_like(s_scr)

    cos, sin = cos_ref[...], sin_ref[...]
    half = R_DK // 2

    def rot(x):
        x1, x2 = x[:, :half], x[:, half:]
        return jnp.concatenate([x1 * cos - x2 * sin, x1 * sin + x2 * cos], axis=1)

    q = (rot(q_ref[...]) * R_DK ** -0.5).astype(bf16)
    k = rot(k_ref[...])
    v = v_ref[...].astype(bf16)
    s_old = s_scr[...]
    inner = lax.dot_general(q, k.astype(bf16), (((1,), (1,)), ((), ())),
                            preferred_element_type=f32) * dmask_ref[0]
    o = jnp.dot(inner.astype(bf16), v, preferred_element_type=f32)
    o = o + jnp.dot(q, s_old.astype(bf16), preferred_element_type=f32) * dq_ref[0]
    kdec = (k * dk_ref[0]).astype(bf16)
    s_new = s_old * cdec_ref[0] + lax.dot_general(kdec, v, (((0,), (0,)), ((), ())),
                                                  preferred_element_type=f32)
    s_scr[...] = s_new
    y = o * lax.rsqrt(jnp.mean(o * o, axis=-1, keepdims=True) + EPS) * gn_ref[0]
    gate = gate_ref[...]
    o_ref[...] = y * (gate * (1.0 / (1.0 + jnp.exp(-gate))))

    @pl.when(c == pl.num_programs(1) - 1)
    def _():
        state_ref[0] = s_new


def retention_prompt_pallas(proj, r_gn):
    f32 = jnp.float32
    T = proj.shape[0]
    assert T % R_CHUNK == 0 and R_DK == R_DV
    half = R_DK // 2
    inv = ROPE_BASE ** (-jnp.arange(half, dtype=f32) / half)
    ang = jnp.arange(T, dtype=f32)[:, None] * inv[None, :]
    lg = jnp.log1p(-jnp.exp2(-5.0 - jnp.arange(R_HEADS, dtype=f32)))
    idx = jnp.arange(R_CHUNK, dtype=f32)
    rel = idx[:, None] - idx[None, :]
    dmask = jnp.where(rel[None] >= 0, jnp.exp(jnp.maximum(rel, 0.0)[None] * lg[:, None, None]), 0.0)
    dq = jnp.exp((idx + 1.0)[None, :] * lg[:, None])[:, :, None]
    dk = jnp.exp((R_CHUNK - 1.0 - idx)[None, :] * lg[:, None])[:, :, None]
    cdec = jnp.broadcast_to(jnp.exp(R_CHUNK * lg)[:, None, None], (R_HEADS, 1, R_DV))
    col = lambda base: pl.BlockSpec((R_CHUNK, R_DK), lambda h, c, base=base: (c, base + h))
    per_head = lambda shape: pl.BlockSpec((1,) + shape, lambda h, c: (h, 0, 0))
    trig = pl.BlockSpec((R_CHUNK, half), lambda h, c: (c, 0))
    return pl.pallas_call(
        _retention_prompt_kernel,
        grid=(R_HEADS, T // R_CHUNK),
        in_specs=[col(0), col(R_HEADS), col(2 * R_HEADS), col(3 * R_HEADS), trig, trig,
                  per_head((R_CHUNK, R_CHUNK)), per_head((R_CHUNK, 1)), per_head((R_CHUNK, 1)),
                  per_head((1, R_DV)), per_head((1, R_DV))],
        out_specs=[pl.BlockSpec((R_CHUNK, R_DV), lambda h, c: (c, h)),
                   pl.BlockSpec((1, R_DK, R_DV), lambda h, c: (h, 0, 0))],
        out_shape=[jax.ShapeDtypeStruct((T, R_W), f32),
                   jax.ShapeDtypeStruct((R_HEADS, R_DK, R_DV), f32)],
        scratch_shapes=[pltpu.VMEM((R_DK, R_DV), f32)],
        compiler_params=pltpu.CompilerParams(
            dimension_semantics=("parallel", "arbitrary"), vmem_limit_bytes=VMEM_LIMIT_BYTES),
        name="retention_prompt",
    )(proj, proj, proj, proj, jnp.cos(ang), jnp.sin(ang), dmask, dq, dk, cdec,
      r_gn.astype(f32).reshape(R_HEADS, 1, R_DV))


def rmsnorm(x, g):
    xf = x.astype(jnp.float32)
    y = xf * lax.rsqrt(jnp.mean(xf * xf, axis=-1, keepdims=True) + EPS)
    return (y * g.astype(jnp.float32)).astype(x.dtype)


def rotary(x, pos):
    half = x.shape[-1] // 2
    inv = ROPE_BASE ** (-jnp.arange(half, dtype=jnp.float32) / half)
    ang = pos.astype(jnp.float32)[:, None] * inv[None, :]
    cos, sin = jnp.cos(ang)[:, None, :], jnp.sin(ang)[:, None, :]
    xf = x.astype(jnp.float32)
    x1, x2 = xf[..., :half], xf[..., half:]
    return jnp.concatenate([x1 * cos - x2 * sin, x1 * sin + x2 * cos], axis=-1).astype(x.dtype)


def masked_softmax(s, mask):
    p = jax.nn.softmax(jnp.where(mask, s, NEG), axis=-1)
    return jnp.where(mask, p, 0.0)


def split_mix(h, w_in):
    B, T, _ = h.shape
    cuts = np.cumsum(IN_SIZES)[:-1].tolist()
    rq, rk, rv, rg, aq, akc, avc, aks, avs, akw, avw, ag = jnp.split(mmnd(h, w_in), cuts, axis=-1)
    r = lambda a, n, d: a.reshape(B, T, n, d)
    return (r(rq, R_HEADS, R_DK), r(rk, R_HEADS, R_DK), r(rv, R_HEADS, R_DV), rg,
            r(aq, A_HEADS, A_DH), r(akc, A_KV, A_DH), r(avc, A_KV, A_DH), r(aks, A_KV, A_DH),
            r(avs, A_KV, A_DH), r(akw, A_KV, A_DH), r(avw, A_KV, A_DH),
            jax.nn.sigmoid(ag.astype(jnp.float32)).reshape(B, T, A_HEADS, 3), ag)


def retention_chunk(S, q, k, v):
    q, k, v = q.astype(jnp.float32), k.astype(jnp.float32), v.astype(jnp.float32)
    C = q.shape[1]
    lg = jnp.log1p(-jnp.exp2(-5.0 - jnp.arange(R_HEADS, dtype=jnp.float32)))
    idx = jnp.arange(C, dtype=jnp.float32)
    rel = idx[:, None] - idx[None, :]
    dmask = jnp.where(rel[None] >= 0, jnp.exp(jnp.maximum(rel, 0.0)[None] * lg[:, None, None]), 0.0)
    inner = jnp.einsum('bihd,bjhd->bhij', q, k) * dmask[None]
    o = jnp.einsum('bhij,bjhe->bihe', inner, v)
    o = o + jnp.einsum('bihd,bhde->bihe', q, S) * jnp.exp((idx + 1.0)[:, None] * lg[None, :])[None, :, :, None]
    kdec = k * jnp.exp((C - 1.0 - idx)[:, None] * lg[None, :])[None, :, :, None]
    S_new = S * jnp.exp(C * lg)[None, :, None, None] + jnp.einsum('bjhd,bjhe->bhde', kdec, v)
    return S_new, o


def compress(rows, pe, w1, b1, w2, b2):
    T = rows.shape[-3]
    nc = T // L_CMP
    lead = rows.shape[:-3]
    blk = rows[..., :nc * L_CMP, :, :].reshape(lead + (nc, L_CMP, A_KV, A_DH)) + pe[:, None, :]
    blk = jnp.swapaxes(blk, -3, -2).reshape(lead + (nc, A_KV, L_CMP * A_DH))
    return mmnd(jax.nn.gelu(mmnd(blk, w1) + b1), w2) + b2


NSA_TK = 512
N_TOPK_NEG = -3.0e38


def _softmax_rows(s, mask):
    s = jnp.where(mask, s, NEG)
    m = jnp.max(s, axis=-1, keepdims=True)
    e = jnp.where(mask, jnp.exp(s - m), 0.0)
    l = jnp.sum(e, axis=-1, keepdims=True)
    return e * (1.0 / jnp.maximum(l, 1e-30))


def _top_n_mask_cols(score_t, n):
    rows = score_t.shape[0]
    row_id = lax.broadcasted_iota(jnp.int32, score_t.shape, 0).astype(jnp.float32)

    def one(_, c):
        sc, sel = c
        m = jnp.max(sc, axis=0, keepdims=True)
        first = jnp.min(jnp.where(sc == m, row_id, float(rows)), axis=0, keepdims=True)
        hit = row_id == first
        return jnp.where(hit, N_TOPK_NEG, sc), jnp.where(hit, 1.0, sel)

    _, sel = lax.fori_loop(0, n, one, (score_t, jnp.zeros_like(score_t)), unroll=True)
    return sel


def _nsa_prompt_kernel(q_ref, gate_ref, kc_ref, vc_ref, ks_ref, vs_ref, kw_ref, vw_ref, e_ref, o_ref,
                       *, n_win_keys):
    f32, bf16 = jnp.float32, jnp.bfloat16
    qb = pl.program_id(1)
    start = qb * NSA_QB
    nt_dims = (((1,), (1,)), ((), ()))
    q = q_ref[...]
    qg = jnp.concatenate([q[:, h * A_DH:(h + 1) * A_DH] for h in range(A_HPG)], axis=0)
    q_pos = start + lax.broadcasted_iota(jnp.int32, (NSA_QB, 1), 0)
    rep = lambda a: jnp.concatenate([a] * A_HPG, axis=0)

    kc, vc = kc_ref[0], vc_ref[0]
    nc = kc.shape[0]
    nsb = nc // 2
    s_c = lax.dot_general(qg, kc, nt_dims, preferred_element_type=f32)
    lane_c = lax.broadcasted_iota(jnp.int32, (1, nc), 1)
    c_orig = jnp.where(lane_c < nsb, 2 * lane_c, 2 * (lane_c - nsb) + 1)
    cmask = ((c_orig + 1) * L_CMP - 1) <= q_pos
    p_c = _softmax_rows(s_c, rep(cmask))
    o_c = jnp.dot(p_c.astype(bf16), vc, preferred_element_type=f32)

    ph = p_c[0:NSA_QB]
    for h in range(1, A_HPG):
        ph = ph + p_c[h * NSA_QB:(h + 1) * NSA_QB]
    imp = ph[:, :nsb] + ph[:, nsb:]
    blk = lax.broadcasted_iota(jnp.int32, (1, nsb), 1)
    cur = q_pos // L_SEL
    forced = (blk == 0) | (blk == cur) | (blk == cur - 1)
    imp = jnp.where(blk <= cur, jnp.where(forced, FORCE_SCORE, imp), NEG)
    sel = _top_n_mask_cols(imp.T, min(N_SEL, nsb)).T
    sel_b = sel.astype(bf16)

    n_tiles = (start + NSA_QB + NSA_TK - 1) // NSA_TK

    def sel_step(j, carry):
        m, l, acc = carry
        off = pl.multiple_of(j * NSA_TK, NSA_TK)
        k = ks_ref[0, pl.ds(off, NSA_TK), :]
        v = vs_ref[0, pl.ds(off, NSA_TK), :]
        s = lax.dot_general(qg, k, nt_dims, preferred_element_type=f32)
        picked = jnp.dot(sel_b, e_ref[j], preferred_element_type=f32)
        k_pos = off + lax.broadcasted_iota(jnp.int32, (1, NSA_TK), 1)
        bias = jnp.where((picked > 0.5) & (k_pos <= q_pos), 0.0, NEG)
        s = s + rep(bias)
        m_new = jnp.maximum(m, jnp.max(s, axis=-1, keepdims=True))
        alpha = jnp.exp(m - m_new)
        p = jnp.exp(s - m_new)
        l = alpha * l + jnp.sum(p, axis=-1, keepdims=True)
        acc = alpha * acc + jnp.dot(p.astype(bf16), v, preferred_element_type=f32)
        return m_new, l, acc

    def pair_step(p, carry):
        return sel_step(2 * p + 1, sel_step(2 * p, carry))

    rows = A_HPG * NSA_QB
    m_s, l_s, acc_s = lax.fori_loop(
        0, (n_tiles + 1) // 2, pair_step,
        (jnp.full((rows, 1), NEG, f32), jnp.zeros((rows, 1), f32), jnp.zeros((rows, A_DH), f32)))
    o_s = acc_s * (1.0 / l_s)

    base = pl.multiple_of(jnp.maximum(start + NSA_QB - n_win_keys, 0), NSA_QB)
    kw = kw_ref[0, pl.ds(base, n_win_keys), :]
    vw = vw_ref[0, pl.ds(base, n_win_keys), :]
    s_w = lax.dot_general(qg, kw, nt_dims, preferred_element_type=f32)
    rel = q_pos - (base + lax.broadcasted_iota(jnp.int32, (1, n_win_keys), 1))
    p_w = _softmax_rows(s_w, rep((rel >= 0) & (rel < WINDOW)))
    o_w = jnp.dot(p_w.astype(bf16), vw, preferred_element_type=f32)

    gl = gate_ref[0]
    g = 1.0 / (1.0 + jnp.exp(-gl))
    outs = []
    for h in range(A_HPG):
        r = slice(h * NSA_QB, (h + 1) * NSA_QB)
        outs.append(o_c[r] * g[:, 3 * h:3 * h + 1] + o_s[r] * g[:, 3 * h + 1:3 * h + 2]
                    + o_w[r] * g[:, 3 * h + 2:3 * h + 3])
    o_ref[...] = jnp.concatenate(outs, axis=-1)


def nsa_prompt_pallas(q, kc, vc, ks, vs, kw, vw, gate_logits):
    bf16 = jnp.bfloat16
    T = q.shape[0]
    nc = kc.shape[0]
    nsb = T // L_SEL
    assert T % (2 * NSA_TK) == 0 and nc == 2 * nsb
    n_win_keys = min(WINDOW + NSA_QB, T)
    qs = (q.astype(jnp.float32) * A_DH ** -0.5).astype(bf16).reshape(T, A_W)
    grp = lambda a: jnp.swapaxes(a, 0, 1).astype(bf16)
    perm = lambda a: jnp.concatenate([a[0::2], a[1::2]], axis=0)
    gl = jnp.swapaxes(gate_logits.reshape(T, A_KV, A_HPG * 3), 0, 1)
    n_t = T // NSA_TK
    key_blk = (jnp.arange(T, dtype=jnp.int32) // L_SEL).reshape(n_t, 1, NSA_TK)
    expand = (jnp.arange(nsb, dtype=jnp.int32)[None, :, None] == key_blk).astype(bf16)
    kv_spec = lambda rows: pl.BlockSpec((1, rows, A_DH), lambda g, i: (g, 0, 0))
    return pl.pallas_call(
        functools.partial(_nsa_prompt_kernel, n_win_keys=n_win_keys),
        grid=(A_KV, T // NSA_QB),
        in_specs=[pl.BlockSpec((NSA_QB, A_HPG * A_DH), lambda g, i: (i, g)),
                  pl.BlockSpec((1, NSA_QB, A_HPG * 3), lambda g, i: (g, i, 0)),
                  kv_spec(nc), kv_spec(nc), kv_spec(T), kv_spec(T), kv_spec(T), kv_spec(T),
                  pl.BlockSpec((n_t, nsb, NSA_TK), lambda g, i: (0, 0, 0))],
        out_specs=pl.BlockSpec((NSA_QB, A_HPG * A_DH), lambda g, i: (i, g)),
        out_shape=jax.ShapeDtypeStruct((T, A_W), jnp.float32),
        compiler_params=pltpu.CompilerParams(
            dimension_semantics=("parallel", "arbitrary"),
            vmem_limit_bytes=VMEM_LIMIT_BYTES),
        name="nsa_prompt",
    )(qs, gl, grp(perm(kc)), grp(perm(vc)), grp(ks), grp(vs), grp(kw), grp(vw), expand)


SQ_PAD = 8
SEL_TAIL = 128


CMP_CHUNK = 8


def _compress_seq(src, pe_ref, perm_ref, w1_ref, b1_ref, w2_ref, b2_ref, n_blk):
    f32, bf16 = jnp.float32, jnp.bfloat16
    chunk_rows = CMP_CHUNK * L_CMP
    regrouped = []
    chunk = lambda g, j: (src[g, j * chunk_rows:(j + 1) * chunk_rows, :] + pe_ref[...]).astype(bf16)
    for g in range(A_KV):
        for j in range(0, n_blk // CMP_CHUNK, 2):
            y = jnp.dot(perm_ref[...], jnp.concatenate([chunk(g, j), chunk(g, j + 1)], axis=1),
                        preferred_element_type=f32)
            regrouped.extend([y[:, :A_DH], y[:, A_DH:]])
    acc = jnp.zeros((A_KV * n_blk, CMP_HID), f32)
    row_l = lambda l: jnp.concatenate([y[l * CMP_CHUNK:(l + 1) * CMP_CHUNK] for y in regrouped], axis=0)
    for l in range(0, L_CMP, 2):
        xg = jnp.concatenate([row_l(l), row_l(l + 1)], axis=1).astype(bf16)
        acc = acc + jnp.dot(xg, w1_ref[l * A_DH:(l + 2) * A_DH, :], preferred_element_type=jnp.float32)
    hid = jax.nn.gelu(acc + b1_ref[...])
    return jnp.dot(hid.astype(bf16), w2_ref[...], preferred_element_type=jnp.float32) + b2_ref[...]


def _nsa_sample_kernel(pt_ref, kcp_ref, vcp_ref, ksp_ref, vsp_ref, kwb_ref, vwb_ref, q_ref, gate_ref,
                       ksn_ref, vsn_ref, kwn_ref, vwn_ref, pek_ref, pev_ref, perm_ref,
                       w1k_ref, b1k_ref, w2k_ref, b2k_ref, w1v_ref, b1v_ref, w2v_ref, b2v_ref, gkc_ref,
                       e_ref, o_ref, kc_scr, vc_scr, ks_scr, vs_scr, kw_scr, vw_scr, sem,
                       *, past, n_pages, n_new):
    f32, bf16 = jnp.float32, jnp.bfloat16
    b = pl.program_id(0)
    slot = b % 2

    def seq_copies(seq, dst_slot):
        cps = []
        for pg in range(n_pages):
            page = pt_ref[seq, pg]
            rows = pl.ds(pg * PAGE_SIZE, PAGE_SIZE)
            for g in range(A_KV):
                for pool_ref, scr in ((kcp_ref, kc_scr), (vcp_ref, vc_scr), (ksp_ref, ks_scr), (vsp_ref, vs_scr)):
                    cps.append(pltpu.make_async_copy(pool_ref.at[page, :, g, :], scr.at[dst_slot, g, rows, :],
                                                     sem.at[dst_slot]))
        for g in range(A_KV):
            for buf_ref, scr in ((kwb_ref, kw_scr), (vwb_ref, vw_scr)):
                cps.append(pltpu.make_async_copy(buf_ref.at[seq, :, g, :], scr.at[dst_slot, g, pl.ds(0, wb), :],
                                                 sem.at[dst_slot]))
        return cps

    wb = kwb_ref.shape[1]

    @pl.when(b == 0)
    def _():
        for cp in seq_copies(0, 0):
            cp.start()

    @pl.when(b + 1 < pl.num_programs(0))
    def _():
        for cp in seq_copies(b + 1, 1 - slot):
            cp.start()

    for cp in seq_copies(b, slot):
        cp.wait()

    def compute():
        nt_dims = (((1,), (1,)), ((), ()))
        n_blk = past // L_CMP
        half = n_blk // 2
        rows = A_HPG * SQ_PAD
        pad_rows = jnp.zeros((SEL_TAIL - SQ_PAD, A_DH), f32)
        tail = lambda new_ref, g: jnp.concatenate(
            [new_ref[0, :, g * A_DH:(g + 1) * A_DH], pad_rows], axis=0)
        for g in range(A_KV):
            ks_scr[slot, g, pl.ds(past, SEL_TAIL), :] = tail(ksn_ref, g)
            vs_scr[slot, g, pl.ds(past, SEL_TAIL), :] = tail(vsn_ref, g)
            kw_scr[slot, g, pl.ds(wb, SEL_TAIL), :] = tail(kwn_ref, g)
            vw_scr[slot, g, pl.ds(wb, SEL_TAIL), :] = tail(vwn_ref, g)
        kcmp = _compress_seq(kc_scr.at[slot], pek_ref, perm_ref, w1k_ref, b1k_ref, w2k_ref, b2k_ref, n_blk)
        kcmp = kcmp * lax.rsqrt(jnp.mean(kcmp * kcmp, axis=-1, keepdims=True) + EPS) * gkc_ref[...]
        vcmp = _compress_seq(vc_scr.at[slot], pev_ref, perm_ref, w1v_ref, b1v_ref, w2v_ref, b2v_ref, n_blk)

        q_pos = past + (lax.broadcasted_iota(jnp.int32, (SQ_PAD, 1), 0))
        rep = lambda a: jnp.concatenate([a] * A_HPG, axis=0)
        lanes = P_NKEYS
        zero_blk = jnp.zeros((lanes - n_blk, A_DH), f32)
        lane_c = lax.broadcasted_iota(jnp.int32, (1, lanes), 1)
        cmask = jnp.broadcast_to(lane_c < n_blk, (rows, lanes))

        o_c, imps = [], []
        for g in range(A_KV):
            qg = q_ref[0, g]
            kc_g = jnp.concatenate([kcmp[g * n_blk:(g + 1) * n_blk], zero_blk], axis=0).astype(bf16)
            vc_g = jnp.concatenate([vcmp[g * n_blk:(g + 1) * n_blk], zero_blk], axis=0).astype(bf16)
            s_c = lax.dot_general(qg, kc_g, nt_dims, preferred_element_type=f32)
            p_c = _softmax_rows(s_c, cmask)
            o_c.append(jnp.dot(p_c.astype(bf16), vc_g, preferred_element_type=f32))
            ph = p_c[0:SQ_PAD]
            for h in range(1, A_HPG):
                ph = ph + p_c[h * SQ_PAD:(h + 1) * SQ_PAD]
            imps.append(ph + pltpu.roll(ph, lanes - 1, axis=1))
        imp = jnp.concatenate(imps, axis=0)
        blk = lane_c // 2
        q_pos4 = jnp.concatenate([q_pos] * A_KV, axis=0)
        cur = q_pos4 // L_SEL
        forced = (blk == 0) | (blk == cur) | (blk == cur - 1)
        imp = jnp.where(blk <= cur, jnp.where(forced, FORCE_SCORE, imp), NEG)
        imp = jnp.where(lane_c % 2 == 0, imp, N_TOPK_NEG)
        nsb = -(-(past + n_new) // L_SEL)
        imp_sq = jnp.concatenate([imp, jnp.full((lanes - imp.shape[0], lanes), N_TOPK_NEG, f32)], axis=0)
        sel = _top_n_mask_cols(imp_sq.T, min(N_SEL, nsb)).T[:imp.shape[0]]
        picked = jnp.dot(sel.astype(bf16), e_ref[...], preferred_element_type=f32)
        n_keys = past + SEL_TAIL
        k_pos = lax.broadcasted_iota(jnp.int32, (1, n_keys), 1)
        w_pos = past - wb + lax.broadcasted_iota(jnp.int32, (1, wb + SEL_TAIL), 1)
        rel = q_pos - w_pos
        w_mask = rep((rel >= 0) & (rel < WINDOW) & (w_pos >= 0))

        gl = gate_ref[0]
        outs = []
        for g in range(A_KV):
            qg = q_ref[0, g]
            s_mask = rep((picked[g * SQ_PAD:(g + 1) * SQ_PAD] > 0.5) & (k_pos <= q_pos))
            s_s = lax.dot_general(qg, ks_scr[slot, g].astype(bf16), nt_dims, preferred_element_type=f32)
            p_s = _softmax_rows(s_s, s_mask)
            o_s = jnp.dot(p_s.astype(bf16), vs_scr[slot, g].astype(bf16), preferred_element_type=f32)
            s_w = lax.dot_general(qg, kw_scr[slot, g].astype(bf16), nt_dims, preferred_element_type=f32)
            p_w = _softmax_rows(s_w, w_mask)
            o_w = jnp.dot(p_w.astype(bf16), vw_scr[slot, g].astype(bf16), preferred_element_type=f32)
            gs = 1.0 / (1.0 + jnp.exp(-gl[g]))
            o = o_c[g] * gs[:, 0:1] + o_s * gs[:, 1:2] + o_w * gs[:, 2:3]
            outs.extend([o[h * SQ_PAD:(h + 1) * SQ_PAD] for h in range(A_HPG)])
        o_ref[0] = jnp.concatenate(outs, axis=1)

    compute()


def nsa_sample_pallas(q, ksn, vsn, kwn, vwn, gate_logits, pool_kc, pool_vc, pool_ks, pool_vs,
                      buf_kw, buf_vw, page_table, cmp_k, cmp_v, g_kc):
    f32, bf16 = jnp.float32, jnp.bfloat16
    B, S = q.shape[:2]
    assert S <= SQ_PAD and S < L_CMP
    n_pages = page_table.shape[1]
    past = n_pages * PAGE_SIZE
    wb = buf_kw.shape[1]
    kvw = A_KV * A_DH
    pad_q = lambda a: jnp.pad(a, ((0, 0), (0, SQ_PAD - S)) + ((0, 0),) * (a.ndim - 2))
    qs = pad_q((q.astype(f32) * A_DH ** -0.5).astype(bf16)).reshape(B, SQ_PAD, A_KV, A_HPG, A_DH)
    qs = qs.transpose(0, 2, 3, 1, 4).reshape(B, A_KV, A_HPG * SQ_PAD, A_DH)
    gl = pad_q(gate_logits).reshape(B, SQ_PAD, A_KV, A_HPG, 3).transpose(0, 2, 3, 1, 4)
    gl = gl.reshape(B, A_KV, A_HPG * SQ_PAD, 3)
    new = lambda a: pad_q(a.reshape(B, S, kvw))
    pe_k, w1_k, b1_k, w2_k, b2_k = cmp_k
    pe_v, w1_v, b1_v, w2_v, b2_v = cmp_v
    row = lambda a: a.reshape(1, -1).astype(f32)
    n_keys = past + SEL_TAIL
    lane = jnp.arange(P_NKEYS, dtype=jnp.int32)[:, None]
    key_blk = (jnp.arange(n_keys, dtype=jnp.int32) // L_SEL)[None, :]
    expand = ((lane % 2 == 0) & (lane // 2 == key_blk)).astype(bf16)
    chunk_rows = CMP_CHUNK * L_CMP
    dst = jnp.arange(chunk_rows, dtype=jnp.int32)
    src_row = (dst % CMP_CHUNK) * L_CMP + dst // CMP_CHUNK
    perm = (src_row[:, None] == jnp.arange(chunk_rows, dtype=jnp.int32)[None, :]).astype(bf16)
    tile_pe = lambda pe: jnp.tile(pe.astype(f32), (CMP_CHUNK, 1))

    hbm = pl.BlockSpec(memory_space=pl.ANY)
    seq3 = lambda r, c: pl.BlockSpec((1, r, c), lambda b, pt: (b, 0, 0))
    seq4 = lambda a, r, c: pl.BlockSpec((1, a, r, c), lambda b, pt: (b, 0, 0, 0))
    full = lambda a: pl.BlockSpec(a.shape, lambda b, pt: (0,) * a.ndim, pipeline_mode=pl.Buffered(1))
    consts = [tile_pe(pe_k), tile_pe(pe_v), perm, w1_k.astype(bf16), row(b1_k), w2_k.astype(bf16), row(b2_k),
              w1_v.astype(bf16), row(b1_v), w2_v.astype(bf16), row(b2_v), row(g_kc), expand]
    slots = 2
    out = pl.pallas_call(
        functools.partial(_nsa_sample_kernel, past=past, n_pages=n_pages, n_new=S),
        grid_spec=pltpu.PrefetchScalarGridSpec(
            num_scalar_prefetch=1,
            grid=(B,),
            in_specs=[hbm] * 6
                     + [seq4(A_KV, A_HPG * SQ_PAD, A_DH), seq4(A_KV, A_HPG * SQ_PAD, 3)]
                     + [seq3(SQ_PAD, kvw)] * 4
                     + [full(c) for c in consts],
            out_specs=pl.BlockSpec((1, SQ_PAD, A_W), lambda b, pt: (b, 0, 0)),
            scratch_shapes=[pltpu.VMEM((slots, A_KV, past, A_DH), f32), pltpu.VMEM((slots, A_KV, past, A_DH), f32),
                            pltpu.VMEM((slots, A_KV, n_keys, A_DH), f32), pltpu.VMEM((slots, A_KV, n_keys, A_DH), f32),
                            pltpu.VMEM((slots, A_KV, wb + SEL_TAIL, A_DH), f32),
                            pltpu.VMEM((slots, A_KV, wb + SEL_TAIL, A_DH), f32),
                            pltpu.SemaphoreType.DMA((slots,))]),
        out_shape=jax.ShapeDtypeStruct((B, SQ_PAD, A_W), f32),
        compiler_params=pltpu.CompilerParams(
            dimension_semantics=("arbitrary",), vmem_limit_bytes=VMEM_LIMIT_BYTES),
        name="nsa_sample",
    )(page_table, pool_kc, pool_vc, pool_ks, pool_vs, buf_kw, buf_vw, qs, gl,
      new(ksn), new(vsn), new(kwn), new(vwn), *consts)
    return out[:, :S]


def merge_groups(ret_o, rg, nsa_o, r_gn, w_out):
    B, T = rg.shape[:2]
    r = rmsnorm(ret_o, r_gn).reshape(B, T, R_W) * jax.nn.silu(rg.astype(jnp.float32))
    mix = jnp.concatenate([r.astype(rg.dtype), nsa_o.reshape(B, T, A_W).astype(rg.dtype)], axis=-1)
    return mmnd(mix, w_out)


def memory_kv(mem, g_memtok, w_mk, w_mv, m_kn):
    B, M, _ = mem.shape
    m = rmsnorm(mem, g_memtok)
    k = rmsnorm(mmnd(m, w_mk).reshape(B, M, M_HEADS, M_DH), m_kn)
    v = mmnd(m, w_mv).reshape(B, M, M_HEADS, M_DH)
    return k, v


def memory_attend(h, mk, mv, w_mq, m_qn, w_mo):
    B, T, _ = h.shape
    q = rmsnorm(mmnd(h, w_mq).reshape(B, T, M_HEADS, M_DH), m_qn)
    s = jnp.einsum('bthd,bmhd->bhtm', q.astype(jnp.float32), mk.astype(jnp.float32)) * M_DH ** -0.5
    p = jax.nn.softmax(s, axis=-1)
    o = jnp.einsum('bhtm,bmhd->bthd', p, mv.astype(jnp.float32)).astype(h.dtype)
    return mmnd(o.reshape(B, T, M_W), w_mo)


def kernel(x_prompt, x_sample, mem_prompt, cache_k_cmp, cache_v_cmp, cache_k_sel, cache_v_sel, state_k_win, state_v_win, state_ret, cache_mem_k, cache_mem_v, page_table, g_mix, w_in, r_gn, a_qn, a_kcn, a_ksn, a_kwn, cmp_pe_k, cmp_w1_k, cmp_b1_k, cmp_w2_k, cmp_b2_k, cmp_pe_v, cmp_w1_v, cmp_b1_v, cmp_w2_v, cmp_b2_v, w_out, g_xattn, g_memtok, w_mq, w_mk, w_mv, m_qn, m_kn, w_mo, g_ffn, w_pq, sub_k1, sub_k2, exp_u, exp_v):
    cmp_k = (cmp_pe_k, cmp_w1_k, cmp_b1_k, cmp_w2_k, cmp_b2_k)
    cmp_v = (cmp_pe_v, cmp_w1_v, cmp_b1_v, cmp_w2_v, cmp_b2_v)
    xp, xs = x_prompt, x_sample

    T = xp.shape[1]
    w_in = w_in.astype(jnp.bfloat16)
    w_out = w_out.astype(jnp.bfloat16)
    assert xp.shape[0] == 1
    proj_p = mm(rmsnorm(xp, g_mix).astype(jnp.bfloat16)[0], w_in)
    ret_mix, ret_p = retention_prompt_pallas(proj_p, r_gn)
    ret_p = ret_p[None]
    cuts = np.cumsum(IN_SIZES)[3:-1].tolist()
    aq, akc, avc, aks, avs, akw, avw, ag_p = jnp.split(proj_p[None, :, cuts[0]:], [c - cuts[0] for c in cuts[1:]], axis=-1)
    kv = lambda a: a.reshape(1, T, A_KV, A_DH)
    aq, akc, avc, aks, avs, akw, avw = aq.reshape(1, T, A_HEADS, A_DH), kv(akc), kv(avc), kv(aks), kv(avs), kv(akw), kv(avw)
    aq, aks, akw = rmsnorm(aq, a_qn), rmsnorm(aks, a_ksn), rmsnorm(akw, a_kwn)
    kc = rmsnorm(compress(akc, *cmp_k), a_kcn)
    vc = compress(avc, *cmp_v)
    nsa_o = nsa_prompt_pallas(aq[0], kc[0], vc[0], aks[0], avs[0], akw[0], avw[0], ag_p[0])
    hp = xp + mm(jnp.concatenate([ret_mix, nsa_o], axis=-1).astype(jnp.bfloat16), w_out)[None]
    mem_k_p, mem_v_p = memory_kv(mem_prompt, g_memtok, w_mk, w_mv, m_kn)
    hp = hp + memory_attend(rmsnorm(hp, g_xattn), mem_k_p, mem_v_p, w_mq, m_qn, w_mo)
    wl = min(WINDOW, T)
    k_cmp_p, v_cmp_p, k_sel_p, v_sel_p = akc, avc, aks, avs
    k_win_p, v_win_p = akw[:, T - wl:], avw[:, T - wl:]
    ret_p = ret_p.astype(xp.dtype)

    S = xs.shape[1]
    past = page_table.shape[1] * PAGE_SIZE
    rq, rk, rv, rg, aq, akc, avc, aks, avs, akw, avw, ag, ag_s = split_mix(
        rmsnorm(xs, g_mix).astype(jnp.bfloat16), w_in)
    pos = past + jnp.arange(S)
    ret_s, ret_o = retention_chunk(state_ret.astype(jnp.float32), rotary(rq, pos) * R_DK ** -0.5,
                                   rotary(rk, pos), rv)
    aq, aks, akw = rmsnorm(aq, a_qn), rmsnorm(aks, a_ksn), rmsnorm(akw, a_kwn)
    nsa_o = nsa_sample_pallas(aq, aks, avs, akw, avw, ag_s, cache_k_cmp, cache_v_cmp, cache_k_sel,
                              cache_v_sel, state_k_win, state_v_win, page_table, cmp_k, cmp_v, a_kcn)
    hs = xs + merge_groups(ret_o, rg, nsa_o, r_gn, w_out)
    hs = hs + memory_attend(rmsnorm(hs, g_xattn), cache_mem_k, cache_mem_v, w_mq, m_qn, w_mo)
    n_p = hp.shape[0] * hp.shape[1]
    tokens = jnp.concatenate([rmsnorm(hp, g_ffn).reshape(-1, D_MODEL), rmsnorm(hs, g_ffn).reshape(-1, D_MODEL)], axis=0)
    y = peer_pallas(tokens, w_pq, sub_k1, sub_k2, exp_u.astype(jnp.bfloat16), exp_v.astype(jnp.bfloat16))
    xp = hp + y[:n_p].reshape(hp.shape)
    xs = hs + y[n_p:].reshape(hs.shape)
    wb = state_k_win.shape[1]
    k_cmp_s, v_cmp_s, k_sel_s, v_sel_s = akc, avc, aks, avs
    k_win_s = jnp.concatenate([state_k_win, akw.astype(state_k_win.dtype)], axis=1)[:, -wb:]
    v_win_s = jnp.concatenate([state_v_win, avw.astype(state_v_win.dtype)], axis=1)[:, -wb:]
    ret_s = ret_s.astype(state_ret.dtype)
    return (xp, xs, k_cmp_p, v_cmp_p, k_sel_p, v_sel_p, k_win_p, v_win_p, ret_p, mem_k_p, mem_v_p,
            k_cmp_s, v_cmp_s, k_sel_s, v_sel_s, k_win_s, v_win_s, ret_s)
```

```python
import functools

import jax
import jax.numpy as jnp
import numpy as np
from jax import lax
from jax.experimental import pallas as pl
from jax.experimental.pallas import tpu as pltpu

D_MODEL = 4096
PAGE_SIZE = 128
R_HEADS = 8
R_DK = 256
R_DV = 256
R_CHUNK = 128
ROPE_BASE = 10000.0
A_HEADS = 16
A_KV = 4
A_DH = 128
A_HPG = A_HEADS // A_KV
L_CMP = 32
CMP_HID = 256
L_SEL = 64
N_SEL = 16
WINDOW = 512
NSA_QB = 256
FORCE_SCORE = 1e4
NEG = -1e30
M_HEADS = 4
M_DH = 128
M_W = M_HEADS * M_DH
P_HEADS = 8
P_NKEYS = 128
P_DQ = 256
P_TOPK = 16
P_BLOCK = 128
EPS = 1e-6
R_W = R_HEADS * R_DV
A_W = A_HEADS * A_DH
IN_SIZES = (R_HEADS * R_DK, R_HEADS * R_DK, R_W, R_W, A_W) + (A_KV * A_DH,) * 6 + (A_HEADS * 3,)

VMEM_LIMIT_BYTES = 56 * 1024 * 1024


def _mm_kernel(a_ref, b_ref, o_ref, acc_ref):
    k = pl.program_id(2)

    @pl.when(k == 0)
    def _():
        acc_ref[...] = jnp.zeros_like(acc_ref)

    acc_ref[...] += jnp.dot(a_ref[...].astype(jnp.bfloat16), b_ref[...].astype(jnp.bfloat16),
                            preferred_element_type=jnp.float32)

    @pl.when(k == pl.num_programs(2) - 1)
    def _():
        o_ref[...] = acc_ref[...]


def _mm_fullk_kernel(a_ref, b_ref, o_ref):
    o_ref[...] = jnp.dot(a_ref[...].astype(jnp.bfloat16), b_ref[...].astype(jnp.bfloat16),
                         preferred_element_type=jnp.float32)


MM_FULLK_TM = 1024
MM_FULLK_TN = 512


def mm(a, b, tm=1024, tn=1024, tk=1024):
    m, kdim = a.shape
    _, n = b.shape
    if (a.dtype == jnp.bfloat16 and b.dtype == jnp.bfloat16 and m % MM_FULLK_TM == 0
            and n >= MM_FULLK_TN and kdim <= 4096):
        return pl.pallas_call(
            _mm_fullk_kernel,
            grid=(m // MM_FULLK_TM, pl.cdiv(n, MM_FULLK_TN)),
            in_specs=[pl.BlockSpec((MM_FULLK_TM, kdim), lambda i, j: (i, 0)),
                      pl.BlockSpec((kdim, MM_FULLK_TN), lambda i, j: (0, j))],
            out_specs=pl.BlockSpec((MM_FULLK_TM, MM_FULLK_TN), lambda i, j: (i, j)),
            out_shape=jax.ShapeDtypeStruct((m, n), jnp.float32),
            compiler_params=pltpu.CompilerParams(
                dimension_semantics=("parallel", "parallel"), vmem_limit_bytes=VMEM_LIMIT_BYTES),
            name="mm_fullk",
        )(a, b)
    tm = min(tm, m)
    tn = min(tn, n)
    tk = min(tk, kdim)
    assert m % tm == 0 and kdim % tk == 0
    grid = (m // tm, pl.cdiv(n, tn), kdim // tk)
    return pl.pallas_call(
        _mm_kernel,
        grid=grid,
        in_specs=[pl.BlockSpec((tm, tk), lambda i, j, k: (i, k)),
                  pl.BlockSpec((tk, tn), lambda i, j, k: (k, j))],
        out_specs=pl.BlockSpec((tm, tn), lambda i, j, k: (i, j)),
        out_shape=jax.ShapeDtypeStruct((m, n), jnp.float32),
        scratch_shapes=[pltpu.VMEM((tm, tn), jnp.float32)],
        compiler_params=pltpu.CompilerParams(
            dimension_semantics=("parallel", "parallel", "arbitrary"),
            vmem_limit_bytes=VMEM_LIMIT_BYTES),
        name="mm",
    )(a, b)


def mmnd(a, b):
    lead = a.shape[:-1]
    out = mm(a.reshape(-1, a.shape[-1]), b)
    return out.reshape(lead + (b.shape[-1],))


def _mm_nt_kernel(a_ref, b_ref, o_ref):
    o_ref[...] = lax.dot_general(a_ref[...].astype(jnp.bfloat16), b_ref[...].astype(jnp.bfloat16),
                                 (((1,), (1,)), ((), ())), preferred_element_type=jnp.float32)


def mm_nt(a, b, tm=512, tn=512):
    m, kdim = a.shape
    n, _ = b.shape
    tm, tn = min(tm, m), min(tn, n)
    assert m % tm == 0 and n % tn == 0
    return pl.pallas_call(
        _mm_nt_kernel,
        grid=(m // tm, n // tn),
        in_specs=[pl.BlockSpec((tm, kdim), lambda i, j: (i, 0)),
                  pl.BlockSpec((tn, kdim), lambda i, j: (j, 0))],
        out_specs=pl.BlockSpec((tm, tn), lambda i, j: (i, j)),
        out_shape=jax.ShapeDtypeStruct((m, n), jnp.float32),
        compiler_params=pltpu.CompilerParams(
            dimension_semantics=("parallel", "parallel"),
            vmem_limit_bytes=VMEM_LIMIT_BYTES),
        name="mm_nt",
    )(a, b)


PEER_TR = 256
PEER_TM = 512
PEER_TE = 512
PEER_SLABS = PEER_TE // P_NKEYS
TAKEN = -3.0e38


def _extract_top(s, n):
    rows = s.shape[0]
    row_id = lax.broadcasted_iota(jnp.int32, s.shape, 0).astype(jnp.float32)
    rank = jnp.full(s.shape, float(n), jnp.float32)
    vals = []
    for k in range(n):
        m = jnp.max(s, axis=0, keepdims=True)
        first = jnp.min(jnp.where(s == m, row_id, float(rows)), axis=0, keepdims=True)
        hit = row_id == first
        rank = jnp.where(hit, float(k), rank)
        s = jnp.where(hit, TAKEN, s)
        vals.append(m)
    return rank, jnp.concatenate(vals, axis=0)


def _peer_route_kernel(qt_ref, k1_ref, k2_ref, r2_ref, e2_ref, n_ref, e1n_ref):
    f32, bf16 = jnp.float32, jnp.bfloat16
    half = P_DQ // 2
    qt = qt_ref[...].astype(bf16)
    s1 = jnp.dot(k1_ref[0].astype(bf16), qt[:half], preferred_element_type=f32)
    s2 = jnp.dot(k2_ref[0].astype(bf16), qt[half:], preferred_element_type=f32)
    rank1, v1 = _extract_top(s1, P_TOPK)
    rank2, v2 = _extract_top(s2, P_TOPK)
    r1_id = lax.broadcasted_iota(jnp.int32, v1.shape, 0).astype(f32)
    cnt = jnp.zeros_like(v1)
    for _ in range(P_TOPK):
        nxt = jnp.full_like(v1, TAKEN)
        for j in range(P_TOPK):
            nxt = jnp.where(cnt == float(j), v2[j:j + 1], nxt)
        front = jnp.where(cnt < float(P_TOPK), v1 + nxt, TAKEN)
        m = jnp.max(front, axis=0, keepdims=True)
        first = jnp.min(jnp.where(front == m, r1_id, float(P_TOPK)), axis=0, keepdims=True)
        cnt = cnt + jnp.where(r1_id == first, 1.0, 0.0)
    ev1 = jnp.exp(v1 - v1[0:1])
    ev2 = jnp.exp(v2 - v2[0:1])
    inner = jnp.zeros_like(v1)
    for j in range(P_TOPK):
        inner = inner + jnp.where(cnt > float(j), ev2[j:j + 1], 0.0)
    z = jnp.sum(ev1 * inner, axis=0, keepdims=True)
    n_dense = jnp.zeros_like(s1)
    for r in range(P_TOPK):
        n_dense = n_dense + jnp.where(rank1 == float(r), cnt[r:r + 1], 0.0)
    r2_ref[0] = rank2
    e2_ref[0] = jnp.exp(s2 - v2[0:1])
    n_ref[0] = n_dense
    e1n_ref[0] = jnp.exp(s1 - v1[0:1]) * (1.0 / z)


def _peer_mix_kernel(x_ref, u_ref, v_ref, r2_ref, e2_ref, n_ref, e1n_ref, y_ref, a_scr):
    f32, bf16 = jnp.float32, jnp.bfloat16
    j = pl.program_id(1)

    @pl.when(j == 0)
    def _():
        y_ref[...] = jnp.zeros_like(y_ref)
        a_scr[1] = jnp.zeros(a_scr.shape[1:], f32)

    act = jax.nn.gelu(a_scr[(j + 1) % 2])
    a_scr[j % 2] = lax.dot_general(u_ref[...], x_ref[...], (((1,), (1,)), ((), ())),
                                   preferred_element_type=f32)

    tile = jnp.maximum(j - 1, 0)
    parts = []
    for s in range(PEER_SLABS):
        first_key = tile * PEER_SLABS + s
        g = jnp.zeros((P_NKEYS, act.shape[1]), f32)
        for h in range(P_HEADS):
            n_row = n_ref[h, pl.ds(first_key, 1), :]
            w_row = e1n_ref[h, pl.ds(first_key, 1), :]
            g = g + jnp.where(r2_ref[h] < n_row, e2_ref[h] * w_row, 0.0)
        parts.append((g * act[s * P_NKEYS:(s + 1) * P_NKEYS]).astype(bf16))
    w_t = jnp.concatenate(parts, axis=0)
    y_ref[...] += lax.dot_general(w_t, v_ref[...], (((0,), (0,)), ((), ())),
                                  preferred_element_type=f32)


def peer_pallas(x, w_pq, sub_k1, sub_k2, exp_u_b, exp_v_b):
    bf16 = jnp.bfloat16
    n, d = x.shape
    n_exp = exp_u_b.shape[0]
    assert n % PEER_TM == 0 and n % PEER_TR == 0 and n_exp % PEER_TE == 0
    xb = x.astype(bf16)
    qt = mm_nt(jnp.swapaxes(w_pq, 0, 1), xb)
    route_shape = jax.ShapeDtypeStruct((P_HEADS, P_NKEYS, n), jnp.float32)
    key_spec = pl.BlockSpec((1, P_NKEYS, P_DQ // 2), lambda i, h: (h, 0, 0))
    route_out = pl.BlockSpec((1, P_NKEYS, PEER_TR), lambda i, h: (h, 0, i))
    r2, e2, nd, e1n = pl.pallas_call(
        _peer_route_kernel,
        grid=(n // PEER_TR, P_HEADS),
        in_specs=[pl.BlockSpec((P_DQ, PEER_TR), lambda i, h: (h, i)), key_spec, key_spec],
        out_specs=[route_out] * 4,
        out_shape=[route_shape] * 4,
        compiler_params=pltpu.CompilerParams(
            dimension_semantics=("parallel", "parallel"), vmem_limit_bytes=VMEM_LIMIT_BYTES),
        name="peer_route",
    )(qt, sub_k1, sub_k2)
    route_in = pl.BlockSpec((P_HEADS, P_NKEYS, PEER_TM), lambda i, j: (0, 0, i),
                            pipeline_mode=pl.Buffered(1))
    n_tiles = n_exp // PEER_TE
    return pl.pallas_call(
        _peer_mix_kernel,
        grid=(n // PEER_TM, n_tiles + 1),
        in_specs=[pl.BlockSpec((PEER_TM, d), lambda i, j: (i, 0), pipeline_mode=pl.Buffered(1)),
                  pl.BlockSpec((PEER_TE, d), lambda i, j: (jnp.minimum(j, n_tiles - 1), 0)),
                  pl.BlockSpec((PEER_TE, d), lambda i, j: (jnp.maximum(j - 1, 0), 0)),
                  route_in, route_in, route_in, route_in],
        out_specs=pl.BlockSpec((PEER_TM, d), lambda i, j: (i, 0)),
        out_shape=jax.ShapeDtypeStruct((n, d), jnp.float32),
        scratch_shapes=[pltpu.VMEM((2, PEER_TE, PEER_TM), jnp.float32)],
        compiler_params=pltpu.CompilerParams(
            dimension_semantics=("parallel", "arbitrary"), vmem_limit_bytes=VMEM_LIMIT_BYTES),
        name="peer_mix",
    )(xb, exp_u_b, exp_v_b, r2, e2, nd, e1n)


RET_CHUNKS_PER_STEP = 8


def _retention_prompt_kernel(q_ref, k_ref, v_ref, gate_ref, cos_ref, sin_ref, dmask_ref, dq_ref, dk_ref,
                             cdec_ref, gn_ref, o_ref, state_ref, s_scr):
    f32, bf16 = jnp.float32, jnp.bfloat16
    c = pl.program_id(1)

    @pl.when(c == 0)
    def _():
        s_scr[...] = jnp.zeros_like(s_scr)

    half = R_DK // 2
    state = s_scr[...]
    for sub in range(RET_CHUNKS_PER_STEP):
        rows = slice(sub * R_CHUNK, (sub + 1) * R_CHUNK)
        cos, sin = cos_ref[rows, :], sin_ref[rows, :]

        def rot(x):
            x1, x2 = x[:, :half], x[:, half:]
            return jnp.concatenate([x1 * cos - x2 * sin, x1 * sin + x2 * cos], axis=1)

        q = (rot(q_ref[rows, :]) * R_DK ** -0.5).astype(bf16)
        k = rot(k_ref[rows, :])
        v = v_ref[rows, :].astype(bf16)
        inner = lax.dot_general(q, k.astype(bf16), (((1,), (1,)), ((), ())),
                                preferred_element_type=f32) * dmask_ref[0]
        o = jnp.dot(inner.astype(bf16), v, preferred_element_type=f32)
        o = o + jnp.dot(q, state.astype(bf16), preferred_element_type=f32) * dq_ref[0]
        kdec = (k * dk_ref[0]).astype(bf16)
        state = state * cdec_ref[0] + lax.dot_general(kdec, v, (((0,), (0,)), ((), ())),
                                                      preferred_element_type=f32)
        y = o * lax.rsqrt(jnp.mean(o * o, axis=-1, keepdims=True) + EPS) * gn_ref[0]
        gate = gate_ref[rows, :]
        o_ref[rows, :] = y * (gate * (1.0 / (1.0 + jnp.exp(-gate))))
    s_scr[...] = state

    @pl.when(c == pl.num_programs(1) - 1)
    def _():
        state_ref[0] = state


def retention_prompt_pallas(proj, r_gn):
    f32 = jnp.float32
    T = proj.shape[0]
    step_rows = RET_CHUNKS_PER_STEP * R_CHUNK
    assert T % step_rows == 0 and R_DK == R_DV
    half = R_DK // 2
    inv = ROPE_BASE ** (-jnp.arange(half, dtype=f32) / half)
    ang = jnp.arange(T, dtype=f32)[:, None] * inv[None, :]
    lg = jnp.log1p(-jnp.exp2(-5.0 - jnp.arange(R_HEADS, dtype=f32)))
    idx = jnp.arange(R_CHUNK, dtype=f32)
    rel = idx[:, None] - idx[None, :]
    dmask = jnp.where(rel[None] >= 0, jnp.exp(jnp.maximum(rel, 0.0)[None] * lg[:, None, None]), 0.0)
    dq = jnp.exp((idx + 1.0)[None, :] * lg[:, None])[:, :, None]
    dk = jnp.exp((R_CHUNK - 1.0 - idx)[None, :] * lg[:, None])[:, :, None]
    cdec = jnp.broadcast_to(jnp.exp(R_CHUNK * lg)[:, None, None], (R_HEADS, 1, R_DV))
    col = lambda base: pl.BlockSpec((step_rows, R_DK), lambda h, c, base=base: (c, base + h))
    per_head = lambda shape: pl.BlockSpec((1,) + shape, lambda h, c: (h, 0, 0))
    trig = pl.BlockSpec((step_rows, half), lambda h, c: (c, 0))
    return pl.pallas_call(
        _retention_prompt_kernel,
        grid=(R_HEADS, T // step_rows),
        in_specs=[col(0), col(R_HEADS), col(2 * R_HEADS), col(3 * R_HEADS), trig, trig,
                  per_head((R_CHUNK, R_CHUNK)), per_head((R_CHUNK, 1)), per_head((R_CHUNK, 1)),
                  per_head((1, R_DV)), per_head((1, R_DV))],
        out_specs=[pl.BlockSpec((step_rows, R_DV), lambda h, c: (c, h)),
                   pl.BlockSpec((1, R_DK, R_DV), lambda h, c: (h, 0, 0))],
        out_shape=[jax.ShapeDtypeStruct((T, R_W), f32),
                   jax.ShapeDtypeStruct((R_HEADS, R_DK, R_DV), f32)],
        scratch_shapes=[pltpu.VMEM((R_DK, R_DV), f32)],
        compiler_params=pltpu.CompilerParams(
            dimension_semantics=("parallel", "arbitrary"), vmem_limit_bytes=VMEM_LIMIT_BYTES),
        name="retention_prompt",
    )(proj, proj, proj, proj, jnp.cos(ang), jnp.sin(ang), dmask, dq, dk, cdec,
      r_gn.astype(f32).reshape(R_HEADS, 1, R_DV))


def rmsnorm(x, g):
    xf = x.astype(jnp.float32)
    y = xf * lax.rsqrt(jnp.mean(xf * xf, axis=-1, keepdims=True) + EPS)
    return (y * g.astype(jnp.float32)).astype(x.dtype)


def rotary(x, pos):
    half = x.shape[-1] // 2
    inv = ROPE_BASE ** (-jnp.arange(half, dtype=jnp.float32) / half)
    ang = pos.astype(jnp.float32)[:, None] * inv[None, :]
    cos, sin = jnp.cos(ang)[:, None, :], jnp.sin(ang)[:, None, :]
    xf = x.astype(jnp.float32)
    x1, x2 = xf[..., :half], xf[..., half:]
    return jnp.concatenate([x1 * cos - x2 * sin, x1 * sin + x2 * cos], axis=-1).astype(x.dtype)


def masked_softmax(s, mask):
    p = jax.nn.softmax(jnp.where(mask, s, NEG), axis=-1)
    return jnp.where(mask, p, 0.0)


def split_mix(h, w_in):
    B, T, _ = h.shape
    cuts = np.cumsum(IN_SIZES)[:-1].tolist()
    rq, rk, rv, rg, aq, akc, avc, aks, avs, akw, avw, ag = jnp.split(mmnd(h, w_in), cuts, axis=-1)
    r = lambda a, n, d: a.reshape(B, T, n, d)
    return (r(rq, R_HEADS, R_DK), r(rk, R_HEADS, R_DK), r(rv, R_HEADS, R_DV), rg,
            r(aq, A_HEADS, A_DH), r(akc, A_KV, A_DH), r(avc, A_KV, A_DH), r(aks, A_KV, A_DH),
            r(avs, A_KV, A_DH), r(akw, A_KV, A_DH), r(avw, A_KV, A_DH),
            jax.nn.sigmoid(ag.astype(jnp.float32)).reshape(B, T, A_HEADS, 3), ag)


def retention_chunk(S, q, k, v):
    q, k, v = q.astype(jnp.float32), k.astype(jnp.float32), v.astype(jnp.float32)
    C = q.shape[1]
    lg = jnp.log1p(-jnp.exp2(-5.0 - jnp.arange(R_HEADS, dtype=jnp.float32)))
    idx = jnp.arange(C, dtype=jnp.float32)
    rel = idx[:, None] - idx[None, :]
    dmask = jnp.where(rel[None] >= 0, jnp.exp(jnp.maximum(rel, 0.0)[None] * lg[:, None, None]), 0.0)
    inner = jnp.einsum('bihd,bjhd->bhij', q, k) * dmask[None]
    o = jnp.einsum('bhij,bjhe->bihe', inner, v)
    o = o + jnp.einsum('bihd,bhde->bihe', q, S) * jnp.exp((idx + 1.0)[:, None] * lg[None, :])[None, :, :, None]
    kdec = k * jnp.exp((C - 1.0 - idx)[:, None] * lg[None, :])[None, :, :, None]
    S_new = S * jnp.exp(C * lg)[None, :, None, None] + jnp.einsum('bjhd,bjhe->bhde', kdec, v)
    return S_new, o


def compress(rows, pe, w1, b1, w2, b2):
    T = rows.shape[-3]
    nc = T // L_CMP
    lead = rows.shape[:-3]
    blk = rows[..., :nc * L_CMP, :, :].reshape(lead + (nc, L_CMP, A_KV, A_DH)) + pe[:, None, :]
    blk = jnp.swapaxes(blk, -3, -2).reshape(lead + (nc, A_KV, L_CMP * A_DH))
    return mmnd(jax.nn.gelu(mmnd(blk, w1) + b1), w2) + b2


NSA_TK = 512
N_TOPK_NEG = -3.0e38


def _softmax_rows(s, mask):
    s = jnp.where(mask, s, NEG)
    m = jnp.max(s, axis=-1, keepdims=True)
    e = jnp.where(mask, jnp.exp(s - m), 0.0)
    l = jnp.sum(e, axis=-1, keepdims=True)
    return e * (1.0 / jnp.maximum(l, 1e-30))


def _top_n_mask_cols(score_t, n):
    rows = score_t.shape[0]
    row_id = lax.broadcasted_iota(jnp.int32, score_t.shape, 0).astype(jnp.float32)

    def one(_, c):
        sc, sel = c
        m = jnp.max(sc, axis=0, keepdims=True)
        first = jnp.min(jnp.where(sc == m, row_id, float(rows)), axis=0, keepdims=True)
        hit = row_id == first
        return jnp.where(hit, N_TOPK_NEG, sc), jnp.where(hit, 1.0, sel)

    _, sel = lax.fori_loop(0, n, one, (score_t, jnp.zeros_like(score_t)), unroll=True)
    return sel


def _nsa_prompt_kernel(q_ref, gate_ref, kc_ref, vc_ref, ks_ref, vs_ref, kw_ref, vw_ref, e_ref, o_ref,
                       *, n_win_keys):
    f32, bf16 = jnp.float32, jnp.bfloat16
    qb = pl.program_id(1)
    start = qb * NSA_QB
    nt_dims = (((1,), (1,)), ((), ()))
    q = q_ref[...]
    qg = jnp.concatenate([q[:, h * A_DH:(h + 1) * A_DH] for h in range(A_HPG)], axis=0)
    q_pos = start + lax.broadcasted_iota(jnp.int32, (NSA_QB, 1), 0)
    rep = lambda a: jnp.concatenate([a] * A_HPG, axis=0)

    kc, vc = kc_ref[0], vc_ref[0]
    nc = kc.shape[0]
    nsb = nc // 2
    s_c = lax.dot_general(qg, kc, nt_dims, preferred_element_type=f32)
    lane_c = lax.broadcasted_iota(jnp.int32, (1, nc), 1)
    c_orig = jnp.where(lane_c < nsb, 2 * lane_c, 2 * (lane_c - nsb) + 1)
    cmask = ((c_orig + 1) * L_CMP - 1) <= q_pos
    p_c = _softmax_rows(s_c, rep(cmask))
    o_c = jnp.dot(p_c.astype(bf16), vc, preferred_element_type=f32)

    ph = p_c[0:NSA_QB]
    for h in range(1, A_HPG):
        ph = ph + p_c[h * NSA_QB:(h + 1) * NSA_QB]
    imp = ph[:, :nsb] + ph[:, nsb:]
    blk = lax.broadcasted_iota(jnp.int32, (1, nsb), 1)
    cur = q_pos // L_SEL
    forced = (blk == 0) | (blk == cur) | (blk == cur - 1)
    imp = jnp.where(blk <= cur, jnp.where(forced, FORCE_SCORE, imp), NEG)
    sel = _top_n_mask_cols(imp.T, min(N_SEL, nsb)).T
    sel_b = sel.astype(bf16)

    n_tiles = (start + NSA_QB + NSA_TK - 1) // NSA_TK

    def sel_step(j, carry):
        m, l, acc = carry
        off = pl.multiple_of(j * NSA_TK, NSA_TK)
        k = ks_ref[0, pl.ds(off, NSA_TK), :]
        v = vs_ref[0, pl.ds(off, NSA_TK), :]
        s = lax.dot_general(qg, k, nt_dims, preferred_element_type=f32)
        picked = jnp.dot(sel_b, e_ref[j], preferred_element_type=f32)
        k_pos = off + lax.broadcasted_iota(jnp.int32, (1, NSA_TK), 1)
        bias = jnp.where((picked > 0.5) & (k_pos <= q_pos), 0.0, NEG)
        s = s + rep(bias)
        m_new = jnp.maximum(m, jnp.max(s, axis=-1, keepdims=True))
        alpha = jnp.exp(m - m_new)
        p = jnp.exp(s - m_new)
        l = alpha * l + jnp.sum(p, axis=-1, keepdims=True)
        acc = alpha * acc + jnp.dot(p.astype(bf16), v, preferred_element_type=f32)
        return m_new, l, acc

    def pair_step(p, carry):
        return sel_step(2 * p + 1, sel_step(2 * p, carry))

    rows = A_HPG * NSA_QB
    m_s, l_s, acc_s = lax.fori_loop(
        0, (n_tiles + 1) // 2, pair_step,
        (jnp.full((rows, 1), NEG, f32), jnp.zeros((rows, 1), f32), jnp.zeros((rows, A_DH), f32)))
    o_s = acc_s * (1.0 / l_s)

    base = pl.multiple_of(jnp.maximum(start + NSA_QB - n_win_keys, 0), NSA_QB)
    kw = kw_ref[0, pl.ds(base, n_win_keys), :]
    vw = vw_ref[0, pl.ds(base, n_win_keys), :]
    s_w = lax.dot_general(qg, kw, nt_dims, preferred_element_type=f32)
    rel = q_pos - (base + lax.broadcasted_iota(jnp.int32, (1, n_win_keys), 1))
    p_w = _softmax_rows(s_w, rep((rel >= 0) & (rel < WINDOW)))
    o_w = jnp.dot(p_w.astype(bf16), vw, preferred_element_type=f32)

    gl = gate_ref[0]
    g = 1.0 / (1.0 + jnp.exp(-gl))
    outs = []
    for h in range(A_HPG):
        r = slice(h * NSA_QB, (h + 1) * NSA_QB)
        outs.append(o_c[r] * g[:, 3 * h:3 * h + 1] + o_s[r] * g[:, 3 * h + 1:3 * h + 2]
                    + o_w[r] * g[:, 3 * h + 2:3 * h + 3])
    o_ref[...] = jnp.concatenate(outs, axis=-1)


def nsa_prompt_pallas(q, kc, vc, ks, vs, kw, vw, gate_logits):
    bf16 = jnp.bfloat16
    T = q.shape[0]
    nc = kc.shape[0]
    nsb = T // L_SEL
    assert T % (2 * NSA_TK) == 0 and nc == 2 * nsb
    n_win_keys = min(WINDOW + NSA_QB, T)
    qs = (q.astype(jnp.float32) * A_DH ** -0.5).astype(bf16).reshape(T, A_W)
    grp = lambda a: jnp.swapaxes(a, 0, 1).astype(bf16)
    perm = lambda a: jnp.concatenate([a[0::2], a[1::2]], axis=0)
    gl = jnp.swapaxes(gate_logits.reshape(T, A_KV, A_HPG * 3), 0, 1)
    n_t = T // NSA_TK
    key_blk = (jnp.arange(T, dtype=jnp.int32) // L_SEL).reshape(n_t, 1, NSA_TK)
    expand = (jnp.arange(nsb, dtype=jnp.int32)[None, :, None] == key_blk).astype(bf16)
    kv_spec = lambda rows: pl.BlockSpec((1, rows, A_DH), lambda g, i: (g, 0, 0))
    return pl.pallas_call(
        functools.partial(_nsa_prompt_kernel, n_win_keys=n_win_keys),
        grid=(A_KV, T // NSA_QB),
        in_specs=[pl.BlockSpec((NSA_QB, A_HPG * A_DH), lambda g, i: (i, g)),
                  pl.BlockSpec((1, NSA_QB, A_HPG * 3), lambda g, i: (g, i, 0)),
                  kv_spec(nc), kv_spec(nc), kv_spec(T), kv_spec(T), kv_spec(T), kv_spec(T),
                  pl.BlockSpec((n_t, nsb, NSA_TK), lambda g, i: (0, 0, 0))],
        out_specs=pl.BlockSpec((NSA_QB, A_HPG * A_DH), lambda g, i: (i, g)),
        out_shape=jax.ShapeDtypeStruct((T, A_W), jnp.float32),
        compiler_params=pltpu.CompilerParams(
            dimension_semantics=("parallel", "arbitrary"),
            vmem_limit_bytes=VMEM_LIMIT_BYTES),
        name="nsa_prompt",
    )(qs, gl, grp(perm(kc)), grp(perm(vc)), grp(ks), grp(vs), grp(kw), grp(vw), expand)


SQ_PAD = 8
SEL_TAIL = 128


CMP_CHUNK = 8


def _compress_seq(src, pe_ref, perm_ref, w1_ref, b1_ref, w2_ref, b2_ref, n_blk):
    f32, bf16 = jnp.float32, jnp.bfloat16
    chunk_rows = CMP_CHUNK * L_CMP
    regrouped = []
    chunk = lambda g, j: (src[g, j * chunk_rows:(j + 1) * chunk_rows, :] + pe_ref[...]).astype(bf16)
    for g in range(A_KV):
        for j in range(0, n_blk // CMP_CHUNK, 2):
            y = jnp.dot(perm_ref[...], jnp.concatenate([chunk(g, j), chunk(g, j + 1)], axis=1),
                        preferred_element_type=f32)
            regrouped.extend([y[:, :A_DH], y[:, A_DH:]])
    acc = jnp.zeros((A_KV * n_blk, CMP_HID), f32)
    row_l = lambda l: jnp.concatenate([y[l * CMP_CHUNK:(l + 1) * CMP_CHUNK] for y in regrouped], axis=0)
    for l in range(0, L_CMP, 2):
        xg = jnp.concatenate([row_l(l), row_l(l + 1)], axis=1).astype(bf16)
        acc = acc + jnp.dot(xg, w1_ref[l * A_DH:(l + 2) * A_DH, :], preferred_element_type=jnp.float32)
    hid = jax.nn.gelu(acc + b1_ref[...])
    return jnp.dot(hid.astype(bf16), w2_ref[...], preferred_element_type=jnp.float32) + b2_ref[...]


def _nsa_sample_kernel(pt_ref, kcp_ref, vcp_ref, ksp_ref, vsp_ref, kwb_ref, vwb_ref, q_ref, gate_ref,
                       ksn_ref, vsn_ref, kwn_ref, vwn_ref, pek_ref, pev_ref, perm_ref,
                       w1k_ref, b1k_ref, w2k_ref, b2k_ref, w1v_ref, b1v_ref, w2v_ref, b2v_ref, gkc_ref,
                       e_ref, o_ref, kc_scr, vc_scr, ks_scr, vs_scr, kw_scr, vw_scr, sem,
                       *, past, n_pages, n_new):
    f32, bf16 = jnp.float32, jnp.bfloat16
    b = pl.program_id(0)
    slot = b % 2

    def seq_copies(seq, dst_slot):
        cps = []
        for pg in range(n_pages):
            page = pt_ref[seq, pg]
            rows = pl.ds(pg * PAGE_SIZE, PAGE_SIZE)
            for g in range(A_KV):
                for pool_ref, scr in ((kcp_ref, kc_scr), (vcp_ref, vc_scr), (ksp_ref, ks_scr), (vsp_ref, vs_scr)):
                    cps.append(pltpu.make_async_copy(pool_ref.at[page, :, g, :], scr.at[dst_slot, g, rows, :],
                                                     sem.at[dst_slot]))
        for g in range(A_KV):
            for buf_ref, scr in ((kwb_ref, kw_scr), (vwb_ref, vw_scr)):
                cps.append(pltpu.make_async_copy(buf_ref.at[seq, :, g, :], scr.at[dst_slot, g, pl.ds(0, wb), :],
                                                 sem.at[dst_slot]))
        return cps

    wb = kwb_ref.shape[1]

    @pl.when(b == 0)
    def _():
        for cp in seq_copies(0, 0):
            cp.start()

    @pl.when(b + 1 < pl.num_programs(0))
    def _():
        for cp in seq_copies(b + 1, 1 - slot):
            cp.start()

    for cp in seq_copies(b, slot):
        cp.wait()

    def compute():
        nt_dims = (((1,), (1,)), ((), ()))
        n_blk = past // L_CMP
        half = n_blk // 2
        rows = A_HPG * SQ_PAD
        pad_rows = jnp.zeros((SEL_TAIL - SQ_PAD, A_DH), f32)
        tail = lambda new_ref, g: jnp.concatenate(
            [new_ref[0, :, g * A_DH:(g + 1) * A_DH], pad_rows], axis=0)
        for g in range(A_KV):
            ks_scr[slot, g, pl.ds(past, SEL_TAIL), :] = tail(ksn_ref, g)
            vs_scr[slot, g, pl.ds(past, SEL_TAIL), :] = tail(vsn_ref, g)
            kw_scr[slot, g, pl.ds(wb, SEL_TAIL), :] = tail(kwn_ref, g)
            vw_scr[slot, g, pl.ds(wb, SEL_TAIL), :] = tail(vwn_ref, g)
        kcmp = _compress_seq(kc_scr.at[slot], pek_ref, perm_ref, w1k_ref, b1k_ref, w2k_ref, b2k_ref, n_blk)
        kcmp = kcmp * lax.rsqrt(jnp.mean(kcmp * kcmp, axis=-1, keepdims=True) + EPS) * gkc_ref[...]
        vcmp = _compress_seq(vc_scr.at[slot], pev_ref, perm_ref, w1v_ref, b1v_ref, w2v_ref, b2v_ref, n_blk)

        q_pos = past + (lax.broadcasted_iota(jnp.int32, (SQ_PAD, 1), 0))
        rep = lambda a: jnp.concatenate([a] * A_HPG, axis=0)
        lanes = P_NKEYS
        zero_blk = jnp.zeros((lanes - n_blk, A_DH), f32)
        lane_c = lax.broadcasted_iota(jnp.int32, (1, lanes), 1)
        cmask = jnp.broadcast_to(lane_c < n_blk, (rows, lanes))

        o_c, imps = [], []
        for g in range(A_KV):
            qg = q_ref[0, g]
            kc_g = jnp.concatenate([kcmp[g * n_blk:(g + 1) * n_blk], zero_blk], axis=0).astype(bf16)
            vc_g = jnp.concatenate([vcmp[g * n_blk:(g + 1) * n_blk], zero_blk], axis=0).astype(bf16)
            s_c = lax.dot_general(qg, kc_g, nt_dims, preferred_element_type=f32)
            p_c = _softmax_rows(s_c, cmask)
            o_c.append(jnp.dot(p_c.astype(bf16), vc_g, preferred_element_type=f32))
            ph = p_c[0:SQ_PAD]
            for h in range(1, A_HPG):
                ph = ph + p_c[h * SQ_PAD:(h + 1) * SQ_PAD]
            imps.append(ph + pltpu.roll(ph, lanes - 1, axis=1))
        imp = jnp.concatenate(imps, axis=0)
        blk = lane_c // 2
        q_pos4 = jnp.concatenate([q_pos] * A_KV, axis=0)
        cur = q_pos4 // L_SEL
        forced = (blk == 0) | (blk == cur) | (blk == cur - 1)
        imp = jnp.where(blk <= cur, jnp.where(forced, FORCE_SCORE, imp), NEG)
        imp = jnp.where(lane_c % 2 == 0, imp, N_TOPK_NEG)
        nsb = -(-(past + n_new) // L_SEL)
        imp_sq = jnp.concatenate([imp, jnp.full((lanes - imp.shape[0], lanes), N_TOPK_NEG, f32)], axis=0)
        sel = _top_n_mask_cols(imp_sq.T, min(N_SEL, nsb)).T[:imp.shape[0]]
        picked = jnp.dot(sel.astype(bf16), e_ref[...], preferred_element_type=f32)
        n_keys = past + SEL_TAIL
        k_pos = lax.broadcasted_iota(jnp.int32, (1, n_keys), 1)
        w_pos = past - wb + lax.broadcasted_iota(jnp.int32, (1, wb + SEL_TAIL), 1)
        rel = q_pos - w_pos
        w_mask = rep((rel >= 0) & (rel < WINDOW) & (w_pos >= 0))

        gl = gate_ref[0]
        outs = []
        for g in range(A_KV):
            qg = q_ref[0, g]
            s_mask = rep((picked[g * SQ_PAD:(g + 1) * SQ_PAD] > 0.5) & (k_pos <= q_pos))
            s_s = lax.dot_general(qg, ks_scr[slot, g].astype(bf16), nt_dims, preferred_element_type=f32)
            p_s = _softmax_rows(s_s, s_mask)
            o_s = jnp.dot(p_s.astype(bf16), vs_scr[slot, g].astype(bf16), preferred_element_type=f32)
            s_w = lax.dot_general(qg, kw_scr[slot, g].astype(bf16), nt_dims, preferred_element_type=f32)
            p_w = _softmax_rows(s_w, w_mask)
            o_w = jnp.dot(p_w.astype(bf16), vw_scr[slot, g].astype(bf16), preferred_element_type=f32)
            gs = 1.0 / (1.0 + jnp.exp(-gl[g]))
            o = o_c[g] * gs[:, 0:1] + o_s * gs[:, 1:2] + o_w * gs[:, 2:3]
            outs.extend([o[h * SQ_PAD:(h + 1) * SQ_PAD] for h in range(A_HPG)])
        o_ref[0] = jnp.concatenate(outs, axis=1)

    compute()


def nsa_sample_pallas(q, ksn, vsn, kwn, vwn, gate_logits, pool_kc, pool_vc, pool_ks, pool_vs,
                      buf_kw, buf_vw, page_table, cmp_k, cmp_v, g_kc):
    f32, bf16 = jnp.float32, jnp.bfloat16
    B, S = q.shape[:2]
    assert S <= SQ_PAD and S < L_CMP
    n_pages = page_table.shape[1]
    past = n_pages * PAGE_SIZE
    wb = buf_kw.shape[1]
    kvw = A_KV * A_DH
    pad_q = lambda a: jnp.pad(a, ((0, 0), (0, SQ_PAD - S)) + ((0, 0),) * (a.ndim - 2))
    qs = pad_q((q.astype(f32) * A_DH ** -0.5).astype(bf16)).reshape(B, SQ_PAD, A_KV, A_HPG, A_DH)
    qs = qs.transpose(0, 2, 3, 1, 4).reshape(B, A_KV, A_HPG * SQ_PAD, A_DH)
    gl = pad_q(gate_logits).reshape(B, SQ_PAD, A_KV, A_HPG, 3).transpose(0, 2, 3, 1, 4)
    gl = gl.reshape(B, A_KV, A_HPG * SQ_PAD, 3)
    new = lambda a: pad_q(a.reshape(B, S, kvw))
    pe_k, w1_k, b1_k, w2_k, b2_k = cmp_k
    pe_v, w1_v, b1_v, w2_v, b2_v = cmp_v
    row = lambda a: a.reshape(1, -1).astype(f32)
    n_keys = past + SEL_TAIL
    lane = jnp.arange(P_NKEYS, dtype=jnp.int32)[:, None]
    key_blk = (jnp.arange(n_keys, dtype=jnp.int32) // L_SEL)[None, :]
    expand = ((lane % 2 == 0) & (lane // 2 == key_blk)).astype(bf16)
    chunk_rows = CMP_CHUNK * L_CMP
    dst = jnp.arange(chunk_rows, dtype=jnp.int32)
    src_row = (dst % CMP_CHUNK) * L_CMP + dst // CMP_CHUNK
    perm = (src_row[:, None] == jnp.arange(chunk_rows, dtype=jnp.int32)[None, :]).astype(bf16)
    tile_pe = lambda pe: jnp.tile(pe.astype(f32), (CMP_CHUNK, 1))

    hbm = pl.BlockSpec(memory_space=pl.ANY)
    seq3 = lambda r, c: pl.BlockSpec((1, r, c), lambda b, pt: (b, 0, 0))
    seq4 = lambda a, r, c: pl.BlockSpec((1, a, r, c), lambda b, pt: (b, 0, 0, 0))
    full = lambda a: pl.BlockSpec(a.shape, lambda b, pt: (0,) * a.ndim, pipeline_mode=pl.Buffered(1))
    consts = [tile_pe(pe_k), tile_pe(pe_v), perm, w1_k.astype(bf16), row(b1_k), w2_k.astype(bf16), row(b2_k),
              w1_v.astype(bf16), row(b1_v), w2_v.astype(bf16), row(b2_v), row(g_kc), expand]
    slots = 2
    out = pl.pallas_call(
        functools.partial(_nsa_sample_kernel, past=past, n_pages=n_pages, n_new=S),
        grid_spec=pltpu.PrefetchScalarGridSpec(
            num_scalar_prefetch=1,
            grid=(B,),
            in_specs=[hbm] * 6
                     + [seq4(A_KV, A_HPG * SQ_PAD, A_DH), seq4(A_KV, A_HPG * SQ_PAD, 3)]
                     + [seq3(SQ_PAD, kvw)] * 4
                     + [full(c) for c in consts],
            out_specs=pl.BlockSpec((1, SQ_PAD, A_W), lambda b, pt: (b, 0, 0)),
            scratch_shapes=[pltpu.VMEM((slots, A_KV, past, A_DH), f32), pltpu.VMEM((slots, A_KV, past, A_DH), f32),
                            pltpu.VMEM((slots, A_KV, n_keys, A_DH), f32), pltpu.VMEM((slots, A_KV, n_keys, A_DH), f32),
                            pltpu.VMEM((slots, A_KV, wb + SEL_TAIL, A_DH), f32),
                            pltpu.VMEM((slots, A_KV, wb + SEL_TAIL, A_DH), f32),
                            pltpu.SemaphoreType.DMA((slots,))]),
        out_shape=jax.ShapeDtypeStruct((B, SQ_PAD, A_W), f32),
        compiler_params=pltpu.CompilerParams(
            dimension_semantics=("arbitrary",), vmem_limit_bytes=VMEM_LIMIT_BYTES),
        name="nsa_sample",
    )(page_table, pool_kc, pool_vc, pool_ks, pool_vs, buf_kw, buf_vw, qs, gl,
      new(ksn), new(vsn), new(kwn), new(vwn), *consts)
    return out[:, :S]


def merge_groups(ret_o, rg, nsa_o, r_gn, w_out):
    B, T = rg.shape[:2]
    r = rmsnorm(ret_o, r_gn).reshape(B, T, R_W) * jax.nn.silu(rg.astype(jnp.float32))
    mix = jnp.concatenate([r.astype(rg.dtype), nsa_o.reshape(B, T, A_W).astype(rg.dtype)], axis=-1)
    return mmnd(mix, w_out)


def memory_kv(mem, g_memtok, w_mk, w_mv, m_kn):
    B, M, _ = mem.shape
    m = rmsnorm(mem, g_memtok)
    k = rmsnorm(mmnd(m, w_mk).reshape(B, M, M_HEADS, M_DH), m_kn)
    v = mmnd(m, w_mv).reshape(B, M, M_HEADS, M_DH)
    return k, v


def memory_attend(h, mk, mv, w_mq, m_qn, w_mo):
    B, T, _ = h.shape
    q = rmsnorm(mmnd(h, w_mq).reshape(B, T, M_HEADS, M_DH), m_qn)
    s = jnp.einsum('bthd,bmhd->bhtm', q.astype(jnp.float32), mk.astype(jnp.float32)) * M_DH ** -0.5
    p = jax.nn.softmax(s, axis=-1)
    o = jnp.einsum('bhtm,bmhd->bthd', p, mv.astype(jnp.float32)).astype(h.dtype)
    return mmnd(o.reshape(B, T, M_W), w_mo)


def kernel(x_prompt, x_sample, mem_prompt, cache_k_cmp, cache_v_cmp, cache_k_sel, cache_v_sel, state_k_win, state_v_win, state_ret, cache_mem_k, cache_mem_v, page_table, g_mix, w_in, r_gn, a_qn, a_kcn, a_ksn, a_kwn, cmp_pe_k, cmp_w1_k, cmp_b1_k, cmp_w2_k, cmp_b2_k, cmp_pe_v, cmp_w1_v, cmp_b1_v, cmp_w2_v, cmp_b2_v, w_out, g_xattn, g_memtok, w_mq, w_mk, w_mv, m_qn, m_kn, w_mo, g_ffn, w_pq, sub_k1, sub_k2, exp_u, exp_v):
    cmp_k = (cmp_pe_k, cmp_w1_k, cmp_b1_k, cmp_w2_k, cmp_b2_k)
    cmp_v = (cmp_pe_v, cmp_w1_v, cmp_b1_v, cmp_w2_v, cmp_b2_v)
    xp, xs = x_prompt, x_sample

    T = xp.shape[1]
    w_in = w_in.astype(jnp.bfloat16)
    w_out = w_out.astype(jnp.bfloat16)
    assert xp.shape[0] == 1
    proj_p = mm(rmsnorm(xp, g_mix).astype(jnp.bfloat16)[0], w_in)
    ret_mix, ret_p = retention_prompt_pallas(proj_p, r_gn)
    ret_p = ret_p[None]
    cuts = np.cumsum(IN_SIZES)[3:-1].tolist()
    aq, akc, avc, aks, avs, akw, avw, ag_p = jnp.split(proj_p[None, :, cuts[0]:], [c - cuts[0] for c in cuts[1:]], axis=-1)
    kv = lambda a: a.reshape(1, T, A_KV, A_DH)
    aq, akc, avc, aks, avs, akw, avw = aq.reshape(1, T, A_HEADS, A_DH), kv(akc), kv(avc), kv(aks), kv(avs), kv(akw), kv(avw)
    aq, aks, akw = rmsnorm(aq, a_qn), rmsnorm(aks, a_ksn), rmsnorm(akw, a_kwn)
    kc = rmsnorm(compress(akc, *cmp_k), a_kcn)
    vc = compress(avc, *cmp_v)
    nsa_o = nsa_prompt_pallas(aq[0], kc[0], vc[0], aks[0], avs[0], akw[0], avw[0], ag_p[0])
    hp = xp + mm(jnp.concatenate([ret_mix, nsa_o], axis=-1).astype(jnp.bfloat16), w_out)[None]
    mem_k_p, mem_v_p = memory_kv(mem_prompt, g_memtok, w_mk, w_mv, m_kn)
    hp = hp + memory_attend(rmsnorm(hp, g_xattn), mem_k_p, mem_v_p, w_mq, m_qn, w_mo)
    wl = min(WINDOW, T)
    k_cmp_p, v_cmp_p, k_sel_p, v_sel_p = akc, avc, aks, avs
    k_win_p, v_win_p = akw[:, T - wl:], avw[:, T - wl:]
    ret_p = ret_p.astype(xp.dtype)

    S = xs.shape[1]
    past = page_table.shape[1] * PAGE_SIZE
    rq, rk, rv, rg, aq, akc, avc, aks, avs, akw, avw, ag, ag_s = split_mix(
        rmsnorm(xs, g_mix).astype(jnp.bfloat16), w_in)
    pos = past + jnp.arange(S)
    ret_s, ret_o = retention_chunk(state_ret.astype(jnp.float32), rotary(rq, pos) * R_DK ** -0.5,
                                   rotary(rk, pos), rv)
    aq, aks, akw = rmsnorm(aq, a_qn), rmsnorm(aks, a_ksn), rmsnorm(akw, a_kwn)
    nsa_o = nsa_sample_pallas(aq, aks, avs, akw, avw, ag_s, cache_k_cmp, cache_v_cmp, cache_k_sel,
                              cache_v_sel, state_k_win, state_v_win, page_table, cmp_k, cmp_v, a_kcn)
    hs = xs + merge_groups(ret_o, rg, nsa_o, r_gn, w_out)
    hs = hs + memory_attend(rmsnorm(hs, g_xattn), cache_mem_k, cache_mem_v, w_mq, m_qn, w_mo)
    n_p = hp.shape[0] * hp.shape[1]
    tokens = jnp.concatenate([rmsnorm(hp, g_ffn).reshape(-1, D_MODEL), rmsnorm(hs, g_ffn).reshape(-1, D_MODEL)], axis=0)
    y = peer_pallas(tokens, w_pq, sub_k1, sub_k2, exp_u.astype(jnp.bfloat16), exp_v.astype(jnp.bfloat16))
    xp = hp + y[:n_p].reshape(hp.shape)
    xs = hs + y[n_p:].reshape(hs.shape)
    wb = state_k_win.shape[1]
    k_cmp_s, v_cmp_s, k_sel_s, v_sel_s = akc, avc, aks, avs
    k_win_s = jnp.concatenate([state_k_win, akw.astype(state_k_win.dtype)], axis=1)[:, -wb:]
    v_win_s = jnp.concatenate([state_v_win, avw.astype(state_v_win.dtype)], axis=1)[:, -wb:]
    ret_s = ret_s.astype(state_ret.dtype)
    return (xp, xs, k_cmp_p, v_cmp_p, k_sel_p, v_sel_p, k_win_p, v_win_p, ret_p, mem_k_p, mem_v_p,
            k_cmp_s, v_cmp_s, k_sel_s, v_sel_s, k_win_s, v_win_s, ret_s)
```

```python
import functools

import jax
import jax.numpy as jnp
import numpy as np
from jax import lax
from jax.experimental import pallas as pl
from jax.experimental.pallas import tpu as pltpu

D_MODEL = 4096
PAGE_SIZE = 128
R_HEADS = 8
R_DK = 256
R_DV = 256
R_CHUNK = 128
ROPE_BASE = 10000.0
A_HEADS = 16
A_KV = 4
A_DH = 128
A_HPG = A_HEADS // A_KV
L_CMP = 32
CMP_HID = 256
L_SEL = 64
N_SEL = 16
WINDOW = 512
NSA_QB = 256
FORCE_SCORE = 1e4
NEG = -1e30
M_HEADS = 4
M_DH = 128
M_W = M_HEADS * M_DH
P_HEADS = 8
P_NKEYS = 128
P_DQ = 256
P_TOPK = 16
P_BLOCK = 128
EPS = 1e-6
R_W = R_HEADS * R_DV
A_W = A_HEADS * A_DH
IN_SIZES = (R_HEADS * R_DK, R_HEADS * R_DK, R_W, R_W, A_W) + (A_KV * A_DH,) * 6 + (A_HEADS * 3,)

VMEM_LIMIT_BYTES = 56 * 1024 * 1024


def _mm_kernel(a_ref, b_ref, o_ref, acc_ref):
    k = pl.program_id(2)

    @pl.when(k == 0)
    def _():
        acc_ref[...] = jnp.zeros_like(acc_ref)

    acc_ref[...] += jnp.dot(a_ref[...].astype(jnp.bfloat16), b_ref[...].astype(jnp.bfloat16),
                            preferred_element_type=jnp.float32)

    @pl.when(k == pl.num_programs(2) - 1)
    def _():
        o_ref[...] = acc_ref[...]


def _mm_fullk_kernel(a_ref, b_ref, o_ref):
    o_ref[...] = jnp.dot(a_ref[...].astype(jnp.bfloat16), b_ref[...].astype(jnp.bfloat16),
                         preferred_element_type=jnp.float32)


MM_FULLK_TM = 1024
MM_FULLK_TN = 512


def mm(a, b, tm=1024, tn=1024, tk=1024):
    m, kdim = a.shape
    _, n = b.shape
    if (a.dtype == jnp.bfloat16 and b.dtype == jnp.bfloat16 and m % MM_FULLK_TM == 0
            and n >= MM_FULLK_TN and kdim <= 4096):
        return pl.pallas_call(
            _mm_fullk_kernel,
            grid=(m // MM_FULLK_TM, pl.cdiv(n, MM_FULLK_TN)),
            in_specs=[pl.BlockSpec((MM_FULLK_TM, kdim), lambda i, j: (i, 0)),
                      pl.BlockSpec((kdim, MM_FULLK_TN), lambda i, j: (0, j))],
            out_specs=pl.BlockSpec((MM_FULLK_TM, MM_FULLK_TN), lambda i, j: (i, j)),
            out_shape=jax.ShapeDtypeStruct((m, n), jnp.float32),
            compiler_params=pltpu.CompilerParams(
                dimension_semantics=("parallel", "parallel"), vmem_limit_bytes=VMEM_LIMIT_BYTES),
            name="mm_fullk",
        )(a, b)
    tm = min(tm, m)
    tn = min(tn, n)
    tk = min(tk, kdim)
    assert m % tm == 0 and kdim % tk == 0
    grid = (m // tm, pl.cdiv(n, tn), kdim // tk)
    return pl.pallas_call(
        _mm_kernel,
        grid=grid,
        in_specs=[pl.BlockSpec((tm, tk), lambda i, j, k: (i, k)),
                  pl.BlockSpec((tk, tn), lambda i, j, k: (k, j))],
        out_specs=pl.BlockSpec((tm, tn), lambda i, j, k: (i, j)),
        out_shape=jax.ShapeDtypeStruct((m, n), jnp.float32),
        scratch_shapes=[pltpu.VMEM((tm, tn), jnp.float32)],
        compiler_params=pltpu.CompilerParams(
            dimension_semantics=("parallel", "parallel", "arbitrary"),
            vmem_limit_bytes=VMEM_LIMIT_BYTES),
        name="mm",
    )(a, b)


def mmnd(a, b):
    lead = a.shape[:-1]
    out = mm(a.reshape(-1, a.shape[-1]), b)
    return out.reshape(lead + (b.shape[-1],))


def _mm_nt_kernel(a_ref, b_ref, o_ref):
    o_ref[...] = lax.dot_general(a_ref[...].astype(jnp.bfloat16), b_ref[...].astype(jnp.bfloat16),
                                 (((1,), (1,)), ((), ())), preferred_element_type=jnp.float32)


def mm_nt(a, b, tm=512, tn=512):
    m, kdim = a.shape
    n, _ = b.shape
    tm, tn = min(tm, m), min(tn, n)
    assert m % tm == 0 and n % tn == 0
    return pl.pallas_call(
        _mm_nt_kernel,
        grid=(m // tm, n // tn),
        in_specs=[pl.BlockSpec((tm, kdim), lambda i, j: (i, 0)),
                  pl.BlockSpec((tn, kdim), lambda i, j: (j, 0))],
        out_specs=pl.BlockSpec((tm, tn), lambda i, j: (i, j)),
        out_shape=jax.ShapeDtypeStruct((m, n), jnp.float32),
        compiler_params=pltpu.CompilerParams(
            dimension_semantics=("parallel", "parallel"),
            vmem_limit_bytes=VMEM_LIMIT_BYTES),
        name="mm_nt",
    )(a, b)


PEER_TR = 512
PEER_TM = 512
PEER_TE = 512
PEER_SLABS = PEER_TE // P_NKEYS
TAKEN = -3.0e38


def _extract_top(s, n):
    rows = s.shape[0]
    row_id = lax.broadcasted_iota(jnp.int32, s.shape, 0).astype(jnp.float32)
    rank = jnp.full(s.shape, float(n), jnp.float32)
    vals = []
    for k in range(n):
        m = jnp.max(s, axis=0, keepdims=True)
        first = jnp.min(jnp.where(s == m, row_id, float(rows)), axis=0, keepdims=True)
        hit = row_id == first
        rank = jnp.where(hit, float(k), rank)
        s = jnp.where(hit, TAKEN, s)
        vals.append(m)
    return rank, jnp.concatenate(vals, axis=0)


def _peer_route_kernel(qt_ref, k1_ref, k2_ref, r2_ref, e2_ref, n_ref, e1n_ref):
    f32, bf16 = jnp.float32, jnp.bfloat16
    half = P_DQ // 2
    qt = qt_ref[...].astype(bf16)
    s1 = jnp.dot(k1_ref[0].astype(bf16), qt[:half], preferred_element_type=f32)
    s2 = jnp.dot(k2_ref[0].astype(bf16), qt[half:], preferred_element_type=f32)
    rank1, v1 = _extract_top(s1, P_TOPK)
    rank2, v2 = _extract_top(s2, P_TOPK)
    r1_id = lax.broadcasted_iota(jnp.int32, v1.shape, 0).astype(f32)
    cnt = jnp.zeros_like(v1)
    for _ in range(P_TOPK):
        nxt = jnp.full_like(v1, TAKEN)
        for j in range(P_TOPK):
            nxt = jnp.where(cnt == float(j), v2[j:j + 1], nxt)
        front = jnp.where(cnt < float(P_TOPK), v1 + nxt, TAKEN)
        m = jnp.max(front, axis=0, keepdims=True)
        first = jnp.min(jnp.where(front == m, r1_id, float(P_TOPK)), axis=0, keepdims=True)
        cnt = cnt + jnp.where(r1_id == first, 1.0, 0.0)
    ev1 = jnp.exp(v1 - v1[0:1])
    ev2 = jnp.exp(v2 - v2[0:1])
    inner = jnp.zeros_like(v1)
    for j in range(P_TOPK):
        inner = inner + jnp.where(cnt > float(j), ev2[j:j + 1], 0.0)
    z = jnp.sum(ev1 * inner, axis=0, keepdims=True)
    n_dense = jnp.zeros_like(s1)
    for r in range(P_TOPK):
        n_dense = n_dense + jnp.where(rank1 == float(r), cnt[r:r + 1], 0.0)
    r2_ref[0] = rank2
    e2_ref[0] = jnp.exp(s2 - v2[0:1])
    n_ref[0] = n_dense
    e1n_ref[0] = jnp.exp(s1 - v1[0:1]) * (1.0 / z)


def _peer_mix_kernel(x_ref, u_ref, v_ref, r2_ref, e2_ref, n_ref, e1n_ref, y_ref, a_scr):
    f32, bf16 = jnp.float32, jnp.bfloat16
    j = pl.program_id(1)

    @pl.when(j == 0)
    def _():
        y_ref[...] = jnp.zeros_like(y_ref)
        a_scr[1] = jnp.zeros(a_scr.shape[1:], f32)

    act = jax.nn.gelu(a_scr[(j + 1) % 2])
    a_scr[j % 2] = lax.dot_general(u_ref[...], x_ref[...], (((1,), (1,)), ((), ())),
                                   preferred_element_type=f32)

    tile = jnp.maximum(j - 1, 0)
    parts = []
    for s in range(PEER_SLABS):
        first_key = tile * PEER_SLABS + s
        g = jnp.zeros((P_NKEYS, act.shape[1]), f32)
        for h in range(P_HEADS):
            n_row = n_ref[h, pl.ds(first_key, 1), :]
            w_row = e1n_ref[h, pl.ds(first_key, 1), :]
            g = g + jnp.where(r2_ref[h] < n_row, e2_ref[h] * w_row, 0.0)
        parts.append((g * act[s * P_NKEYS:(s + 1) * P_NKEYS]).astype(bf16))
    w_t = jnp.concatenate(parts, axis=0)
    y_ref[...] += lax.dot_general(w_t, v_ref[...], (((0,), (0,)), ((), ())),
                                  preferred_element_type=f32)


def peer_pallas(x, w_pq, sub_k1, sub_k2, exp_u_b, exp_v_b):
    bf16 = jnp.bfloat16
    n, d = x.shape
    n_exp = exp_u_b.shape[0]
    assert n % PEER_TM == 0 and n % PEER_TR == 0 and n_exp % PEER_TE == 0
    xb = x.astype(bf16)
    qt = mm_nt(jnp.swapaxes(w_pq, 0, 1), xb)
    route_shape = jax.ShapeDtypeStruct((P_HEADS, P_NKEYS, n), jnp.float32)
    key_spec = pl.BlockSpec((1, P_NKEYS, P_DQ // 2), lambda i, h: (h, 0, 0))
    route_out = pl.BlockSpec((1, P_NKEYS, PEER_TR), lambda i, h: (h, 0, i))
    r2, e2, nd, e1n = pl.pallas_call(
        _peer_route_kernel,
        grid=(n // PEER_TR, P_HEADS),
        in_specs=[pl.BlockSpec((P_DQ, PEER_TR), lambda i, h: (h, i)), key_spec, key_spec],
        out_specs=[route_out] * 4,
        out_shape=[route_shape] * 4,
        compiler_params=pltpu.CompilerParams(
            dimension_semantics=("parallel", "parallel"), vmem_limit_bytes=VMEM_LIMIT_BYTES),
        name="peer_route",
    )(qt, sub_k1, sub_k2)
    route_in = pl.BlockSpec((P_HEADS, P_NKEYS, PEER_TM), lambda i, j: (0, 0, i),
                            pipeline_mode=pl.Buffered(1))
    n_tiles = n_exp // PEER_TE
    return pl.pallas_call(
        _peer_mix_kernel,
        grid=(n // PEER_TM, n_tiles + 1),
        in_specs=[pl.BlockSpec((PEER_TM, d), lambda i, j: (i, 0), pipeline_mode=pl.Buffered(1)),
                  pl.BlockSpec((PEER_TE, d), lambda i, j: (jnp.minimum(j, n_tiles - 1), 0)),
                  pl.BlockSpec((PEER_TE, d), lambda i, j: (jnp.maximum(j - 1, 0), 0)),
                  route_in, route_in, route_in, route_in],
        out_specs=pl.BlockSpec((PEER_TM, d), lambda i, j: (i, 0)),
        out_shape=jax.ShapeDtypeStruct((n, d), jnp.float32),
        scratch_shapes=[pltpu.VMEM((2, PEER_TE, PEER_TM), jnp.float32)],
        compiler_params=pltpu.CompilerParams(
            dimension_semantics=("parallel", "arbitrary"), vmem_limit_bytes=VMEM_LIMIT_BYTES),
        name="peer_mix",
    )(xb, exp_u_b, exp_v_b, r2, e2, nd, e1n)


RET_CHUNKS_PER_STEP = 8


def _retention_prompt_kernel(q_ref, k_ref, v_ref, gate_ref, cos_ref, sin_ref, dmask_ref, dq_ref, dk_ref,
                             cdec_ref, gn_ref, o_ref, state_ref, s_scr):
    f32, bf16 = jnp.float32, jnp.bfloat16
    c = pl.program_id(1)

    @pl.when(c == 0)
    def _():
        s_scr[...] = jnp.zeros_like(s_scr)

    half = R_DK // 2
    state = s_scr[...]
    for sub in range(RET_CHUNKS_PER_STEP):
        rows = slice(sub * R_CHUNK, (sub + 1) * R_CHUNK)
        cos, sin = cos_ref[rows, :], sin_ref[rows, :]

        def rot(x):
            x1, x2 = x[:, :half], x[:, half:]
            return jnp.concatenate([x1 * cos - x2 * sin, x1 * sin + x2 * cos], axis=1)

        q = (rot(q_ref[rows, :]) * R_DK ** -0.5).astype(bf16)
        k = rot(k_ref[rows, :])
        v = v_ref[rows, :].astype(bf16)
        inner = lax.dot_general(q, k.astype(bf16), (((1,), (1,)), ((), ())),
                                preferred_element_type=f32) * dmask_ref[0]
        o = jnp.dot(inner.astype(bf16), v, preferred_element_type=f32)
        o = o + jnp.dot(q, state.astype(bf16), preferred_element_type=f32) * dq_ref[0]
        kdec = (k * dk_ref[0]).astype(bf16)
        state = state * cdec_ref[0] + lax.dot_general(kdec, v, (((0,), (0,)), ((), ())),
                                                      preferred_element_type=f32)
        y = o * lax.rsqrt(jnp.mean(o * o, axis=-1, keepdims=True) + EPS) * gn_ref[0]
        gate = gate_ref[rows, :]
        o_ref[rows, :] = y * (gate * (1.0 / (1.0 + jnp.exp(-gate))))
    s_scr[...] = state

    @pl.when(c == pl.num_programs(1) - 1)
    def _():
        state_ref[0] = state


def retention_prompt_pallas(proj, r_gn):
    f32 = jnp.float32
    T = proj.shape[0]
    step_rows = RET_CHUNKS_PER_STEP * R_CHUNK
    assert T % step_rows == 0 and R_DK == R_DV
    half = R_DK // 2
    inv = ROPE_BASE ** (-jnp.arange(half, dtype=f32) / half)
    ang = jnp.arange(T, dtype=f32)[:, None] * inv[None, :]
    lg = jnp.log1p(-jnp.exp2(-5.0 - jnp.arange(R_HEADS, dtype=f32)))
    idx = jnp.arange(R_CHUNK, dtype=f32)
    rel = idx[:, None] - idx[None, :]
    dmask = jnp.where(rel[None] >= 0, jnp.exp(jnp.maximum(rel, 0.0)[None] * lg[:, None, None]), 0.0)
    dq = jnp.exp((idx + 1.0)[None, :] * lg[:, None])[:, :, None]
    dk = jnp.exp((R_CHUNK - 1.0 - idx)[None, :] * lg[:, None])[:, :, None]
    cdec = jnp.broadcast_to(jnp.exp(R_CHUNK * lg)[:, None, None], (R_HEADS, 1, R_DV))
    col = lambda base: pl.BlockSpec((step_rows, R_DK), lambda h, c, base=base: (c, base + h))
    per_head = lambda shape: pl.BlockSpec((1,) + shape, lambda h, c: (h, 0, 0))
    trig = pl.BlockSpec((step_rows, half), lambda h, c: (c, 0))
    return pl.pallas_call(
        _retention_prompt_kernel,
        grid=(R_HEADS, T // step_rows),
        in_specs=[col(0), col(R_HEADS), col(2 * R_HEADS), col(3 * R_HEADS), trig, trig,
                  per_head((R_CHUNK, R_CHUNK)), per_head((R_CHUNK, 1)), per_head((R_CHUNK, 1)),
                  per_head((1, R_DV)), per_head((1, R_DV))],
        out_specs=[pl.BlockSpec((step_rows, R_DV), lambda h, c: (c, h)),
                   pl.BlockSpec((1, R_DK, R_DV), lambda h, c: (h, 0, 0))],
        out_shape=[jax.ShapeDtypeStruct((T, R_W), f32),
                   jax.ShapeDtypeStruct((R_HEADS, R_DK, R_DV), f32)],
        scratch_shapes=[pltpu.VMEM((R_DK, R_DV), f32)],
        compiler_params=pltpu.CompilerParams(
            dimension_semantics=("parallel", "arbitrary"), vmem_limit_bytes=VMEM_LIMIT_BYTES),
        name="retention_prompt",
    )(proj, proj, proj, proj, jnp.cos(ang), jnp.sin(ang), dmask, dq, dk, cdec,
      r_gn.astype(f32).reshape(R_HEADS, 1, R_DV))


def rmsnorm(x, g):
    xf = x.astype(jnp.float32)
    y = xf * lax.rsqrt(jnp.mean(xf * xf, axis=-1, keepdims=True) + EPS)
    return (y * g.astype(jnp.float32)).astype(x.dtype)


def rotary(x, pos):
    half = x.shape[-1] // 2
    inv = ROPE_BASE ** (-jnp.arange(half, dtype=jnp.float32) / half)
    ang = pos.astype(jnp.float32)[:, None] * inv[None, :]
    cos, sin = jnp.cos(ang)[:, None, :], jnp.sin(ang)[:, None, :]
    xf = x.astype(jnp.float32)
    x1, x2 = xf[..., :half], xf[..., half:]
    return jnp.concatenate([x1 * cos - x2 * sin, x1 * sin + x2 * cos], axis=-1).astype(x.dtype)


def masked_softmax(s, mask):
    p = jax.nn.softmax(jnp.where(mask, s, NEG), axis=-1)
    return jnp.where(mask, p, 0.0)


def split_mix(h, w_in):
    B, T, _ = h.shape
    cuts = np.cumsum(IN_SIZES)[:-1].tolist()
    rq, rk, rv, rg, aq, akc, avc, aks, avs, akw, avw, ag = jnp.split(mmnd(h, w_in), cuts, axis=-1)
    r = lambda a, n, d: a.reshape(B, T, n, d)
    return (r(rq, R_HEADS, R_DK), r(rk, R_HEADS, R_DK), r(rv, R_HEADS, R_DV), rg,
            r(aq, A_HEADS, A_DH), r(akc, A_KV, A_DH), r(avc, A_KV, A_DH), r(aks, A_KV, A_DH),
            r(avs, A_KV, A_DH), r(akw, A_KV, A_DH), r(avw, A_KV, A_DH),
            jax.nn.sigmoid(ag.astype(jnp.float32)).reshape(B, T, A_HEADS, 3), ag)


def retention_chunk(S, q, k, v):
    q, k, v = q.astype(jnp.float32), k.astype(jnp.float32), v.astype(jnp.float32)
    C = q.shape[1]
    lg = jnp.log1p(-jnp.exp2(-5.0 - jnp.arange(R_HEADS, dtype=jnp.float32)))
    idx = jnp.arange(C, dtype=jnp.float32)
    rel = idx[:, None] - idx[None, :]
    dmask = jnp.where(rel[None] >= 0, jnp.exp(jnp.maximum(rel, 0.0)[None] * lg[:, None, None]), 0.0)
    inner = jnp.einsum('bihd,bjhd->bhij', q, k) * dmask[None]
    o = jnp.einsum('bhij,bjhe->bihe', inner, v)
    o = o + jnp.einsum('bihd,bhde->bihe', q, S) * jnp.exp((idx + 1.0)[:, None] * lg[None, :])[None, :, :, None]
    kdec = k * jnp.exp((C - 1.0 - idx)[:, None] * lg[None, :])[None, :, :, None]
    S_new = S * jnp.exp(C * lg)[None, :, None, None] + jnp.einsum('bjhd,bjhe->bhde', kdec, v)
    return S_new, o


def compress(rows, pe, w1, b1, w2, b2):
    T = rows.shape[-3]
    nc = T // L_CMP
    lead = rows.shape[:-3]
    blk = rows[..., :nc * L_CMP, :, :].reshape(lead + (nc, L_CMP, A_KV, A_DH)) + pe[:, None, :]
    blk = jnp.swapaxes(blk, -3, -2).reshape(lead + (nc, A_KV, L_CMP * A_DH))
    return mmnd(jax.nn.gelu(mmnd(blk, w1) + b1), w2) + b2


NSA_TK = 512
N_TOPK_NEG = -3.0e38


def _softmax_rows(s, mask):
    s = jnp.where(mask, s, NEG)
    m = jnp.max(s, axis=-1, keepdims=True)
    e = jnp.where(mask, jnp.exp(s - m), 0.0)
    l = jnp.sum(e, axis=-1, keepdims=True)
    return e * (1.0 / jnp.maximum(l, 1e-30))


def _top_n_mask_cols(score_t, n):
    rows = score_t.shape[0]
    row_id = lax.broadcasted_iota(jnp.int32, score_t.shape, 0).astype(jnp.float32)

    def one(_, c):
        sc, sel = c
        m = jnp.max(sc, axis=0, keepdims=True)
        first = jnp.min(jnp.where(sc == m, row_id, float(rows)), axis=0, keepdims=True)
        hit = row_id == first
        return jnp.where(hit, N_TOPK_NEG, sc), jnp.where(hit, 1.0, sel)

    _, sel = lax.fori_loop(0, n, one, (score_t, jnp.zeros_like(score_t)), unroll=True)
    return sel


def _nsa_prompt_kernel(q_ref, gate_ref, kc_ref, vc_ref, ks_ref, vs_ref, kw_ref, vw_ref, e_ref, o_ref,
                       *, n_win_keys):
    f32, bf16 = jnp.float32, jnp.bfloat16
    qb = pl.program_id(1)
    start = qb * NSA_QB
    nt_dims = (((1,), (1,)), ((), ()))
    q = q_ref[...]
    qg = jnp.concatenate([q[:, h * A_DH:(h + 1) * A_DH] for h in range(A_HPG)], axis=0)
    q_pos = start + lax.broadcasted_iota(jnp.int32, (NSA_QB, 1), 0)
    rep = lambda a: jnp.concatenate([a] * A_HPG, axis=0)

    kc, vc = kc_ref[0], vc_ref[0]
    nc = kc.shape[0]
    nsb = nc // 2
    s_c = lax.dot_general(qg, kc, nt_dims, preferred_element_type=f32)
    lane_c = lax.broadcasted_iota(jnp.int32, (1, nc), 1)
    c_orig = jnp.where(lane_c < nsb, 2 * lane_c, 2 * (lane_c - nsb) + 1)
    cmask = ((c_orig + 1) * L_CMP - 1) <= q_pos
    p_c = _softmax_rows(s_c, rep(cmask))
    o_c = jnp.dot(p_c.astype(bf16), vc, preferred_element_type=f32)

    ph = p_c[0:NSA_QB]
    for h in range(1, A_HPG):
        ph = ph + p_c[h * NSA_QB:(h + 1) * NSA_QB]
    imp = ph[:, :nsb] + ph[:, nsb:]
    blk = lax.broadcasted_iota(jnp.int32, (1, nsb), 1)
    cur = q_pos // L_SEL
    forced = (blk == 0) | (blk == cur) | (blk == cur - 1)
    imp = jnp.where(blk <= cur, jnp.where(forced, FORCE_SCORE, imp), NEG)
    sel = _top_n_mask_cols(imp.T, min(N_SEL, nsb)).T
    sel_b = sel.astype(bf16)

    n_tiles = (start + NSA_QB + NSA_TK - 1) // NSA_TK

    def sel_step(j, carry):
        m, l, acc = carry
        off = pl.multiple_of(j * NSA_TK, NSA_TK)
        k = ks_ref[0, pl.ds(off, NSA_TK), :]
        v = vs_ref[0, pl.ds(off, NSA_TK), :]
        s = lax.dot_general(qg, k, nt_dims, preferred_element_type=f32)
        picked = jnp.dot(sel_b, e_ref[j], preferred_element_type=f32)
        k_pos = off + lax.broadcasted_iota(jnp.int32, (1, NSA_TK), 1)
        bias = jnp.where((picked > 0.5) & (k_pos <= q_pos), 0.0, NEG)
        s = s + rep(bias)
        m_new = jnp.maximum(m, jnp.max(s, axis=-1, keepdims=True))
        alpha = jnp.exp(m - m_new)
        p = jnp.exp(s - m_new)
        l = alpha * l + jnp.sum(p, axis=-1, keepdims=True)
        acc = alpha * acc + jnp.dot(p.astype(bf16), v, preferred_element_type=f32)
        return m_new, l, acc

    def pair_step(p, carry):
        return sel_step(2 * p + 1, sel_step(2 * p, carry))

    rows = A_HPG * NSA_QB
    m_s, l_s, acc_s = lax.fori_loop(
        0, (n_tiles + 1) // 2, pair_step,
        (jnp.full((rows, 1), NEG, f32), jnp.zeros((rows, 1), f32), jnp.zeros((rows, A_DH), f32)))
    o_s = acc_s * (1.0 / l_s)

    base = pl.multiple_of(jnp.maximum(start + NSA_QB - n_win_keys, 0), NSA_QB)
    kw = kw_ref[0, pl.ds(base, n_win_keys), :]
    vw = vw_ref[0, pl.ds(base, n_win_keys), :]
    s_w = lax.dot_general(qg, kw, nt_dims, preferred_element_type=f32)
    rel = q_pos - (base + lax.broadcasted_iota(jnp.int32, (1, n_win_keys), 1))
    p_w = _softmax_rows(s_w, rep((rel >= 0) & (rel < WINDOW)))
    o_w = jnp.dot(p_w.astype(bf16), vw, preferred_element_type=f32)

    gl = gate_ref[0]
    g = 1.0 / (1.0 + jnp.exp(-gl))
    outs = []
    for h in range(A_HPG):
        r = slice(h * NSA_QB, (h + 1) * NSA_QB)
        outs.append(o_c[r] * g[:, 3 * h:3 * h + 1] + o_s[r] * g[:, 3 * h + 1:3 * h + 2]
                    + o_w[r] * g[:, 3 * h + 2:3 * h + 3])
    o_ref[...] = jnp.concatenate(outs, axis=-1)


def nsa_prompt_pallas(q, kc, vc, ks, vs, kw, vw, gate_logits):
    bf16 = jnp.bfloat16
    T = q.shape[0]
    nc = kc.shape[0]
    nsb = T // L_SEL
    assert T % (2 * NSA_TK) == 0 and nc == 2 * nsb
    n_win_keys = min(WINDOW + NSA_QB, T)
    qs = (q.astype(jnp.float32) * A_DH ** -0.5).astype(bf16).reshape(T, A_W)
    grp = lambda a: jnp.swapaxes(a, 0, 1).astype(bf16)
    perm = lambda a: jnp.concatenate([a[0::2], a[1::2]], axis=0)
    gl = jnp.swapaxes(gate_logits.reshape(T, A_KV, A_HPG * 3), 0, 1)
    n_t = T // NSA_TK
    key_blk = (jnp.arange(T, dtype=jnp.int32) // L_SEL).reshape(n_t, 1, NSA_TK)
    expand = (jnp.arange(nsb, dtype=jnp.int32)[None, :, None] == key_blk).astype(bf16)
    kv_spec = lambda rows: pl.BlockSpec((1, rows, A_DH), lambda g, i: (g, 0, 0))
    return pl.pallas_call(
        functools.partial(_nsa_prompt_kernel, n_win_keys=n_win_keys),
        grid=(A_KV, T // NSA_QB),
        in_specs=[pl.BlockSpec((NSA_QB, A_HPG * A_DH), lambda g, i: (i, g)),
                  pl.BlockSpec((1, NSA_QB, A_HPG * 3), lambda g, i: (g, i, 0)),
                  kv_spec(nc), kv_spec(nc), kv_spec(T), kv_spec(T), kv_spec(T), kv_spec(T),
                  pl.BlockSpec((n_t, nsb, NSA_TK), lambda g, i: (0, 0, 0))],
        out_specs=pl.BlockSpec((NSA_QB, A_HPG * A_DH), lambda g, i: (i, g)),
        out_shape=jax.ShapeDtypeStruct((T, A_W), jnp.float32),
        compiler_params=pltpu.CompilerParams(
            dimension_semantics=("parallel", "arbitrary"),
            vmem_limit_bytes=VMEM_LIMIT_BYTES),
        name="nsa_prompt",
    )(qs, gl, grp(perm(kc)), grp(perm(vc)), grp(ks), grp(vs), grp(kw), grp(vw), expand)


SQ_PAD = 8
SEL_TAIL = 128


CMP_CHUNK = 8


def _compress_seq(src, pe_ref, perm_ref, w1_ref, b1_ref, w2_ref, b2_ref, n_blk):
    f32, bf16 = jnp.float32, jnp.bfloat16
    chunk_rows = CMP_CHUNK * L_CMP
    regrouped = []
    chunk = lambda g, j: (src[g, j * chunk_rows:(j + 1) * chunk_rows, :] + pe_ref[...]).astype(bf16)
    for g in range(A_KV):
        for j in range(0, n_blk // CMP_CHUNK, 2):
            y = jnp.dot(perm_ref[...], jnp.concatenate([chunk(g, j), chunk(g, j + 1)], axis=1),
                        preferred_element_type=f32)
            regrouped.extend([y[:, :A_DH], y[:, A_DH:]])
    acc = jnp.zeros((A_KV * n_blk, CMP_HID), f32)
    row_l = lambda l: jnp.concatenate([y[l * CMP_CHUNK:(l + 1) * CMP_CHUNK] for y in regrouped], axis=0)
    for l in range(0, L_CMP, 2):
        xg = jnp.concatenate([row_l(l), row_l(l + 1)], axis=1).astype(bf16)
        acc = acc + jnp.dot(xg, w1_ref[l * A_DH:(l + 2) * A_DH, :], preferred_element_type=jnp.float32)
    hid = jax.nn.gelu(acc + b1_ref[...])
    return jnp.dot(hid.astype(bf16), w2_ref[...], preferred_element_type=jnp.float32) + b2_ref[...]


def _nsa_sample_kernel(pt_ref, kcp_ref, vcp_ref, ksp_ref, vsp_ref, kwb_ref, vwb_ref, q_ref, gate_ref,
                       ksn_ref, vsn_ref, kwn_ref, vwn_ref, pek_ref, pev_ref, perm_ref,
                       w1k_ref, b1k_ref, w2k_ref, b2k_ref, w1v_ref, b1v_ref, w2v_ref, b2v_ref, gkc_ref,
                       e_ref, o_ref, kc_scr, vc_scr, ks_scr, vs_scr, kw_scr, vw_scr, sem,
                       *, past, n_pages, n_new):
    f32, bf16 = jnp.float32, jnp.bfloat16
    b = pl.program_id(0)
    slot = b % 2

    def seq_copies(seq, dst_slot):
        cps = []
        for pg in range(n_pages):
            page = pt_ref[seq, pg]
            rows = pl.ds(pg * PAGE_SIZE, PAGE_SIZE)
            for g in range(A_KV):
                for pool_ref, scr in ((kcp_ref, kc_scr), (vcp_ref, vc_scr), (ksp_ref, ks_scr), (vsp_ref, vs_scr)):
                    cps.append(pltpu.make_async_copy(pool_ref.at[page, :, g, :], scr.at[dst_slot, g, rows, :],
                                                     sem.at[dst_slot]))
        for g in range(A_KV):
            for buf_ref, scr in ((kwb_ref, kw_scr), (vwb_ref, vw_scr)):
                cps.append(pltpu.make_async_copy(buf_ref.at[seq, :, g, :], scr.at[dst_slot, g, pl.ds(0, wb), :],
                                                 sem.at[dst_slot]))
        return cps

    wb = kwb_ref.shape[1]

    @pl.when(b == 0)
    def _():
        for cp in seq_copies(0, 0):
            cp.start()

    @pl.when(b + 1 < pl.num_programs(0))
    def _():
        for cp in seq_copies(b + 1, 1 - slot):
            cp.start()

    for cp in seq_copies(b, slot):
        cp.wait()

    def compute():
        nt_dims = (((1,), (1,)), ((), ()))
        n_blk = past // L_CMP
        half = n_blk // 2
        rows = A_HPG * SQ_PAD
        pad_rows = jnp.zeros((SEL_TAIL - SQ_PAD, A_DH), f32)
        tail = lambda new_ref, g: jnp.concatenate(
            [new_ref[0, :, g * A_DH:(g + 1) * A_DH], pad_rows], axis=0)
        for g in range(A_KV):
            ks_scr[slot, g, pl.ds(past, SEL_TAIL), :] = tail(ksn_ref, g)
            vs_scr[slot, g, pl.ds(past, SEL_TAIL), :] = tail(vsn_ref, g)
            kw_scr[slot, g, pl.ds(wb, SEL_TAIL), :] = tail(kwn_ref, g)
            vw_scr[slot, g, pl.ds(wb, SEL_TAIL), :] = tail(vwn_ref, g)
        kcmp = _compress_seq(kc_scr.at[slot], pek_ref, perm_ref, w1k_ref, b1k_ref, w2k_ref, b2k_ref, n_blk)
        kcmp = kcmp * lax.rsqrt(jnp.mean(kcmp * kcmp, axis=-1, keepdims=True) + EPS) * gkc_ref[...]
        vcmp = _compress_seq(vc_scr.at[slot], pev_ref, perm_ref, w1v_ref, b1v_ref, w2v_ref, b2v_ref, n_blk)

        q_pos = past + (lax.broadcasted_iota(jnp.int32, (SQ_PAD, 1), 0))
        rep = lambda a: jnp.concatenate([a] * A_HPG, axis=0)
        lanes = P_NKEYS
        zero_blk = jnp.zeros((lanes - n_blk, A_DH), f32)
        lane_c = lax.broadcasted_iota(jnp.int32, (1, lanes), 1)
        cmask = jnp.broadcast_to(lane_c < n_blk, (rows, lanes))

        o_c, imps = [], []
        for g in range(A_KV):
            qg = q_ref[0, g]
            kc_g = jnp.concatenate([kcmp[g * n_blk:(g + 1) * n_blk], zero_blk], axis=0).astype(bf16)
            vc_g = jnp.concatenate([vcmp[g * n_blk:(g + 1) * n_blk], zero_blk], axis=0).astype(bf16)
            s_c = lax.dot_general(qg, kc_g, nt_dims, preferred_element_type=f32)
            p_c = _softmax_rows(s_c, cmask)
            o_c.append(jnp.dot(p_c.astype(bf16), vc_g, preferred_element_type=f32))
            ph = p_c[0:SQ_PAD]
            for h in range(1, A_HPG):
                ph = ph + p_c[h * SQ_PAD:(h + 1) * SQ_PAD]
            imps.append(ph + pltpu.roll(ph, lanes - 1, axis=1))
        imp = jnp.concatenate(imps, axis=0)
        blk = lane_c // 2
        q_pos4 = jnp.concatenate([q_pos] * A_KV, axis=0)
        cur = q_pos4 // L_SEL
        forced = (blk == 0) | (blk == cur) | (blk == cur - 1)
        imp = jnp.where(blk <= cur, jnp.where(forced, FORCE_SCORE, imp), NEG)
        imp = jnp.where(lane_c % 2 == 0, imp, N_TOPK_NEG)
        nsb = -(-(past + n_new) // L_SEL)
        imp_sq = jnp.concatenate([imp, jnp.full((lanes - imp.shape[0], lanes), N_TOPK_NEG, f32)], axis=0)
        sel = _top_n_mask_cols(imp_sq.T, min(N_SEL, nsb)).T[:imp.shape[0]]
        picked = jnp.dot(sel.astype(bf16), e_ref[...], preferred_element_type=f32)
        n_keys = past + SEL_TAIL
        k_pos = lax.broadcasted_iota(jnp.int32, (1, n_keys), 1)
        w_pos = past - wb + lax.broadcasted_iota(jnp.int32, (1, wb + SEL_TAIL), 1)
        rel = q_pos - w_pos
        w_mask = rep((rel >= 0) & (rel < WINDOW) & (w_pos >= 0))

        gl = gate_ref[0]
        outs = []
        for g in range(A_KV):
            qg = q_ref[0, g]
            s_mask = rep((picked[g * SQ_PAD:(g + 1) * SQ_PAD] > 0.5) & (k_pos <= q_pos))
            s_s = lax.dot_general(qg, ks_scr[slot, g].astype(bf16), nt_dims, preferred_element_type=f32)
            p_s = _softmax_rows(s_s, s_mask)
            o_s = jnp.dot(p_s.astype(bf16), vs_scr[slot, g].astype(bf16), preferred_element_type=f32)
            s_w = lax.dot_general(qg, kw_scr[slot, g].astype(bf16), nt_dims, preferred_element_type=f32)
            p_w = _softmax_rows(s_w, w_mask)
            o_w = jnp.dot(p_w.astype(bf16), vw_scr[slot, g].astype(bf16), preferred_element_type=f32)
            gs = 1.0 / (1.0 + jnp.exp(-gl[g]))
            o = o_c[g] * gs[:, 0:1] + o_s * gs[:, 1:2] + o_w * gs[:, 2:3]
            outs.extend([o[h * SQ_PAD:(h + 1) * SQ_PAD] for h in range(A_HPG)])
        o_ref[0] = jnp.concatenate(outs, axis=1)

    compute()


def nsa_sample_pallas(q, ksn, vsn, kwn, vwn, gate_logits, pool_kc, pool_vc, pool_ks, pool_vs,
                      buf_kw, buf_vw, page_table, cmp_k, cmp_v, g_kc):
    f32, bf16 = jnp.float32, jnp.bfloat16
    B, S = q.shape[:2]
    assert S <= SQ_PAD and S < L_CMP
    n_pages = page_table.shape[1]
    past = n_pages * PAGE_SIZE
    wb = buf_kw.shape[1]
    kvw = A_KV * A_DH
    pad_q = lambda a: jnp.pad(a, ((0, 0), (0, SQ_PAD - S)) + ((0, 0),) * (a.ndim - 2))
    qs = pad_q((q.astype(f32) * A_DH ** -0.5).astype(bf16)).reshape(B, SQ_PAD, A_KV, A_HPG, A_DH)
    qs = qs.transpose(0, 2, 3, 1, 4).reshape(B, A_KV, A_HPG * SQ_PAD, A_DH)
    gl = pad_q(gate_logits).reshape(B, SQ_PAD, A_KV, A_HPG, 3).transpose(0, 2, 3, 1, 4)
    gl = gl.reshape(B, A_KV, A_HPG * SQ_PAD, 3)
    new = lambda a: pad_q(a.reshape(B, S, kvw))
    pe_k, w1_k, b1_k, w2_k, b2_k = cmp_k
    pe_v, w1_v, b1_v, w2_v, b2_v = cmp_v
    row = lambda a: a.reshape(1, -1).astype(f32)
    n_keys = past + SEL_TAIL
    lane = jnp.arange(P_NKEYS, dtype=jnp.int32)[:, None]
    key_blk = (jnp.arange(n_keys, dtype=jnp.int32) // L_SEL)[None, :]
    expand = ((lane % 2 == 0) & (lane // 2 == key_blk)).astype(bf16)
    chunk_rows = CMP_CHUNK * L_CMP
    dst = jnp.arange(chunk_rows, dtype=jnp.int32)
    src_row = (dst % CMP_CHUNK) * L_CMP + dst // CMP_CHUNK
    perm = (src_row[:, None] == jnp.arange(chunk_rows, dtype=jnp.int32)[None, :]).astype(bf16)
    tile_pe = lambda pe: jnp.tile(pe.astype(f32), (CMP_CHUNK, 1))

    hbm = pl.BlockSpec(memory_space=pl.ANY)
    seq3 = lambda r, c: pl.BlockSpec((1, r, c), lambda b, pt: (b, 0, 0))
    seq4 = lambda a, r, c: pl.BlockSpec((1, a, r, c), lambda b, pt: (b, 0, 0, 0))
    full = lambda a: pl.BlockSpec(a.shape, lambda b, pt: (0,) * a.ndim, pipeline_mode=pl.Buffered(1))
    consts = [tile_pe(pe_k), tile_pe(pe_v), perm, w1_k.astype(bf16), row(b1_k), w2_k.astype(bf16), row(b2_k),
              w1_v.astype(bf16), row(b1_v), w2_v.astype(bf16), row(b2_v), row(g_kc), expand]
    slots = 2
    out = pl.pallas_call(
        functools.partial(_nsa_sample_kernel, past=past, n_pages=n_pages, n_new=S),
        grid_spec=pltpu.PrefetchScalarGridSpec(
            num_scalar_prefetch=1,
            grid=(B,),
            in_specs=[hbm] * 6
                     + [seq4(A_KV, A_HPG * SQ_PAD, A_DH), seq4(A_KV, A_HPG * SQ_PAD, 3)]
                     + [seq3(SQ_PAD, kvw)] * 4
                     + [full(c) for c in consts],
            out_specs=pl.BlockSpec((1, SQ_PAD, A_W), lambda b, pt: (b, 0, 0)),
            scratch_shapes=[pltpu.VMEM((slots, A_KV, past, A_DH), f32), pltpu.VMEM((slots, A_KV, past, A_DH), f32),
                            pltpu.VMEM((slots, A_KV, n_keys, A_DH), f32), pltpu.VMEM((slots, A_KV, n_keys, A_DH), f32),
                            pltpu.VMEM((slots, A_KV, wb + SEL_TAIL, A_DH), f32),
                            pltpu.VMEM((slots, A_KV, wb + SEL_TAIL, A_DH), f32),
                            pltpu.SemaphoreType.DMA((slots,))]),
        out_shape=jax.ShapeDtypeStruct((B, SQ_PAD, A_W), f32),
        compiler_params=pltpu.CompilerParams(
            dimension_semantics=("arbitrary",), vmem_limit_bytes=VMEM_LIMIT_BYTES),
        name="nsa_sample",
    )(page_table, pool_kc, pool_vc, pool_ks, pool_vs, buf_kw, buf_vw, qs, gl,
      new(ksn), new(vsn), new(kwn), new(vwn), *consts)
    return out[:, :S]


def merge_groups(ret_o, rg, nsa_o, r_gn, w_out):
    B, T = rg.shape[:2]
    r = rmsnorm(ret_o, r_gn).reshape(B, T, R_W) * jax.nn.silu(rg.astype(jnp.float32))
    mix = jnp.concatenate([r.astype(rg.dtype), nsa_o.reshape(B, T, A_W).astype(rg.dtype)], axis=-1)
    return mmnd(mix, w_out)


def memory_kv(mem, g_memtok, w_mk, w_mv, m_kn):
    B, M, _ = mem.shape
    m = rmsnorm(mem, g_memtok)
    k = rmsnorm(mmnd(m, w_mk).reshape(B, M, M_HEADS, M_DH), m_kn)
    v = mmnd(m, w_mv).reshape(B, M, M_HEADS, M_DH)
    return k, v


def memory_attend(h, mk, mv, w_mq, m_qn, w_mo):
    B, T, _ = h.shape
    q = rmsnorm(mmnd(h, w_mq).reshape(B, T, M_HEADS, M_DH), m_qn)
    s = jnp.einsum('bthd,bmhd->bhtm', q.astype(jnp.float32), mk.astype(jnp.float32)) * M_DH ** -0.5
    p = jax.nn.softmax(s, axis=-1)
    o = jnp.einsum('bhtm,bmhd->bthd', p, mv.astype(jnp.float32)).astype(h.dtype)
    return mmnd(o.reshape(B, T, M_W), w_mo)


def kernel(x_prompt, x_sample, mem_prompt, cache_k_cmp, cache_v_cmp, cache_k_sel, cache_v_sel, state_k_win, state_v_win, state_ret, cache_mem_k, cache_mem_v, page_table, g_mix, w_in, r_gn, a_qn, a_kcn, a_ksn, a_kwn, cmp_pe_k, cmp_w1_k, cmp_b1_k, cmp_w2_k, cmp_b2_k, cmp_pe_v, cmp_w1_v, cmp_b1_v, cmp_w2_v, cmp_b2_v, w_out, g_xattn, g_memtok, w_mq, w_mk, w_mv, m_qn, m_kn, w_mo, g_ffn, w_pq, sub_k1, sub_k2, exp_u, exp_v):
    cmp_k = (cmp_pe_k, cmp_w1_k, cmp_b1_k, cmp_w2_k, cmp_b2_k)
    cmp_v = (cmp_pe_v, cmp_w1_v, cmp_b1_v, cmp_w2_v, cmp_b2_v)
    xp, xs = x_prompt, x_sample

    T = xp.shape[1]
    w_in = w_in.astype(jnp.bfloat16)
    w_out = w_out.astype(jnp.bfloat16)
    assert xp.shape[0] == 1
    proj_p = mm(rmsnorm(xp, g_mix).astype(jnp.bfloat16)[0], w_in)
    ret_mix, ret_p = retention_prompt_pallas(proj_p, r_gn)
    ret_p = ret_p[None]
    cuts = np.cumsum(IN_SIZES)[3:-1].tolist()
    aq, akc, avc, aks, avs, akw, avw, ag_p = jnp.split(proj_p[None, :, cuts[0]:], [c - cuts[0] for c in cuts[1:]], axis=-1)
    kv = lambda a: a.reshape(1, T, A_KV, A_DH)
    aq, akc, avc, aks, avs, akw, avw = aq.reshape(1, T, A_HEADS, A_DH), kv(akc), kv(avc), kv(aks), kv(avs), kv(akw), kv(avw)
    aq, aks, akw = rmsnorm(aq, a_qn), rmsnorm(aks, a_ksn), rmsnorm(akw, a_kwn)
    kc = rmsnorm(compress(akc, *cmp_k), a_kcn)
    vc = compress(avc, *cmp_v)
    nsa_o = nsa_prompt_pallas(aq[0], kc[0], vc[0], aks[0], avs[0], akw[0], avw[0], ag_p[0])
    hp = xp + mm(jnp.concatenate([ret_mix, nsa_o], axis=-1).astype(jnp.bfloat16), w_out)[None]
    mem_k_p, mem_v_p = memory_kv(mem_prompt, g_memtok, w_mk, w_mv, m_kn)
    hp = hp + memory_attend(rmsnorm(hp, g_xattn), mem_k_p, mem_v_p, w_mq, m_qn, w_mo)
    wl = min(WINDOW, T)
    k_cmp_p, v_cmp_p, k_sel_p, v_sel_p = akc, avc, aks, avs
    k_win_p, v_win_p = akw[:, T - wl:], avw[:, T - wl:]
    ret_p = ret_p.astype(xp.dtype)

    S = xs.shape[1]
    past = page_table.shape[1] * PAGE_SIZE
    rq, rk, rv, rg, aq, akc, avc, aks, avs, akw, avw, ag, ag_s = split_mix(
        rmsnorm(xs, g_mix).astype(jnp.bfloat16), w_in)
    pos = past + jnp.arange(S)
    ret_s, ret_o = retention_chunk(state_ret.astype(jnp.float32), rotary(rq, pos) * R_DK ** -0.5,
                                   rotary(rk, pos), rv)
    aq, aks, akw = rmsnorm(aq, a_qn), rmsnorm(aks, a_ksn), rmsnorm(akw, a_kwn)
    nsa_o = nsa_sample_pallas(aq, aks, avs, akw, avw, ag_s, cache_k_cmp, cache_v_cmp, cache_k_sel,
                              cache_v_sel, state_k_win, state_v_win, page_table, cmp_k, cmp_v, a_kcn)
    hs = xs + merge_groups(ret_o, rg, nsa_o, r_gn, w_out)
    hs = hs + memory_attend(rmsnorm(hs, g_xattn), cache_mem_k, cache_mem_v, w_mq, m_qn, w_mo)
    n_p = hp.shape[0] * hp.shape[1]
    tokens = jnp.concatenate([rmsnorm(hp, g_ffn).reshape(-1, D_MODEL), rmsnorm(hs, g_ffn).reshape(-1, D_MODEL)], axis=0)
    y = peer_pallas(tokens, w_pq, sub_k1, sub_k2, exp_u.astype(jnp.bfloat16), exp_v.astype(jnp.bfloat16))
    xp = hp + y[:n_p].reshape(hp.shape)
    xs = hs + y[n_p:].reshape(hs.shape)
    wb = state_k_win.shape[1]
    k_cmp_s, v_cmp_s, k_sel_s, v_sel_s = akc, avc, aks, avs
    k_win_s = jnp.concatenate([state_k_win, akw.astype(state_k_win.dtype)], axis=1)[:, -wb:]
    v_win_s = jnp.concatenate([state_v_win, avw.astype(state_v_win.dtype)], axis=1)[:, -wb:]
    ret_s = ret_s.astype(state_ret.dtype)
    return (xp, xs, k_cmp_p, v_cmp_p, k_sel_p, v_sel_p, k_win_p, v_win_p, ret_p, mem_k_p, mem_v_p,
            k_cmp_s, v_cmp_s, k_sel_s, v_sel_s, k_win_s, v_win_s, ret_s)
```
